```python
import math
import jax, jax.numpy as jnp
from jax import lax
import numpy as np

D_MODEL = 1024
BATCH = 1
SEQ = 16384
DEPTH = 1
DEC_BATCH = 128
DEC_SEQ = 1
PAST_LEN = 8192
PAGE_SIZE = 128

HEAD_DIM = 64
SSM_WIDTH = D_MODEL // 2
ATT_WIDTH = D_MODEL - SSM_WIDTH
M_HEADS = SSM_WIDTH // HEAD_DIM
M_GROUPS = 2
D_STATE = 128
CONV_W = 4
CONV_DIM = SSM_WIDTH + 2 * M_GROUPS * D_STATE
SSD_CHUNK = 256
RMS_EPS = 1e-6
N_HEADS = ATT_WIDTH // HEAD_DIM
N_KV = 2
GQA = N_HEADS // N_KV
ROT_DIM = HEAD_DIM // 4
ROPE_THETA = 500000.0
L_CMP = 32
STRIDE = 16
CMP_RATIO = L_CMP // STRIDE
CMP_HIDDEN = 2 * HEAD_DIM
L_SEL = 64
N_SEL = 16
N_LOCAL = 2
WINDOW = 512
Q_BLOCK = 128
ATT_SCALE = HEAD_DIM ** -0.5
D_FF = 4 * D_MODEL
LN_EPS = 1e-5
DEEPNORM_ALPHA = (2.0 * DEPTH) ** 0.25
DEEPNORM_BETA = (8.0 * DEPTH) ** -0.25
IN_COLS = SSM_WIDTH + CONV_DIM + M_HEADS + ATT_WIDTH + 6 * N_KV * HEAD_DIM + 3 * N_HEADS

kernel_name = "hymba_ssd_nsa_decoder_step"


def layer_norm(x, g, b):
    xf = x.astype(jnp.float32)
    mu = jnp.mean(xf, -1, keepdims=True)
    var = jnp.mean(jnp.square(xf - mu), -1, keepdims=True)
    return ((xf - mu) * lax.rsqrt(var + LN_EPS) * g + b).astype(x.dtype)


def rms_norm(x, g):
    xf = x.astype(jnp.float32)
    return (xf * lax.rsqrt(jnp.mean(jnp.square(xf), -1, keepdims=True) + RMS_EPS) * g).astype(x.dtype)


def partial_rope(x, pos):
    half = ROT_DIM // 2
    inv = ROPE_THETA ** (-jnp.arange(half, dtype=jnp.float32) * 2.0 / ROT_DIM)
    ang = pos.astype(jnp.float32)[:, None] * inv[None, :]
    cos = jnp.cos(ang)[:, None, :]
    sin = jnp.sin(ang)[:, None, :]
    xr = x[..., :ROT_DIM].astype(jnp.float32)
    x1, x2 = xr[..., :half], xr[..., half:]
    rot = jnp.concatenate([x1 * cos - x2 * sin, x2 * cos + x1 * sin], -1).astype(x.dtype)
    return jnp.concatenate([rot, x[..., ROT_DIM:]], -1)


def masked_softmax(s, mask):
    s = jnp.where(mask, s, -jnp.inf)
    m = jnp.max(s, -1, keepdims=True)
    m = jnp.where(jnp.isfinite(m), m, 0.0)
    e = jnp.exp(s - m)
    return e / jnp.maximum(jnp.sum(e, -1, keepdims=True), 1e-30)


def split_proj(proj):
    bsz, t, _ = proj.shape
    sizes = (SSM_WIDTH, CONV_DIM, M_HEADS, ATT_WIDTH, 6 * N_KV * HEAD_DIM, 3 * N_HEADS)
    parts, start = [], 0
    for size in sizes:
        parts.append(proj[..., start:start + size])
        start += size
    z, xbc, dt_raw, q, kv6, g = parts
    q = q.reshape(bsz, t, N_HEADS, HEAD_DIM)
    kv6 = kv6.reshape(bsz, t, 6, N_KV, HEAD_DIM)
    gates = jax.nn.sigmoid(g.astype(jnp.float32)).astype(proj.dtype).reshape(bsz, t, N_KV, GQA, 3)
    return z, xbc, dt_raw, q, kv6, gates


def causal_conv(u, buf, w, b):
    t = u.shape[1]
    up = jnp.concatenate([buf.astype(u.dtype), u], 1)
    y = up[:, 0:t] * w[0]
    for k in range(1, CONV_W):
        y = y + up[:, k:k + t] * w[k]
    return y + b, up[:, up.shape[1] - (CONV_W - 1):]


def ssd_scan(xs, dt, a, bg, cg, h0):
    bsz, t, nh, p = xs.shape
    q = math.gcd(t, SSD_CHUNK)
    nc = t // q
    rep = nh // M_GROUPS
    bh = jnp.repeat(bg, rep, axis=2).reshape(bsz, nc, q, nh, D_STATE)
    ch = jnp.repeat(cg, rep, axis=2).reshape(bsz, nc, q, nh, D_STATE)
    xc = (xs * dt[..., None]).reshape(bsz, nc, q, nh, p)
    cum = jnp.cumsum((dt * a).reshape(bsz, nc, q, nh), axis=2)
    causal = jnp.tril(jnp.ones((q, q), dtype=bool))[None, None, :, :, None]
    seg = cum[:, :, :, None, :] - cum[:, :, None, :, :]
    decay = jnp.where(causal, jnp.exp(jnp.where(causal, seg, 0.0)), 0.0)
    scores = jnp.einsum('bclhn,bcshn->bclsh', ch, bh) * decay
    y_diag = jnp.einsum('bclsh,bcshp->bclhp', scores, xc)
    to_end = jnp.exp(cum[:, :, -1:, :] - cum)
    chunk_states = jnp.einsum('bclhn,bclh,bclhp->bchpn', bh, to_end, xc)
    chunk_decay = jnp.exp(cum[:, :, -1, :])

    def step(h, inp):
        dec, st = inp
        return dec[:, :, None, None] * h + st, h

    h_last, h_prev = lax.scan(step, h0, (jnp.moveaxis(chunk_decay, 1, 0), jnp.moveaxis(chunk_states, 1, 0)))
    h_prev = jnp.moveaxis(h_prev, 0, 1)
    y_off = jnp.einsum('bclhn,bchpn,bclh->bclhp', ch, h_prev, jnp.exp(cum))
    return (y_diag + y_off).reshape(bsz, t, nh, p), h_last


def mamba_group(z, xbc, dt_raw, conv_buf, h0, conv_w, conv_b, dt_bias, a_log, d_skip, norm_g):
    bsz, t, _ = z.shape
    xbc, new_buf = causal_conv(xbc, conv_buf, conv_w, conv_b)
    xbc = jax.nn.silu(xbc.astype(jnp.float32))
    nb = M_GROUPS * D_STATE
    xs = xbc[..., :SSM_WIDTH].reshape(bsz, t, M_HEADS, HEAD_DIM)
    bg = xbc[..., SSM_WIDTH:SSM_WIDTH + nb].reshape(bsz, t, M_GROUPS, D_STATE)
    cg = xbc[..., SSM_WIDTH + nb:].reshape(bsz, t, M_GROUPS, D_STATE)
    dt = jax.nn.softplus(dt_raw.astype(jnp.float32) + dt_bias.astype(jnp.float32))
    a = -jnp.exp(a_log.astype(jnp.float32))
    y, h_new = ssd_scan(xs, dt, a, bg, cg, h0.astype(jnp.float32))
    y = y + d_skip.astype(jnp.float32)[:, None] * xs
    y = y.reshape(bsz, t, SSM_WIDTH) * jax.nn.silu(z.astype(jnp.float32))
    yg = y.reshape(bsz, t, M_GROUPS, SSM_WIDTH // M_GROUPS)
    yg = yg * lax.rsqrt(jnp.mean(jnp.square(yg), -1, keepdims=True) + RMS_EPS)
    y = yg.reshape(bsz, t, SSM_WIDTH) * norm_g
    return y.astype(z.dtype), new_buf, h_new.astype(h0.dtype)


def subblock_proj(rows, w1):
    bsz, t = rows.shape[:2]
    sub = rows.reshape(bsz, t // STRIDE, STRIDE, N_KV, HEAD_DIM)
    return jnp.einsum('bsjkd,rjdf->bsrkf', sub, w1.reshape(CMP_RATIO, STRIDE, HEAD_DIM, CMP_HIDDEN))


def finish_compress(h, pe, w1, b1, w2, b2):
    n_cmp = h.shape[1] - CMP_RATIO + 1
    pre = b1 + jnp.einsum('jd,jdf->f', pe, w1)
    for r in range(CMP_RATIO):
        pre = pre + h[:, r:r + n_cmp, r]
    return jnp.einsum('bckf,fd->bckd', jax.nn.silu(pre), w2) + b2


def cmp_branch(q, kc, vc, qp):
    s = jnp.einsum('bqkgd,bckd->bqkgc', q, kc, preferred_element_type=jnp.float32) * ATT_SCALE
    end = jnp.arange(kc.shape[1], dtype=jnp.int32) * STRIDE + (L_CMP - 1)
    mask = (end[None, :] <= qp[:, None])[None, :, None, None, :]
    p = masked_softmax(s, mask)
    o = jnp.einsum('bqkgc,bckd->bqkgd', p.astype(vc.dtype), vc)
    return o, jnp.sum(p, axis=3)


def select_blocks(imp, qp, n_slc):
    nsb = L_SEL // STRIDE
    n_taps = nsb + CMP_RATIO - 1
    nc = imp.shape[-1]
    span = nsb * (n_slc - 1) + 1
    back = max(0, n_taps - 1 + span - nc - (CMP_RATIO - 1))
    padded = jnp.pad(imp, ((0, 0), (0, 0), (0, 0), (CMP_RATIO - 1, back)))
    score = None
    for k in range(n_taps):
        lo = STRIDE * (k - (CMP_RATIO - 1))
        wgt = float(min(lo + L_CMP, L_SEL) - max(lo, 0))
        term = wgt * padded[..., k:k + span:nsb]
        score = term if score is None else score + term
    blk = jnp.arange(n_slc, dtype=jnp.int32)[None, :]
    cur = (qp // L_SEL)[:, None]
    valid = blk * L_SEL <= qp[:, None]
    forced = (blk == 0) | ((blk <= cur) & (blk > cur - N_LOCAL))
    score = jnp.where(forced[None, :, None, :], jnp.inf, jnp.where(valid[None, :, None, :], score, -jnp.inf))
    _, idx = lax.top_k(score, min(N_SEL, n_slc))
    return idx


def sel_branch(q_rot, ks_g, vs_g, idx, qp):
    bsz, nq, nkv, nt, ls, hd = ks_g.shape
    kpos = idx[..., None] * L_SEL + jnp.arange(L_SEL, dtype=jnp.int32)
    mask = (kpos <= qp[None, :, None, None, None]).reshape(bsz, nq, nkv, 1, nt * ls)
    s = jnp.einsum('bqkgd,bqknld->bqkgnl', q_rot, ks_g, preferred_element_type=jnp.float32) * ATT_SCALE
    p = masked_softmax(s.reshape(bsz, nq, nkv, GQA, nt * ls), mask)
    return jnp.einsum('bqkgm,bqkmd->bqkgd', p.astype(vs_g.dtype), vs_g.reshape(bsz, nq, nkv, nt * ls, hd))


def win_branch(q_rot, kw, vw, kpos, qp):
    s = jnp.einsum('bqkgd,bmkd->bqkgm', q_rot, kw, preferred_element_type=jnp.float32) * ATT_SCALE
    mask = (kpos[None, :] <= qp[:, None]) & (kpos[None, :] > qp[:, None] - WINDOW) & (kpos[None, :] >= 0)
    p = masked_softmax(s, mask[None, :, None, None, :])
    return jnp.einsum('bqkgm,bmkd->bqkgd', p.astype(vw.dtype), vw)


def nsa_core(qb, qp, gb, kcc, vcc, gather_sel, n_slc, kw, vw, kpos_w):
    bsz, nq = qb.shape[:2]
    q_un = qb.reshape(bsz, nq, N_KV, GQA, HEAD_DIM)
    q_rot = partial_rope(qb, qp).reshape(bsz, nq, N_KV, GQA, HEAD_DIM)
    o_cmp, imp = cmp_branch(q_un, kcc, vcc, qp)
    idx = select_blocks(imp, qp, n_slc)
    ks_g, vs_g = gather_sel(idx)
    o_sel = sel_branch(q_rot, ks_g, vs_g, idx, qp)
    o_win = win_branch(q_rot, kw, vw, kpos_w, qp)
    o = gb[..., 0:1] * o_cmp + gb[..., 1:2] * o_sel + gb[..., 2:3] * o_win
    return o.reshape(bsz, nq, ATT_WIDTH)


def nsa_prompt(q, gates, kcc, vcc, ks, vs, kw, vw):
    bsz, seq = q.shape[:2]
    n_slc = seq // L_SEL
    ksb = ks.reshape(bsz, n_slc, L_SEL, N_KV, HEAD_DIM).transpose(0, 1, 3, 2, 4)
    vsb = vs.reshape(bsz, n_slc, L_SEL, N_KV, HEAD_DIM).transpose(0, 1, 3, 2, 4)
    kwp = jnp.pad(kw, ((0, 0), (WINDOW, 0), (0, 0), (0, 0)))
    vwp = jnp.pad(vw, ((0, 0), (WINDOW, 0), (0, 0), (0, 0)))
    bidx = jnp.arange(bsz)[:, None, None, None]
    kvi = jnp.arange(N_KV)[None, None, :, None]

    def gather_sel(idx):
        return ksb[bidx, idx, kvi], vsb[bidx, idx, kvi]

    def block(i):
        s0 = i * Q_BLOCK
        qp = s0 + jnp.arange(Q_BLOCK, dtype=jnp.int32)
        qb = lax.dynamic_slice_in_dim(q, s0, Q_BLOCK, 1)
        gb = lax.dynamic_slice_in_dim(gates, s0, Q_BLOCK, 1)
        kwb = lax.dynamic_slice_in_dim(kwp, s0, WINDOW + Q_BLOCK, 1)
        vwb = lax.dynamic_slice_in_dim(vwp, s0, WINDOW + Q_BLOCK, 1)
        kpos = s0 - WINDOW + jnp.arange(WINDOW + Q_BLOCK, dtype=jnp.int32)
        return nsa_core(qb, qp, gb, kcc, vcc, gather_sel, n_slc, kwb, vwb, kpos)

    out = lax.map(block, jnp.arange(seq // Q_BLOCK, dtype=jnp.int32))
    return jnp.moveaxis(out, 0, 1).reshape(bsz, seq, ATT_WIDTH)


def output_block(x, m_out, a_out, att_norm_g, w_out, ln1_g, ln1_b, w_ff1, w_ff2, ln2_g, ln2_b):
    mix = jnp.concatenate([m_out, rms_norm(a_out, att_norm_g)], -1) @ w_out
    h = layer_norm(DEEPNORM_ALPHA * x + mix, ln1_g, ln1_b)
    f = jnp.square(jax.nn.relu(h @ w_ff1)) @ w_ff2
    return layer_norm(DEEPNORM_ALPHA * h + f, ln2_g, ln2_b)


def setup_inputs(seed: int = 0) -> dict:
    key = jax.random.key(seed)
    k = jax.random.split(key, 32)
    f32 = jnp.float32
    n_pages = PAST_LEN // PAGE_SIZE
    n_used = DEC_BATCH * n_pages
    n_pool = n_used + -(-n_used // 4)
    win_buf = min(WINDOW, PAST_LEN)

    def nrm(kk, shape, scale):
        return jax.random.normal(kk, shape, f32) * scale

    x_prompt = nrm(k[0], (BATCH, SEQ, D_MODEL), 1.0)
    x_sample = nrm(k[1], (DEC_BATCH, DEC_SEQ, D_MODEL), 1.0)
    cache_kv = nrm(k[2], (n_pool, PAGE_SIZE, 4, N_KV, HEAD_DIM), 1.0)
    state_win = nrm(k[3], (DEC_BATCH, win_buf, 2, N_KV, HEAD_DIM), 1.0)
    state_ssm = nrm(k[4], (DEC_BATCH, M_HEADS, HEAD_DIM, D_STATE), 0.5)
    state_conv = nrm(k[5], (DEC_BATCH, CONV_W - 1, CONV_DIM), 1.0)
    page_table = jax.random.permutation(k[6], n_pool)[:n_used].reshape(DEC_BATCH, n_pages).astype(jnp.int32)
    w_in = nrm(k[7], (D_MODEL, IN_COLS), D_MODEL ** -0.5)
    conv_w = nrm(k[8], (CONV_W, CONV_DIM), CONV_W ** -0.5)
    conv_b = nrm(k[9], (CONV_DIM,), 0.02)
    dt0 = jnp.exp(jax.random.uniform(k[10], (M_HEADS,), f32, math.log(1e-3), math.log(1e-1)))
    dt_bias = dt0 + jnp.log(-jnp.expm1(-dt0))
    a_log = jnp.log(jax.random.uniform(k[11], (M_HEADS,), f32, 1.0, 16.0))
    d_skip = 1.0 + nrm(k[12], (M_HEADS,), 0.1)
    ssm_norm_g = 1.0 + nrm(k[13], (SSM_WIDTH,), 0.02)
    pe_cmp = nrm(k[14], (2, L_CMP, HEAD_DIM), 0.1)
    w1_cmp = nrm(k[15], (2, L_CMP, HEAD_DIM, CMP_HIDDEN), (L_CMP * HEAD_DIM) ** -0.5)
    b1_cmp = nrm(k[16], (2, CMP_HIDDEN), 0.02)
    w2_cmp = nrm(k[17], (2, CMP_HIDDEN, HEAD_DIM), CMP_HIDDEN ** -0.5)
    b2_cmp = nrm(k[18], (2, HEAD_DIM), 0.02)
    att_norm_g = 1.0 + nrm(k[19], (ATT_WIDTH,), 0.02)
    w_out = nrm(k[20], (D_MODEL, D_MODEL), D_MODEL ** -0.5 * DEEPNORM_BETA)
    ln1_g = 1.0 + nrm(k[21], (D_MODEL,), 0.02)
    ln1_b = nrm(k[22], (D_MODEL,), 0.02)
    w_ff1 = nrm(k[23], (D_MODEL, D_FF), D_MODEL ** -0.5)
    w_ff2 = nrm(k[24], (D_FF, D_MODEL), D_FF ** -0.5 * DEEPNORM_BETA)
    ln2_g = 1.0 + nrm(k[25], (D_MODEL,), 0.02)
    ln2_b = nrm(k[26], (D_MODEL,), 0.02)
    return {"x_prompt": x_prompt, "x_sample": x_sample, "cache_kv": cache_kv, "state_win": state_win,
            "state_ssm": state_ssm, "state_conv": state_conv, "page_table": page_table,
            "w_in": w_in, "conv_w": conv_w, "conv_b": conv_b, "dt_bias": dt_bias, "a_log": a_log,
            "d_skip": d_skip, "ssm_norm_g": ssm_norm_g, "pe_cmp": pe_cmp, "w1_cmp": w1_cmp,
            "b1_cmp": b1_cmp, "w2_cmp": w2_cmp, "b2_cmp": b2_cmp, "att_norm_g": att_norm_g,
            "w_out": w_out, "ln1_g": ln1_g, "ln1_b": ln1_b, "w_ff1": w_ff1, "w_ff2": w_ff2,
            "ln2_g": ln2_g, "ln2_b": ln2_b}


def reference(x_prompt, x_sample, cache_kv, state_win, state_ssm, state_conv, page_table,
              w_in, conv_w, conv_b, dt_bias, a_log, d_skip, ssm_norm_g,
              pe_cmp, w1_cmp, b1_cmp, w2_cmp, b2_cmp, att_norm_g, w_out,
              ln1_g, ln1_b, w_ff1, w_ff2, ln2_g, ln2_b):
    ssm_params = (conv_w, conv_b, dt_bias, a_log, d_skip, ssm_norm_g)
    out_params = (att_norm_g, w_out, ln1_g, ln1_b, w_ff1, w_ff2, ln2_g, ln2_b)

    bsz, seq, _ = x_prompt.shape
    z, xbc, dtr, q, kv6, gates = split_proj(x_prompt @ w_in)
    m_out_p, conv_p, ssm_p = mamba_group(
        z, xbc, dtr, jnp.zeros((bsz, CONV_W - 1, CONV_DIM), x_prompt.dtype),
        jnp.zeros((bsz, M_HEADS, HEAD_DIM, D_STATE), x_prompt.dtype), *ssm_params)
    pos = jnp.arange(seq, dtype=jnp.int32)
    kc, vc, ks, vs, kw, vw = [kv6[:, :, i] for i in range(6)]
    ks = partial_rope(ks, pos)
    kw = partial_rope(kw, pos)
    kcc = finish_compress(subblock_proj(kc, w1_cmp[0]), pe_cmp[0], w1_cmp[0], b1_cmp[0], w2_cmp[0], b2_cmp[0])
    vcc = finish_compress(subblock_proj(vc, w1_cmp[1]), pe_cmp[1], w1_cmp[1], b1_cmp[1], w2_cmp[1], b2_cmp[1])
    a_out_p = nsa_prompt(q, gates, kcc, vcc, ks, vs, kw, vw)
    y_prompt = output_block(x_prompt, m_out_p, a_out_p, *out_params)
    kv_prompt = jnp.stack([kc, vc, ks, vs], axis=2)
    win_prompt = jnp.stack([kw, vw], axis=2)[:, seq - min(WINDOW, seq):]

    dbs, dseq, _ = x_sample.shape
    past_len = page_table.shape[1] * PAGE_SIZE
    wb = state_win.shape[1]
    z, xbc, dtr, q, kv6, gates = split_proj(x_sample @ w_in)
    m_out_s, conv_s, ssm_s = mamba_group(z, xbc, dtr, state_conv, state_ssm, *ssm_params)
    pos_s = past_len + jnp.arange(dseq, dtype=jnp.int32)
    kc, vc, ks, vs, kw, vw = [kv6[:, :, i] for i in range(6)]
    ks = partial_rope(ks, pos_s)
    kw = partial_rope(kw, pos_s)
    pt = page_table[:, :, None]
    offs = jnp.arange(PAGE_SIZE)[None, None, :]
    kc_past = cache_kv[pt, offs, 0].reshape(dbs, past_len, N_KV, HEAD_DIM)
    vc_past = cache_kv[pt, offs, 1].reshape(dbs, past_len, N_KV, HEAD_DIM)
    pad = (-dseq) % STRIDE
    kc_new = jnp.pad(kc, ((0, 0), (0, pad), (0, 0), (0, 0)))
    vc_new = jnp.pad(vc, ((0, 0), (0, pad), (0, 0), (0, 0)))
    h_k = jnp.concatenate([subblock_proj(kc_past, w1_cmp[0]), subblock_proj(kc_new, w1_cmp[0])], 1)
    h_v = jnp.concatenate([subblock_proj(vc_past, w1_cmp[1]), subblock_proj(vc_new, w1_cmp[1])], 1)
    kcc = finish_compress(h_k, pe_cmp[0], w1_cmp[0], b1_cmp[0], w2_cmp[0], b2_cmp[0])
    vcc = finish_compress(h_v, pe_cmp[1], w1_cmp[1], b1_cmp[1], w2_cmp[1], b2_cmp[1])
    n_slc = -(-(past_len + dseq) // L_SEL)
    bidx = jnp.arange(dbs)[:, None, None, None, None]
    kvi = jnp.arange(N_KV)[None, None, :, None, None]

    def gather_sel(idx):
        kpos = idx[..., None] * L_SEL + jnp.arange(L_SEL, dtype=jnp.int32)
        past = jnp.clip(kpos, 0, past_len - 1)
        phys = page_table[bidx, past // PAGE_SIZE]
        off = past % PAGE_SIZE
        newi = jnp.clip(kpos - past_len, 0, dseq - 1)
        is_past = (kpos < past_len)[..., None]
        k_sel = jnp.where(is_past, cache_kv[phys, off, 2, kvi], ks[bidx, newi, kvi])
        v_sel = jnp.where(is_past, cache_kv[phys, off, 3, kvi], vs[bidx, newi, kvi])
        return k_sel, v_sel

    kw_all = jnp.concatenate([state_win[:, :, 0], kw], 1)
    vw_all = jnp.concatenate([state_win[:, :, 1], vw], 1)
    kpos_w = past_len - wb + jnp.arange(wb + dseq, dtype=jnp.int32)
    a_out_s = nsa_core(q, pos_s, gates, kcc, vcc, gather_sel, n_slc, kw_all, vw_all, kpos_w)
    y_sample = output_block(x_sample, m_out_s, a_out_s, *out_params)
    kv_sample = jnp.stack([kc, vc, ks, vs], axis=2)
    win_sample = jnp.concatenate([state_win, jnp.stack([kw, vw], axis=2)], 1)[:, dseq:]

    return (y_prompt, y_sample, kv_prompt, win_prompt, ssm_p, conv_p, kv_sample, win_sample, ssm_s, conv_s)
```

```python
import functools

import jax
import jax.numpy as jnp
from jax import lax
from jax.experimental import pallas as pl
from jax.experimental.pallas import tpu as pltpu

F32 = jnp.float32
BF16 = jnp.bfloat16

D_MODEL = 1024
HEAD_DIM = 64
SSM_WIDTH = 512
ATT_WIDTH = 512
M_HEADS = 8
M_GROUPS = 2
D_STATE = 128
CONV_W = 4
CONV_DIM = SSM_WIDTH + 2 * M_GROUPS * D_STATE
SSD_CHUNK = 256
RMS_EPS = 1e-6
N_HEADS = 8
N_KV = 2
GQA = N_HEADS // N_KV
ROT_DIM = 16
ROPE_THETA = 500000.0
L_CMP = 32
STRIDE = 16
CMP_HIDDEN = 128
L_SEL = 64
N_SEL = 16
N_LOCAL = 2
WINDOW = 512
Q_BLOCK = 128
ATT_SCALE = HEAD_DIM ** -0.5
D_FF = 4 * D_MODEL
LN_EPS = 1e-5
DEEPNORM_ALPHA = 2.0 ** 0.25
PAGE_SIZE = 128
LANES = 128
NEG = -1e30
SEL_KEY_TILE = 1024
VMEM_LIMIT = 56 * 1024 * 1024


def _cparams(sem):
    return pltpu.CompilerParams(dimension_semantics=sem, vmem_limit_bytes=VMEM_LIMIT)


def _dot(a, b):
    return jnp.dot(a, b, preferred_element_type=F32)


def _dot_nt(a, b):
    return lax.dot_general(a, b, (((1,), (1,)), ((), ())), preferred_element_type=F32)


def _split3(x):
    hi = x.astype(BF16)
    r = x - hi.astype(F32)
    mid = r.astype(BF16)
    lo = (r - mid.astype(F32)).astype(BF16)
    return hi, mid, lo


def _dot3(x, w):
    hi, mid, lo = _split3(x)
    return _dot(hi, w) + _dot(mid, w) + _dot(lo, w)


def _sigmoid(x):
    return 1.0 / (1.0 + jnp.exp(-x))


def _silu(x):
    return x * _sigmoid(x)


def _softplus(x):
    return jnp.maximum(x, 0.0) + jnp.log1p(jnp.exp(-jnp.abs(x)))


def _layer_norm(x, g, b):
    mu = jnp.mean(x, -1, keepdims=True)
    xc = x - mu
    var = jnp.mean(xc * xc, -1, keepdims=True)
    return xc * lax.rsqrt(var + LN_EPS) * g + b


def _rope128(x, c, s1, s2):
    return x * c + pltpu.roll(x, 8, 1) * s1 + pltpu.roll(x, LANES - 8, 1) * s2


def _full(shape):
    nd = len(shape)
    return pl.BlockSpec(shape, lambda *a: (0,) * nd)


def _resident(shape):
    nd = len(shape)
    return pl.BlockSpec(shape, lambda *a: (0,) * nd, pipeline_mode=pl.Buffered(1))


def _proj_kernel(x_ref, wz_ref, wxbc_ref, wq_ref, wkv_ref, wsm_ref, c_ref, s1_ref, s2_ref,
                 z_ref, xbc_ref, qun_ref, qrot_ref, kv4_ref, win_ref, att_ref, sm_ref):
    x = x_ref[...].astype(BF16)
    c, s1, s2 = c_ref[...], s1_ref[...], s2_ref[...]
    z_ref[...] = _dot(x, wz_ref[...])
    xbc_ref[...] = _dot(x, wxbc_ref[...])
    q = _dot(x, wq_ref[...])
    qun_ref[...] = q.astype(BF16)
    for h in range(N_HEADS):
        sl = slice(h * LANES, (h + 1) * LANES)
        qrot_ref[:, sl] = _rope128(q[:, sl], c, s1, s2).astype(BF16)
    kv = _dot(x, wkv_ref[...])
    ks = _rope128(kv[:, 256:384], c, s1, s2)
    kw = _rope128(kv[:, 512:640], c, s1, s2)
    kv4_ref[:, 0:256] = kv[:, 0:256]
    kv4_ref[:, 256:384] = ks
    kv4_ref[:, 384:512] = kv[:, 384:512]
    win_ref[:, 0:128] = kw
    win_ref[:, 128:256] = kv[:, 640:768]
    att_ref[:, 0:128] = ks.astype(BF16)
    att_ref[:, 128:256] = kv[:, 384:512].astype(BF16)
    att_ref[:, 256:384] = kw.astype(BF16)
    att_ref[:, 384:512] = kv[:, 640:768].astype(BF16)
    sm = _dot(x, wsm_ref[...])
    lane = lax.broadcasted_iota(jnp.int32, sm.shape, 1)
    is_gate = (lane >= M_HEADS) & (lane < M_HEADS + 3 * N_HEADS)
    sm_ref[...] = jnp.where(is_gate, _sigmoid(sm), sm)


def _project(x, weights, tables, tm, win_rows):
    t = x.shape[0]
    nt = t // tm
    nwb = win_rows // tm
    wz, wxbc, wq, wkv, wsm = weights
    row = lambda w: pl.BlockSpec((tm, w), lambda i: (i, 0))
    out_shape = (
        jax.ShapeDtypeStruct((t, 512), F32),
        jax.ShapeDtypeStruct((t, 1024), F32),
        jax.ShapeDtypeStruct((t, 1024), BF16),
        jax.ShapeDtypeStruct((t, 1024), BF16),
        jax.ShapeDtypeStruct((t, 512), F32),
        jax.ShapeDtypeStruct((win_rows, 256), F32),
        jax.ShapeDtypeStruct((t, 512), BF16),
        jax.ShapeDtypeStruct((t, 128), F32),
    )
    out_specs = (row(512), row(1024), row(1024), row(1024), row(512),
                 pl.BlockSpec((tm, 256), lambda i: (jnp.maximum(i - (nt - nwb), 0), 0)),
                 row(512), row(128))
    in_specs = [row(D_MODEL), _full(wz.shape), _full(wxbc.shape), _full(wq.shape), _full(wkv.shape),
                _full(wsm.shape), row(128), row(128), row(128)]
    return pl.pallas_call(
        _proj_kernel, grid=(nt,), in_specs=in_specs, out_specs=out_specs, out_shape=out_shape,
        compiler_params=_cparams(("arbitrary",)), name="proj",
    )(x, wz, wxbc, wq, wkv, wsm, *tables)


def _mamba_prompt_kernel(z_ref, xbc_ref, sm_ref, cw_ref, cb_ref, dtb_ref, alog_ref, dsk_ref, ng_ref,
                         y_ref, conv_ref, ssm_ref, ubuf, state):
    c = pl.program_id(0)
    nc = pl.num_programs(0)
    L = SSD_CHUNK

    @pl.when(c == 0)
    def _():
        ubuf[0:8, :] = jnp.zeros((8, CONV_DIM), F32)
        state[...] = jnp.zeros(state.shape, F32)

    u = xbc_ref[...]
    ubuf[8:8 + L, :] = u
    acc = u * cw_ref[3:4, :] + cb_ref[...]
    for k in range(CONV_W - 1):
        acc = acc + ubuf[pl.ds(8 - (CONV_W - 1) + k, L), :] * cw_ref[k:k + 1, :]
    tail = ubuf[L:L + 8, :]
    ubuf[0:8, :] = tail
    xa = _silu(acc)
    xs = xa[:, 0:SSM_WIDTH]
    bm = xa[:, SSM_WIDTH:SSM_WIDTH + 256]
    cm = xa[:, SSM_WIDTH + 256:SSM_WIDTH + 512]

    lane = lax.broadcasted_iota(jnp.int32, (L, LANES), 1)
    head_lane = lane < M_HEADS
    dt = jnp.where(head_lane, _softplus(sm_ref[...] + dtb_ref[...]), 0.0)
    a_row = jnp.where(head_lane[0:1], -jnp.exp(alog_ref[...]), 0.0)
    da = dt * a_row
    ri = lax.broadcasted_iota(jnp.int32, (L, L), 0)
    ci = lax.broadcasted_iota(jnp.int32, (L, L), 1)
    causal = ri >= ci
    tril = jnp.where(causal, 1.0, 0.0).astype(BF16)
    da_hi, da_mid, da_lo = _split3(da)
    cum = _dot(tril, da_hi) + _dot(tril, da_mid) + _dot(tril, da_lo)
    cum_t = cum.T
    er = lax.broadcasted_iota(jnp.int32, (LANES, SSM_WIDTH), 0)
    ec = lax.broadcasted_iota(jnp.int32, (LANES, SSM_WIDTH), 1)
    expand = jnp.where(er == (ec >> 6), 1.0, 0.0).astype(BF16)
    dtx = _dot3(dt, expand)
    cumx = _dot3(cum, expand)
    cumx_last = cumx[L - 1:L, :]
    xc = xs * dtx
    xw = xc * jnp.exp(cumx_last - cumx)
    ecx = jnp.exp(cumx)
    dec_last = jnp.exp(cumx_last)

    half = lax.broadcasted_iota(jnp.int32, (L, LANES), 1) < HEAD_DIM
    y_parts = []
    for g in range(M_GROUPS):
        bg = bm[:, g * D_STATE:(g + 1) * D_STATE].astype(BF16)
        cg = cm[:, g * D_STATE:(g + 1) * D_STATE].astype(BF16)
        gs = slice(g * 256, (g + 1) * 256)
        scores = _dot_nt(cg, bg)
        h_prev = state[:, gs]
        y_off = _dot(cg, h_prev.astype(BF16)) * ecx[:, gs]
        st = _dot(bg.T, xw[:, gs].astype(BF16))
        state[:, gs] = h_prev * dec_last[:, gs] + st
        for pr in range(2):
            ha = g * 4 + pr * 2
            xcp = xc[:, ha * HEAD_DIM:(ha + 2) * HEAD_DIM].astype(BF16)
            outs = []
            for hh in (ha, ha + 1):
                seg = cum[:, hh:hh + 1] - cum_t[hh:hh + 1, :]
                dec = jnp.exp(jnp.where(causal, seg, -jnp.inf))
                outs.append(_dot((scores * dec).astype(BF16), xcp))
            y_parts.append(jnp.where(half, outs[0], outs[1]) + y_off[:, pr * 128:(pr + 1) * 128])
    y = jnp.concatenate(y_parts, axis=1) + dsk_ref[...] * xs
    y = y * _silu(z_ref[...])
    outs = []
    for g in range(M_GROUPS):
        yg = y[:, g * 256:(g + 1) * 256]
        outs.append(yg * lax.rsqrt(jnp.mean(yg * yg, -1, keepdims=True) + RMS_EPS))
    y_ref[...] = (jnp.concatenate(outs, axis=1) * ng_ref[...]).astype(BF16)

    @pl.when(c == nc - 1)
    def _():
        conv_ref[...] = ubuf[pl.ds(8 - (CONV_W - 1), CONV_W - 1), :]
        ssm_ref[...] = state[...].T


def _mamba_prompt(z, xbc, sm, cw, cb, dtb, alog, dsk, ng):
    t = z.shape[0]
    L = SSD_CHUNK
    row = lambda w: pl.BlockSpec((L, w), lambda i: (i, 0))
    return pl.pallas_call(
        _mamba_prompt_kernel, grid=(t // L,),
        in_specs=[row(512), row(1024), row(128), _full(cw.shape), _full(cb.shape), _full(dtb.shape),
                  _full(alog.shape), _full(dsk.shape), _full(ng.shape)],
        out_specs=(row(512), _full((CONV_W - 1, CONV_DIM)), _full((SSM_WIDTH, D_STATE))),
        out_shape=(jax.ShapeDtypeStruct((t, 512), BF16),
                   jax.ShapeDtypeStruct((CONV_W - 1, CONV_DIM), F32),
                   jax.ShapeDtypeStruct((SSM_WIDTH, D_STATE), F32)),
        scratch_shapes=[pltpu.VMEM((L + 8, CONV_DIM), F32), pltpu.VMEM((D_STATE, SSM_WIDTH), F32)],
        compiler_params=_cparams(("arbitrary",)), name="mamba_prompt",
    )(z, xbc, sm, cw, cb, dtb, alog, dsk, ng)


def _mamba_sample_kernel(z_ref, xbc_ref, sm_ref, sc_ref, h0_ref, cw_ref, cb_ref, dtb_ref, alog_ref, dsk_ref,
                         ng_ref, y_ref, conv_ref, h_ref, xct, dect, brow, crow, xcrow, decrow, xsrow, yrow):
    b = pl.program_id(0)
    nb = pl.num_programs(0)
    db = z_ref.shape[0]

    @pl.when(b == 0)
    def _():
        u = xbc_ref[...]
        acc = u * cw_ref[3:4, :] + cb_ref[...]
        for k in range(CONV_W - 1):
            acc = acc + sc_ref[k] * cw_ref[k:k + 1, :]
        conv_ref[0] = sc_ref[1]
        conv_ref[1] = sc_ref[2]
        conv_ref[2] = u
        xa = _silu(acc)
        xs = xa[:, 0:SSM_WIDTH]
        lane = lax.broadcasted_iota(jnp.int32, (db, LANES), 1)
        head_lane = lane < M_HEADS
        dt = jnp.where(head_lane, _softplus(sm_ref[...] + dtb_ref[...]), 0.0)
        da = dt * jnp.where(head_lane[0:1], -jnp.exp(alog_ref[...]), 0.0)
        er = lax.broadcasted_iota(jnp.int32, (LANES, SSM_WIDTH), 0)
        ec = lax.broadcasted_iota(jnp.int32, (LANES, SSM_WIDTH), 1)
        expand = jnp.where(er == (ec >> 6), 1.0, 0.0).astype(BF16)
        xc = xs * _dot3(dt, expand)
        dec = jnp.exp(_dot3(da, expand))
        xct[...] = xc.T
        dect[...] = dec.T
        xcrow[...] = xc
        decrow[...] = dec
        xsrow[...] = xs
        brow[...] = xa[:, SSM_WIDTH:SSM_WIDTH + 256]
        crow[...] = xa[:, SSM_WIDTH + 256:SSM_WIDTH + 512]

    r = lax.broadcasted_iota(jnp.int32, (db, LANES), 0)
    pick = jnp.where(r == b, 1.0, 0.0).astype(BF16)
    colx = _dot3(xct[...], pick)
    cold = _dot3(dect[...], pick)
    bv = brow[pl.ds(b, 1), :]
    cv = crow[pl.ds(b, 1), :]
    h0 = h0_ref[0]
    ys = []
    for g in range(M_GROUPS):
        rs = slice(g * 256, (g + 1) * 256)
        bg = bv[:, g * D_STATE:(g + 1) * D_STATE]
        cg = cv[:, g * D_STATE:(g + 1) * D_STATE]
        h_ref[0, rs, :] = cold[rs] * h0[rs] + colx[rs] * bg
        cb_s = jnp.sum(cg * bg, -1, keepdims=True)
        c8 = jnp.broadcast_to(cg, (8, D_STATE)).astype(BF16)
        y_off = _dot_nt(c8, h0[rs].astype(BF16))[0:1]
        ys.append(y_off * decrow[pl.ds(b, 1), rs] + cb_s * xcrow[pl.ds(b, 1), rs])
    yrow[pl.ds(b, 1), :] = jnp.concatenate(ys, axis=1)

    @pl.when(b == nb - 1)
    def _():
        y = (yrow[...] + dsk_ref[...] * xsrow[...]) * _silu(z_ref[...])
        outs = []
        for g in range(M_GROUPS):
            yg = y[:, g * 256:(g + 1) * 256]
            outs.append(yg * lax.rsqrt(jnp.mean(yg * yg, -1, keepdims=True) + RMS_EPS))
        y_ref[...] = (jnp.concatenate(outs, axis=1) * ng_ref[...]).astype(BF16)


def _mamba_sample(z, xbc, sm, sc_t, h0, cw, cb, dtb, alog, dsk, ng):
    db = z.shape[0]
    vm = lambda s: pltpu.VMEM(s, F32)
    return pl.pallas_call(
        _mamba_sample_kernel, grid=(db,),
        in_specs=[_full(z.shape), _full(xbc.shape), _full(sm.shape), _full(sc_t.shape),
                  pl.BlockSpec((1, SSM_WIDTH, D_STATE), lambda b: (b, 0, 0)),
                  _full(cw.shape), _full(cb.shape), _full(dtb.shape), _full(alog.shape), _full(dsk.shape),
                  _full(ng.shape)],
        out_specs=(_full((db, 512)), _full((CONV_W - 1, db, CONV_DIM)),
                   pl.BlockSpec((1, SSM_WIDTH, D_STATE), lambda b: (b, 0, 0))),
        out_shape=(jax.ShapeDtypeStruct((db, 512), BF16),
                   jax.ShapeDtypeStruct((CONV_W - 1, db, CONV_DIM), F32),
                   jax.ShapeDtypeStruct((db, SSM_WIDTH, D_STATE), F32)),
        scratch_shapes=[vm((SSM_WIDTH, db)), vm((SSM_WIDTH, db)), vm((db, 256)), vm((db, 256)),
                        vm((db, SSM_WIDTH)), vm((db, SSM_WIDTH)), vm((db, SSM_WIDTH)), vm((db, SSM_WIDTH))],
        compiler_params=_cparams(("arbitrary",)), name="mamba_sample",
    )(z, xbc, sm, sc_t, h0, cw, cb, dtb, alog, dsk, ng)


def _stride_block_proj(x_ref, n_blocks, w_ref):
    acc = jnp.zeros((n_blocks, 512), F32)
    for j in range(STRIDE):
        xj = x_ref[pl.ds(j, n_blocks, stride=STRIDE), :].astype(BF16)
        acc = acc + _dot(xj, w_ref[j])
    return acc


def _cmp_proj_kernel(xk_ref, xv_ref, wk_ref, wv_ref, hk_ref, hv_ref):
    n_blocks = xk_ref.shape[0] // STRIDE
    hk_ref[...] = _stride_block_proj(xk_ref, n_blocks, wk_ref)
    hv_ref[...] = _stride_block_proj(xv_ref, n_blocks, wv_ref)


def _cmp_const(pe_ref, w1f_ref, b1_ref):
    c = _dot(pe_ref[...], w1f_ref[...])[0:1]
    return jnp.concatenate([c, c], axis=1) + b1_ref[...]


def _cmp_mlp(h0, h1_next, const, w2_ref, b2_ref):
    pre = h0 + h1_next + const
    return _dot(_silu(pre).astype(BF16), w2_ref[...]) + b2_ref[...]


def _cmp_finish_kernel(hk_ref, hv_ref, pek, w1k, b1k, w2k, b2k, pev, w1v, b1v, w2v, b2v, kcc_ref, vcc_ref):
    n = hk_ref.shape[0]
    for h_ref, pe, w1f, b1, w2, b2, o_ref in ((hk_ref, pek, w1k, b1k, w2k, b2k, kcc_ref),
                                              (hv_ref, pev, w1v, b1v, w2v, b2v, vcc_ref)):
        h = h_ref[...]
        h1_next = pltpu.roll(h[:, 256:512], n - 1, 0)
        o_ref[...] = _cmp_mlp(h[:, 0:256], h1_next, _cmp_const(pe, w1f, b1), w2, b2).astype(BF16)


def _compress_prompt(kv4, cw):
    t = kv4.shape[0]
    rows = min(t, 2048)
    n_blocks = t // STRIDE
    hk, hv = pl.pallas_call(
        _cmp_proj_kernel, grid=(t // rows,),
        in_specs=[pl.BlockSpec((rows, LANES), lambda i: (i, 0)), pl.BlockSpec((rows, LANES), lambda i: (i, 1)),
                  _full(cw["wk"].shape), _full(cw["wv"].shape)],
        out_specs=(pl.BlockSpec((rows // STRIDE, 512), lambda i: (i, 0)),) * 2,
        out_shape=(jax.ShapeDtypeStruct((n_blocks, 512), F32),) * 2,
        compiler_params=_cparams(("arbitrary",)), name="cmp_proj",
    )(kv4, kv4, cw["wk"], cw["wv"])
    names = ("pek", "w1k", "b1k", "w2k", "b2k", "pev", "w1v", "b1v", "w2v", "b2v")
    args = [cw[k] for k in names]
    return pl.pallas_call(
        _cmp_finish_kernel, grid=(1,),
        in_specs=[_full(hk.shape), _full(hv.shape)] + [_full(a.shape) for a in args],
        out_specs=(_full((n_blocks, 128)),) * 2,
        out_shape=(jax.ShapeDtypeStruct((n_blocks, 128), BF16),) * 2,
        compiler_params=_cparams(("arbitrary",)), name="cmp_finish",
    )(hk, hv, *args)


def _softmax_rows(s):
    m = jnp.max(s, -1, keepdims=True)
    m = jnp.where(jnp.isfinite(m), m, 0.0)
    e = jnp.exp(s - m)
    return e / jnp.maximum(jnp.sum(e, -1, keepdims=True), 1e-30)


def _topk_mask_cols(sc, blkf, n_top):
    taken = jnp.zeros(sc.shape, F32)
    for _ in range(n_top):
        m = jnp.max(sc, 0, keepdims=True)
        cand = (sc == m) & (taken == 0.0)
        idx = jnp.min(jnp.where(cand, blkf, 1e9), 0, keepdims=True)
        pick = blkf == idx
        taken = jnp.where(pick, 1.0, taken)
        sc = jnp.where(pick, -jnp.inf, sc)
    return taken


def _nsa_prompt_kernel(qun_ref, qrot_ref, sm_ref, kcc_ref, vcc_ref, att_ref, wsel_ref, eall_ref, o_ref,
                       m_scr, l_scr, acc_scr):
    i = pl.program_id(0)
    s0 = i * Q_BLOCK
    ncp = kcc_ref.shape[0]
    nb = wsel_ref.shape[0]
    kt = min(SEL_KEY_TILE, att_ref.shape[0])
    n_top = min(N_SEL, nb)

    def head(ref, h):
        return ref[:, h * LANES:(h + 1) * LANES]

    colc = lax.broadcasted_iota(jnp.int32, (Q_BLOCK, ncp), 1)
    rowq = lax.broadcasted_iota(jnp.int32, (Q_BLOCK, ncp), 0) + s0
    vis = colc * STRIDE + (L_CMP - 1) <= rowq
    kcc = kcc_ref[...]
    vcc = vcc_ref[...]
    o_cmp, imps = [], []
    for kv in range(N_KV):
        imp = jnp.zeros((Q_BLOCK, ncp), F32)
        for g in range(GQA):
            s = _dot_nt(head(qun_ref, kv * GQA + g), kcc) * ATT_SCALE
            p = _softmax_rows(jnp.where(vis, s, -jnp.inf))
            imp = imp + p
            o_cmp.append(_dot(p.astype(BF16), vcc))
        imps.append(imp)

    blk = lax.broadcasted_iota(jnp.int32, (nb, Q_BLOCK), 0)
    qpl = lax.broadcasted_iota(jnp.int32, (nb, Q_BLOCK), 1) + s0
    cur = qpl >> 6
    valid = blk * L_SEL <= qpl
    forced = (blk == 0) | ((blk <= cur) & (blk > cur - N_LOCAL))
    blkf = blk.astype(F32)
    wsel = wsel_ref[...]

    o_sel, o_win = [], []
    rowk = lax.broadcasted_iota(jnp.int32, (Q_BLOCK, kt), 0) + s0
    lanek = lax.broadcasted_iota(jnp.int32, (Q_BLOCK, kt), 1)
    n_tiles = (s0 + Q_BLOCK - 1) // kt + 1
    for kv in range(N_KV):
        hi, mid, lo = _split3(imps[kv])
        sc = _dot_nt(wsel, hi) + _dot_nt(wsel, mid) + _dot_nt(wsel, lo)
        sc = jnp.where(forced, jnp.inf, jnp.where(valid, sc, -jnp.inf))
        sel = _topk_mask_cols(sc, blkf, n_top).T.astype(BF16)
        for g in range(GQA):
            m_scr[g] = jnp.full((Q_BLOCK, LANES), NEG, F32)
            l_scr[g] = jnp.zeros((Q_BLOCK, LANES), F32)
            acc_scr[g] = jnp.zeros((Q_BLOCK, LANES), F32)

        def tile_body(t, carry):
            k0 = pl.multiple_of(t * kt, kt)
            ks = att_ref[pl.ds(k0, kt), 0:128]
            vs = att_ref[pl.ds(k0, kt), 128:256]
            picked = _dot(sel, eall_ref[:, pl.ds(k0, kt)])
            ok = (picked > 0.5) & (lanek + k0 <= rowk)
            for g in range(GQA):
                s = jnp.where(ok, _dot_nt(head(qrot_ref, kv * GQA + g), ks) * ATT_SCALE, NEG)
                m_old = m_scr[g]
                m_new = jnp.maximum(m_old, jnp.max(s, -1, keepdims=True))
                alpha = jnp.exp(m_old - m_new)
                p = jnp.exp(s - jnp.tile(m_new, (1, kt // LANES)))
                l_scr[g] = alpha * l_scr[g] + jnp.sum(p, -1, keepdims=True)
                acc_scr[g] = alpha * acc_scr[g] + _dot(p.astype(BF16), vs)
                m_scr[g] = m_new
            return carry

        lax.fori_loop(0, n_tiles, tile_body, 0)
        for g in range(GQA):
            o_sel.append(acc_scr[g] / l_scr[g])

    wk = WINDOW + Q_BLOCK
    w0 = pl.multiple_of(jnp.maximum(s0 - WINDOW, 0), Q_BLOCK)
    kw = att_ref[pl.ds(w0, wk), 256:384]
    vw = att_ref[pl.ds(w0, wk), 384:512]
    kposw = lax.broadcasted_iota(jnp.int32, (Q_BLOCK, wk), 1) + w0
    roww = lax.broadcasted_iota(jnp.int32, (Q_BLOCK, wk), 0) + s0
    okw = (kposw <= roww) & (kposw > roww - WINDOW)
    for h in range(N_HEADS):
        s = _dot_nt(head(qrot_ref, h), kw) * ATT_SCALE
        p = _softmax_rows(jnp.where(okw, s, -jnp.inf))
        o_win.append(_dot(p.astype(BF16), vw))

    sm = sm_ref[...]
    lane = lax.broadcasted_iota(jnp.int32, (Q_BLOCK, LANES), 1)
    chunks = []
    for c in range(N_HEADS // 2):
        parts = []
        for h in (2 * c, 2 * c + 1):
            kv = h // GQA
            gc = M_HEADS + 3 * h
            a = sm[:, gc:gc + 1] * o_cmp[h] + sm[:, gc + 1:gc + 2] * o_sel[h] + sm[:, gc + 2:gc + 3] * o_win[h]
            a = jnp.where((lane >= kv * HEAD_DIM) & (lane < (kv + 1) * HEAD_DIM), a, 0.0)
            if kv != h % 2:
                a = pltpu.roll(a, HEAD_DIM, 1)
            parts.append(a)
        chunks.append(parts[0] + parts[1])
    o_ref[...] = jnp.concatenate(chunks, axis=1)


def _nsa_prompt(qun, qrot, sm, kcc, vcc, att, wsel, eall):
    t = qun.shape[0]
    row = lambda w: pl.BlockSpec((Q_BLOCK, w), lambda i: (i, 0))
    return pl.pallas_call(
        _nsa_prompt_kernel, grid=(t // Q_BLOCK,),
        in_specs=[row(1024), row(1024), row(128), _resident(kcc.shape), _resident(vcc.shape),
                  _resident(att.shape), _resident(wsel.shape), _resident(eall.shape)],
        out_specs=row(512),
        out_shape=jax.ShapeDtypeStruct((t, 512), F32),
        scratch_shapes=[pltpu.VMEM((GQA, Q_BLOCK, LANES), F32)] * 3,
        compiler_params=_cparams(("arbitrary",)), name="nsa_prompt",
    )(qun, qrot, sm, kcc, vcc, att, wsel, eall)


def _out_kernel(x_ref, m_ref, a_ref, amask_ref, ang_ref, wom_ref, woa_ref, l1g_ref, l1b_ref, w1_ref, w2_ref,
                l2g_ref, l2b_ref, y_ref):
    a = jnp.where(amask_ref[...] > 0.0, a_ref[...], 0.0)
    ms = jnp.sum(a * a, -1, keepdims=True) * (1.0 / ATT_WIDTH)
    an = a * lax.rsqrt(ms + RMS_EPS) * ang_ref[...]
    mix = _dot(m_ref[...], wom_ref[...]) + _dot(an.astype(BF16), woa_ref[...])
    h = _layer_norm(DEEPNORM_ALPHA * x_ref[...] + mix, l1g_ref[...], l1b_ref[...])
    hb = h.astype(BF16)
    f = jnp.zeros(h.shape, F32)
    for c in range(D_FF // 1024):
        cs = slice(c * 1024, (c + 1) * 1024)
        u = jnp.maximum(_dot(hb, w1_ref[:, cs]), 0.0)
        f = f + _dot((u * u).astype(BF16), w2_ref[cs, :])
    y_ref[...] = _layer_norm(DEEPNORM_ALPHA * h + f, l2g_ref[...], l2b_ref[...])


def _output_block(x, m_out, a_out, att, ow, tm):
    t = x.shape[0]
    row = lambda w: pl.BlockSpec((tm, w), lambda i: (i, 0))
    names = ("l1g", "l1b", "w1", "w2", "l2g", "l2b")
    args = [att["mask"], att["ang"], ow["wom"], att["woa"]] + [ow[k] for k in names]
    return pl.pallas_call(
        _out_kernel, grid=(t // tm,),
        in_specs=[row(D_MODEL), row(512), row(a_out.shape[1])] + [_resident(a.shape) for a in args],
        out_specs=row(D_MODEL),
        out_shape=jax.ShapeDtypeStruct((t, D_MODEL), F32),
        compiler_params=_cparams(("arbitrary",)), name="out_block",
    )(x, m_out, a_out, *args)


def _sample_cmp_kernel(pt_ref, cache_ref, new_ref, q_ref, wk_ref, wv_ref,
                       pek, w1k, b1k, w2k, b2k, pev, w1v, b1v, w2v, b2v, wsel_ref,
                       ocmp_ref, score_ref, buf, sem, *, n_pages, past_len):
    b = pl.program_id(0)
    nb = pl.num_programs(0)
    n_blocks = n_pages * PAGE_SIZE // STRIDE

    def page_copy(bb, slot, pg, c):
        return pltpu.make_async_copy(
            cache_ref.at[pt_ref[bb * n_pages + pg], :, pl.ds(c * LANES, LANES)],
            buf.at[slot, c, pl.ds(pg * PAGE_SIZE, PAGE_SIZE), :], sem.at[slot])

    def start_all(bb, slot):
        def body(pg, carry):
            page_copy(bb, slot, pg, 0).start()
            page_copy(bb, slot, pg, 1).start()
            return carry
        lax.fori_loop(0, n_pages, body, 0)

    def wait_all(bb, slot):
        def body(pg, carry):
            page_copy(bb, slot, pg, 0).wait()
            page_copy(bb, slot, pg, 1).wait()
            return carry
        lax.fori_loop(0, n_pages, body, 0)

    slot = b % 2

    @pl.when(b == 0)
    def _():
        start_all(0, 0)

    @pl.when(b + 1 < nb)
    def _():
        start_all(b + 1, 1 - slot)

    wait_all(b, slot)

    new = jnp.broadcast_to(new_ref[0], (8, 512))
    rowi = lax.broadcasted_iota(jnp.int32, (n_blocks, 256), 0)
    toks = []
    for c, w_ref, pe, w1f, b1, w2, b2 in ((0, wk_ref, pek, w1k, b1k, w2k, b2k),
                                          (1, wv_ref, pev, w1v, b1v, w2v, b2v)):
        h = _stride_block_proj(buf.at[slot, c], n_blocks, w_ref)
        h_new = _dot(new[:, c * LANES:(c + 1) * LANES].astype(BF16), w_ref[0][:, 256:512])[0:1]
        h1_next = jnp.where(rowi == n_blocks - 1, h_new, pltpu.roll(h[:, 256:512], n_blocks - 1, 0))
        toks.append(_cmp_mlp(h[:, 0:256], h1_next, _cmp_const(pe, w1f, b1), w2, b2).astype(BF16))
    kcc, vcc = toks

    q = q_ref[0]
    s = _dot_nt(q, kcc) * ATT_SCALE
    colc = lax.broadcasted_iota(jnp.int32, s.shape, 1)
    p = _softmax_rows(jnp.where(colc * STRIDE + (L_CMP - 1) <= past_len, s, -jnp.inf))
    ocmp_ref[0] = _dot(p.astype(BF16), vcc)
    imp0 = p[0:1] + p[1:2] + p[2:3] + p[3:4]
    imp1 = p[4:5] + p[5:6] + p[6:7] + p[7:8]
    rowp = lax.broadcasted_iota(jnp.int32, p.shape, 0)
    imp = jnp.where(rowp == 0, imp0, jnp.where(rowp == 1, imp1, 0.0))
    score_ref[0] = _dot3(imp, wsel_ref[...])


def _sample_cmp(page_table, cache3, kv4_s, q8, cw, wsel_s, past_len):
    db, n_pages = page_table.shape
    names = ("pek", "w1k", "b1k", "w2k", "b2k", "pev", "w1v", "b1v", "w2v", "b2v")
    args = [cw[k] for k in names]
    nbp = wsel_s.shape[1]
    full = lambda a: pl.BlockSpec(a.shape, lambda b, pt: (0,) * a.ndim)
    grid_spec = pltpu.PrefetchScalarGridSpec(
        num_scalar_prefetch=1, grid=(db,),
        in_specs=[pl.BlockSpec(memory_space=pl.ANY),
                  pl.BlockSpec((1, 1, 512), lambda b, pt: (b, 0, 0)),
                  pl.BlockSpec((1, 8, 128), lambda b, pt: (b, 0, 0)),
                  full(cw["wk"]), full(cw["wv"])] + [full(a) for a in args] + [full(wsel_s)],
        out_specs=(pl.BlockSpec((1, 8, 128), lambda b, pt: (b, 0, 0)),
                   pl.BlockSpec((1, 8, nbp), lambda b, pt: (b, 0, 0))),
        scratch_shapes=[pltpu.VMEM((2, 2, n_pages * PAGE_SIZE, LANES), F32), pltpu.SemaphoreType.DMA((2,))],
    )
    return pl.pallas_call(
        functools.partial(_sample_cmp_kernel, n_pages=n_pages, past_len=past_len),
        grid_spec=grid_spec,
        out_shape=(jax.ShapeDtypeStruct((db, 8, 128), F32), jax.ShapeDtypeStruct((db, 8, nbp), F32)),
        compiler_params=_cparams(("arbitrary",)), name="sample_cmp",
    )(page_table.reshape(-1), cache3, kv4_s.reshape(db, 1, 512), q8, cw["wk"], cw["wv"], *args, wsel_s)


def _sample_topk_kernel(score_ref, idx_ref, *, n_slc, qp):
    sc = score_ref[...]
    blk = lax.broadcasted_iota(jnp.int32, sc.shape, 1)
    cur = qp // L_SEL
    valid = (blk * L_SEL <= qp) & (blk < n_slc)
    forced = (blk == 0) | ((blk <= cur) & (blk > cur - N_LOCAL))
    sc = jnp.where(forced & (blk < n_slc), jnp.inf, jnp.where(valid, sc, -jnp.inf))
    real = blk < n_slc
    blkf = blk.astype(F32)
    taken = jnp.where(real, 0.0, 1.0)
    out = jnp.zeros(idx_ref.shape, F32)
    lane = lax.broadcasted_iota(jnp.int32, idx_ref.shape, 1)
    for it in range(min(N_SEL, n_slc)):
        m = jnp.max(jnp.where(taken == 0.0, sc, -jnp.inf), -1, keepdims=True)
        cand = (sc == m) & (taken == 0.0)
        idx = jnp.min(jnp.where(cand, blkf, 1e9), -1, keepdims=True)
        pick = blkf == idx
        taken = jnp.where(pick, 1.0, taken)
        out = jnp.where(lane == it, idx, out)
    idx_ref[...] = out.astype(jnp.int32)


def _sample_topk(scores, n_slc, qp):
    rows, nbp = scores.shape
    return pl.pallas_call(
        functools.partial(_sample_topk_kernel, n_slc=n_slc, qp=qp), grid=(1,),
        in_specs=[_full(scores.shape)], out_specs=_full((rows, LANES)),
        out_shape=jax.ShapeDtypeStruct((rows, LANES), jnp.int32),
        compiler_params=_cparams(("arbitrary",)), name="sample_topk",
    )(scores)


def _sample_sel_kernel(pt_ref, idx_ref, cache_ref, q_ref, new_ref, osel_ref, buf, sem, *, n_pages, past_len, n_top):
    b = pl.program_id(0)
    nb = pl.num_programs(0)
    per_page = PAGE_SIZE // L_SEL
    n_past_blocks = past_len // L_SEL
    n_cp = N_KV * n_top

    def blk_copy(bb, slot, j):
        blk = jnp.minimum(idx_ref[bb * n_cp + j], n_past_blocks - 1)
        page = pt_ref[bb * n_pages + blk // per_page]
        r0 = pl.multiple_of((blk % per_page) * L_SEL, L_SEL)
        return pltpu.make_async_copy(cache_ref.at[page, pl.ds(r0, L_SEL), pl.ds(256, 256)],
                                     buf.at[slot, j], sem.at[slot])

    def start_all(bb, slot):
        def body(j, c):
            blk_copy(bb, slot, j).start()
            return c
        lax.fori_loop(0, n_cp, body, 0)

    def wait_all(bb, slot):
        def body(j, c):
            blk_copy(bb, slot, j).wait()
            return c
        lax.fori_loop(0, n_cp, body, 0)

    slot = b % 2

    @pl.when(b == 0)
    def _():
        start_all(0, 0)

    @pl.when(b + 1 < nb)
    def _():
        start_all(b + 1, 1 - slot)

    wait_all(b, slot)

    q = q_ref[0]
    new = jnp.broadcast_to(new_ref[0], (8, 512))
    ks_new = new[:, 256:384].astype(BF16)
    vs_new = new[:, 384:512].astype(BF16).astype(F32)
    s_new = _dot_nt(q, ks_new)[:, 0:1] * ATT_SCALE
    nk = n_top * L_SEL
    lane = lax.broadcasted_iota(jnp.int32, (8, nk), 1)
    outs = []
    for kv in range(N_KV):
        kvblk = buf[slot, kv * n_top:(kv + 1) * n_top].reshape(nk, 256)
        ksel = kvblk[:, 0:128].astype(BF16)
        vsel = kvblk[:, 128:256].astype(BF16)
        blk_of_lane = jnp.zeros((8, nk), jnp.int32)
        for n in range(n_top):
            blk_of_lane = jnp.where((lane >> 6) == n, idx_ref[b * n_cp + kv * n_top + n], blk_of_lane)
        kpos = blk_of_lane * L_SEL + (lane & (L_SEL - 1))
        s = jnp.where(kpos < past_len, _dot_nt(q, ksel) * ATT_SCALE, -jnp.inf)
        m = jnp.maximum(jnp.max(s, -1, keepdims=True), s_new)
        e = jnp.exp(s - m)
        e_new = jnp.exp(s_new - m)
        denom = jnp.sum(e, -1, keepdims=True) + e_new
        o = (_dot((e / denom).astype(BF16), vsel) + (e_new / denom).astype(BF16).astype(F32) * vs_new)
        outs.append(o)
    row = lax.broadcasted_iota(jnp.int32, (8, LANES), 0)
    osel_ref[0] = jnp.where(row < GQA, outs[0], outs[1])


def _sample_sel(page_table, idx, cache3, qrot8, kv4_s, past_len, n_top):
    db, n_pages = page_table.shape
    grid_spec = pltpu.PrefetchScalarGridSpec(
        num_scalar_prefetch=2, grid=(db,),
        in_specs=[pl.BlockSpec(memory_space=pl.ANY),
                  pl.BlockSpec((1, 8, 128), lambda b, pt, ix: (b, 0, 0)),
                  pl.BlockSpec((1, 1, 512), lambda b, pt, ix: (b, 0, 0))],
        out_specs=pl.BlockSpec((1, 8, 128), lambda b, pt, ix: (b, 0, 0)),
        scratch_shapes=[pltpu.VMEM((2, N_KV * n_top, L_SEL, 256), F32), pltpu.SemaphoreType.DMA((2,))],
    )
    return pl.pallas_call(
        functools.partial(_sample_sel_kernel, n_pages=n_pages, past_len=past_len, n_top=n_top),
        grid_spec=grid_spec,
        out_shape=jax.ShapeDtypeStruct((db, 8, 128), F32),
        compiler_params=_cparams(("arbitrary",)), name="sample_sel",
    )(page_table.reshape(-1), idx.reshape(-1), cache3, qrot8, kv4_s.reshape(db, 1, 512))


def _sample_win_kernel(sw_ref, new_ref, q_ref, g_ref, ocmp_ref, osel_ref, a_ref, wout_ref):
    tb, wb = sw_ref.shape[0], sw_ref.shape[1]
    rowi = lax.broadcasted_iota(jnp.int32, (wb, 256), 0)
    for j in range(tb):
        st = sw_ref[j]
        new = jnp.broadcast_to(new_ref[j], (wb, 256))
        wout_ref[j] = jnp.where(rowi == wb - 1, new, pltpu.roll(st, wb - 1, 0))
        keys = jnp.where(rowi == 0, new, st)
        s = _dot_nt(q_ref[j], keys[:, 0:128].astype(BF16)) * ATT_SCALE
        o_win = _dot(_softmax_rows(s).astype(BF16), keys[:, 128:256].astype(BF16))
        g = g_ref[j]
        a_ref[j] = g[:, 0:1] * ocmp_ref[j] + g[:, 1:2] * osel_ref[j] + g[:, 2:3] * o_win


def _sample_win(state_win2, win_new, qrot8, gates8, ocmp, osel, tb):
    db, wb, _ = state_win2.shape
    b3 = lambda s: pl.BlockSpec((tb,) + s, lambda i: (i, 0, 0))
    return pl.pallas_call(
        _sample_win_kernel, grid=(db // tb,),
        in_specs=[b3((wb, 256)), b3((1, 256)), b3((8, 128)), b3((8, 128)), b3((8, 128)), b3((8, 128))],
        out_specs=(b3((8, 128)), b3((wb, 256))),
        out_shape=(jax.ShapeDtypeStruct((db, 8, 128), F32), jax.ShapeDtypeStruct((db, wb, 256), F32)),
        compiler_params=_cparams(("arbitrary",)), name="sample_win",
    )(state_win2, win_new.reshape(db, 1, 256), qrot8, gates8, ocmp, osel)


def _rope_tables(pos):
    half = ROT_DIM // 2
    inv = ROPE_THETA ** (-jnp.arange(half, dtype=F32) * 2.0 / ROT_DIM)
    ang = pos.astype(F32)[:, None] * inv[None, :]
    cos, sin = jnp.cos(ang), jnp.sin(ang)
    n = pos.shape[0]
    rest = HEAD_DIM - ROT_DIM
    one, zero, z8 = jnp.ones((n, rest), F32), jnp.zeros((n, rest), F32), jnp.zeros((n, half), F32)
    c = jnp.concatenate([cos, cos, one], 1)
    s1 = jnp.concatenate([z8, sin, zero], 1)
    s2 = jnp.concatenate([-sin, z8, zero], 1)
    return tuple(jnp.tile(a, (1, 2)) for a in (c, s1, s2))


def _proj_weights(w_in):
    sizes = (SSM_WIDTH, CONV_DIM, M_HEADS, ATT_WIDTH, 6 * N_KV * HEAD_DIM, 3 * N_HEADS)
    parts, start = [], 0
    for s in sizes:
        parts.append(w_in[:, start:start + s])
        start += s
    wz, wxbc, wdt, wq, wkv, wg = parts
    zeros = jnp.zeros((D_MODEL, HEAD_DIM), w_in.dtype)
    chunks = []
    for h in range(N_HEADS):
        wh = wq[:, h * HEAD_DIM:(h + 1) * HEAD_DIM]
        chunks += [wh, zeros] if h // GQA == 0 else [zeros, wh]
    wq_pad = jnp.concatenate(chunks, 1)
    wsm = jnp.concatenate([wdt, wg, jnp.zeros((D_MODEL, LANES - M_HEADS - 3 * N_HEADS), w_in.dtype)], 1)
    return tuple(w.astype(BF16) for w in (wz, wxbc, wq_pad, wkv, wsm))


def _cmp_weights(pe_cmp, w1_cmp, b1_cmp, w2_cmp, b2_cmp):
    out = {}
    for i, nm in enumerate("kv"):
        w1 = w1_cmp[i].reshape(2, STRIDE, HEAD_DIM, CMP_HIDDEN)
        wbig = jnp.zeros((STRIDE, LANES, 512), F32)
        for r in range(2):
            for kv in range(N_KV):
                c0 = r * 256 + kv * CMP_HIDDEN
                wbig = wbig.at[:, kv * HEAD_DIM:(kv + 1) * HEAD_DIM, c0:c0 + CMP_HIDDEN].set(w1[r])
        out["w" + nm] = wbig.astype(BF16)
        out["pe" + nm] = jnp.broadcast_to(pe_cmp[i].reshape(1, L_CMP * HEAD_DIM), (8, L_CMP * HEAD_DIM)).astype(BF16)
        out["w1" + nm] = w1_cmp[i].reshape(L_CMP * HEAD_DIM, CMP_HIDDEN).astype(BF16)
        out["b1" + nm] = jnp.tile(b1_cmp[i].reshape(1, CMP_HIDDEN), (1, 2))
        w2 = jnp.zeros((2 * CMP_HIDDEN, LANES), F32)
        for kv in range(N_KV):
            w2 = w2.at[kv * CMP_HIDDEN:(kv + 1) * CMP_HIDDEN, kv * HEAD_DIM:(kv + 1) * HEAD_DIM].set(w2_cmp[i])
        out["w2" + nm] = w2.astype(BF16)
        out["b2" + nm] = jnp.tile(b2_cmp[i].reshape(1, HEAD_DIM), (1, 2))
    return out


def _select_weights(n_blocks, n_cmp_padded):
    b = jnp.arange(n_blocks)[:, None]
    c = jnp.arange(n_cmp_padded)[None, :]
    k = c - 4 * b + 1
    w = jnp.where((k == 0) | (k == 4), 16.0, jnp.where((k >= 1) & (k <= 3), 32.0, 0.0))
    return w.astype(BF16)


def _pad_row(v, n=LANES):
    v = v.reshape(1, -1).astype(F32)
    return jnp.concatenate([v, jnp.zeros((1, n - v.shape[1]), F32)], 1)


def _pad_head_rows(w):
    zeros = jnp.zeros((HEAD_DIM, w.shape[1]), w.dtype)
    chunks = []
    for h in range(N_HEADS):
        wh = w[h * HEAD_DIM:(h + 1) * HEAD_DIM]
        chunks += [wh, zeros] if h // GQA == 0 else [zeros, wh]
    return jnp.concatenate(chunks, 0)


def kernel(x_prompt, x_sample, cache_kv, state_win, state_ssm, state_conv, page_table, w_in, conv_w, conv_b,
           dt_bias, a_log, d_skip, ssm_norm_g, pe_cmp, w1_cmp, b1_cmp, w2_cmp, b2_cmp, att_norm_g, w_out,
           ln1_g, ln1_b, w_ff1, w_ff2, ln2_g, ln2_b):
    bsz, seq, _ = x_prompt.shape
    db, dseq, _ = x_sample.shape
    n_pages = page_table.shape[1]
    past_len = n_pages * PAGE_SIZE
    wb = state_win.shape[1]
    assert bsz == 1 and dseq == 1 and wb == WINDOW and past_len % L_SEL == 0 and seq % SEL_KEY_TILE == 0

    pw = _proj_weights(w_in)
    cw = _cmp_weights(pe_cmp, w1_cmp, b1_cmp, w2_cmp, b2_cmp)
    cb = conv_b.reshape(1, CONV_DIM)
    dtb, alog = _pad_row(dt_bias), _pad_row(a_log)
    dsk = jnp.repeat(d_skip, HEAD_DIM).reshape(1, SSM_WIDTH)
    ng = ssm_norm_g.reshape(1, SSM_WIDTH)
    ow = dict(wom=w_out[0:SSM_WIDTH].astype(BF16), l1g=ln1_g.reshape(1, -1),
              l1b=ln1_b.reshape(1, -1), w1=w_ff1.astype(BF16), w2=w_ff2.astype(BF16),
              l2g=ln2_g.reshape(1, -1), l2b=ln2_b.reshape(1, -1))
    woa = w_out[SSM_WIDTH:D_MODEL]
    ang = att_norm_g.reshape(ATT_WIDTH, 1)
    att_flat = dict(mask=jnp.ones((1, ATT_WIDTH), F32), ang=ang.reshape(1, -1), woa=woa.astype(BF16))
    att_pad = dict(mask=_pad_head_rows(jnp.ones((ATT_WIDTH, 1), F32)).reshape(1, -1),
                   ang=_pad_head_rows(ang).reshape(1, -1), woa=_pad_head_rows(woa).astype(BF16))

    xp = x_prompt.reshape(seq, D_MODEL)
    tm = min(256, seq)
    z, xbc, qun, qrot, kv4, win, att, sm = _project(
        xp, pw, _rope_tables(jnp.arange(seq, dtype=jnp.int32)), tm, min(WINDOW, seq))
    m_out, conv_p, ssm_p = _mamba_prompt(z, xbc, sm, conv_w, cb, dtb, alog, dsk, ng)
    kcc, vcc = _compress_prompt(kv4, cw)
    n_slc = seq // L_SEL
    wsel = _select_weights(n_slc, seq // STRIDE)
    eall = (jnp.arange(n_slc)[:, None] == (jnp.arange(seq) // L_SEL)[None, :]).astype(BF16)
    a_out = _nsa_prompt(qun, qrot, sm, kcc, vcc, att, wsel, eall)
    y_prompt = _output_block(xp, m_out, a_out, att_flat, ow, tm).reshape(bsz, seq, D_MODEL)
    kv_prompt = kv4.reshape(bsz, seq, 4, N_KV, HEAD_DIM)
    win_prompt = win.reshape(bsz, min(WINDOW, seq), 2, N_KV, HEAD_DIM)
    ssm_prompt = ssm_p.reshape(bsz, M_HEADS, HEAD_DIM, D_STATE)
    conv_prompt = conv_p.reshape(bsz, CONV_W - 1, CONV_DIM)

    xs_ = x_sample.reshape(db, D_MODEL)
    pos_s = jnp.full((db,), past_len, jnp.int32)
    z, xbc, qun, qrot, kv4_s, win_s, _, sm = _project(xs_, pw, _rope_tables(pos_s), db, db)
    m_out_s, conv_s, ssm_s = _mamba_sample(
        z, xbc, sm, jnp.transpose(state_conv, (1, 0, 2)), state_ssm.reshape(db, SSM_WIDTH, D_STATE),
        conv_w, cb, dtb, alog, dsk, ng)
    cache3 = cache_kv.reshape(cache_kv.shape[0], PAGE_SIZE, 4 * N_KV * HEAD_DIM)
    n_slc_s = -(-(past_len + dseq) // L_SEL)
    nbp = -(-n_slc_s // LANES) * LANES
    wsel_s = jnp.transpose(_select_weights(nbp, past_len // STRIDE))
    qun8 = qun.reshape(db, N_HEADS, LANES)
    qrot8 = qrot.reshape(db, N_HEADS, LANES)
    o_cmp, scores = _sample_cmp(page_table, cache3, kv4_s, qun8, cw, wsel_s, past_len)
    n_top = min(N_SEL, n_slc_s)
    idx = _sample_topk(scores[:, 0:N_KV, :].reshape(db * N_KV, nbp), n_slc_s, past_len)[:, 0:n_top]
    o_sel = _sample_sel(page_table, idx, cache3, qrot8, kv4_s, past_len, n_top)
    gates8 = jnp.pad(sm[:, M_HEADS:M_HEADS + 3 * N_HEADS].reshape(db, N_HEADS, 3), ((0, 0), (0, 0), (0, LANES - 3)))
    a8, win_sample = _sample_win(state_win.reshape(db, wb, 256), win_s, qrot8, gates8, o_cmp, o_sel, min(8, db))
    y_sample = _output_block(xs_, m_out_s, a8.reshape(db, N_HEADS * LANES), att_pad, ow, db).reshape(db, dseq, D_MODEL)
    kv_sample = kv4_s.reshape(db, dseq, 4, N_KV, HEAD_DIM)
    win_sample = win_sample.reshape(db, wb, 2, N_KV, HEAD_DIM)
    ssm_sample = ssm_s.reshape(db, M_HEADS, HEAD_DIM, D_STATE)
    conv_sample = jnp.transpose(conv_s, (1, 0, 2))

    return (y_prompt, y_sample, kv_prompt, win_prompt, ssm_prompt, conv_prompt,
            kv_sample, win_sample, ssm_sample, conv_sample)
```

```python
import functools

import jax
import jax.numpy as jnp
from jax import lax
from jax.experimental import pallas as pl
from jax.experimental.pallas import tpu as pltpu

F32 = jnp.float32
BF16 = jnp.bfloat16

D_MODEL = 1024
HEAD_DIM = 64
SSM_WIDTH = 512
ATT_WIDTH = 512
M_HEADS = 8
M_GROUPS = 2
D_STATE = 128
CONV_W = 4
CONV_DIM = SSM_WIDTH + 2 * M_GROUPS * D_STATE
SSD_CHUNK = 256
RMS_EPS = 1e-6
N_HEADS = 8
N_KV = 2
GQA = N_HEADS // N_KV
ROT_DIM = 16
ROPE_THETA = 500000.0
L_CMP = 32
STRIDE = 16
CMP_HIDDEN = 128
L_SEL = 64
N_SEL = 16
N_LOCAL = 2
WINDOW = 512
Q_BLOCK = 128
ATT_SCALE = HEAD_DIM ** -0.5
D_FF = 4 * D_MODEL
LN_EPS = 1e-5
DEEPNORM_ALPHA = 2.0 ** 0.25
PAGE_SIZE = 128
LANES = 128
NEG = -1e30
SEL_KEY_TILE = 1024
VMEM_LIMIT = 56 * 1024 * 1024


def _cparams(sem):
    return pltpu.CompilerParams(dimension_semantics=sem, vmem_limit_bytes=VMEM_LIMIT)


def _dot(a, b):
    return jnp.dot(a, b, preferred_element_type=F32)


def _dot_nt(a, b):
    return lax.dot_general(a, b, (((1,), (1,)), ((), ())), preferred_element_type=F32)


def _split3(x):
    hi = x.astype(BF16)
    r = x - hi.astype(F32)
    mid = r.astype(BF16)
    lo = (r - mid.astype(F32)).astype(BF16)
    return hi, mid, lo


def _dot3(x, w):
    hi, mid, lo = _split3(x)
    return _dot(hi, w) + _dot(mid, w) + _dot(lo, w)


def _sigmoid(x):
    return 1.0 / (1.0 + jnp.exp(-x))


def _silu(x):
    return x * _sigmoid(x)


def _softplus(x):
    return jnp.maximum(x, 0.0) + jnp.log1p(jnp.exp(-jnp.abs(x)))


def _layer_norm(x, g, b):
    mu = jnp.mean(x, -1, keepdims=True)
    xc = x - mu
    var = jnp.mean(xc * xc, -1, keepdims=True)
    return xc * lax.rsqrt(var + LN_EPS) * g + b


def _rope128(x, c, s1, s2):
    return x * c + pltpu.roll(x, 8, 1) * s1 + pltpu.roll(x, LANES - 8, 1) * s2


def _full(shape):
    nd = len(shape)
    return pl.BlockSpec(shape, lambda *a: (0,) * nd)


def _resident(shape):
    nd = len(shape)
    return pl.BlockSpec(shape, lambda *a: (0,) * nd, pipeline_mode=pl.Buffered(1))


def _proj_kernel(x_ref, wz_ref, wxbc_ref, wq_ref, wkv_ref, wsm_ref, c_ref, s1_ref, s2_ref,
                 z_ref, xbc_ref, qun_ref, qrot_ref, kv4_ref, win_ref, att_ref, sm_ref):
    x = x_ref[...].astype(BF16)
    c, s1, s2 = c_ref[...], s1_ref[...], s2_ref[...]
    z_ref[...] = _dot(x, wz_ref[...])
    xbc_ref[...] = _dot(x, wxbc_ref[...])
    q = _dot(x, wq_ref[...]) * ATT_SCALE
    qun_ref[...] = q.astype(BF16)
    for h in range(q.shape[1] // LANES):
        sl = slice(h * LANES, (h + 1) * LANES)
        qrot_ref[:, sl] = _rope128(q[:, sl], c, s1, s2).astype(BF16)
    kv = _dot(x, wkv_ref[...])
    ks = _rope128(kv[:, 256:384], c, s1, s2)
    kw = _rope128(kv[:, 512:640], c, s1, s2)
    kv4_ref[:, 0:256] = kv[:, 0:256]
    kv4_ref[:, 256:384] = ks
    kv4_ref[:, 384:512] = kv[:, 384:512]
    win_ref[:, 0:128] = kw
    win_ref[:, 128:256] = kv[:, 640:768]
    att_ref[:, 0:128] = ks.astype(BF16)
    att_ref[:, 128:256] = kv[:, 384:512].astype(BF16)
    att_ref[:, 256:384] = kw.astype(BF16)
    att_ref[:, 384:512] = kv[:, 640:768].astype(BF16)
    sm = _dot(x, wsm_ref[...])
    lane = lax.broadcasted_iota(jnp.int32, sm.shape, 1)
    is_gate = (lane >= M_HEADS) & (lane < M_HEADS + 3 * N_HEADS)
    sm_ref[...] = jnp.where(is_gate, _sigmoid(sm), sm)


def _project(x, weights, tables, tm, win_rows):
    t = x.shape[0]
    nt = t // tm
    nwb = win_rows // tm
    wz, wxbc, wq, wkv, wsm = weights
    qw = wq.shape[1]
    row = lambda w: pl.BlockSpec((tm, w), lambda i: (i, 0))
    out_shape = (
        jax.ShapeDtypeStruct((t, 512), F32),
        jax.ShapeDtypeStruct((t, 1024), F32),
        jax.ShapeDtypeStruct((t, qw), BF16),
        jax.ShapeDtypeStruct((t, qw), BF16),
        jax.ShapeDtypeStruct((t, 512), F32),
        jax.ShapeDtypeStruct((win_rows, 256), F32),
        jax.ShapeDtypeStruct((t, 512), BF16),
        jax.ShapeDtypeStruct((t, 128), F32),
    )
    out_specs = (row(512), row(1024), row(qw), row(qw), row(512),
                 pl.BlockSpec((tm, 256), lambda i: (jnp.maximum(i - (nt - nwb), 0), 0)),
                 row(512), row(128))
    in_specs = [row(D_MODEL), _full(wz.shape), _full(wxbc.shape), _full(wq.shape), _full(wkv.shape),
                _full(wsm.shape), row(128), row(128), row(128)]
    return pl.pallas_call(
        _proj_kernel, grid=(nt,), in_specs=in_specs, out_specs=out_specs, out_shape=out_shape,
        compiler_params=_cparams(("arbitrary",)), name="proj",
    )(x, wz, wxbc, wq, wkv, wsm, *tables)


def _mamba_prompt_kernel(z_ref, xbc_ref, sm_ref, cw_ref, cb_ref, dtb_ref, alog_ref, dsk_ref, ng_ref,
                         y_ref, conv_ref, ssm_ref, ubuf, state):
    c = pl.program_id(0)
    nc = pl.num_programs(0)
    L = SSD_CHUNK

    @pl.when(c == 0)
    def _():
        ubuf[0:8, :] = jnp.zeros((8, CONV_DIM), F32)
        state[...] = jnp.zeros(state.shape, F32)

    u = xbc_ref[...]
    ubuf[8:8 + L, :] = u
    acc = u * cw_ref[3:4, :] + cb_ref[...]
    for k in range(CONV_W - 1):
        acc = acc + ubuf[pl.ds(8 - (CONV_W - 1) + k, L), :] * cw_ref[k:k + 1, :]
    tail = ubuf[L:L + 8, :]
    ubuf[0:8, :] = tail
    xa = _silu(acc)
    xs = xa[:, 0:SSM_WIDTH]
    bm = xa[:, SSM_WIDTH:SSM_WIDTH + 256]
    cm = xa[:, SSM_WIDTH + 256:SSM_WIDTH + 512]

    lane = lax.broadcasted_iota(jnp.int32, (L, LANES), 1)
    head_lane = lane < M_HEADS
    dt = jnp.where(head_lane, _softplus(sm_ref[...] + dtb_ref[...]), 0.0)
    a_row = jnp.where(head_lane[0:1], -jnp.exp(alog_ref[...]), 0.0)
    da = dt * a_row
    ri = lax.broadcasted_iota(jnp.int32, (L, L), 0)
    ci = lax.broadcasted_iota(jnp.int32, (L, L), 1)
    causal = ri >= ci
    tril = jnp.where(causal, 1.0, 0.0).astype(BF16)
    da_hi, da_mid, da_lo = _split3(da)
    cum = _dot(tril, da_hi) + _dot(tril, da_mid) + _dot(tril, da_lo)
    cum_t = cum.T
    er = lax.broadcasted_iota(jnp.int32, (LANES, SSM_WIDTH), 0)
    ec = lax.broadcasted_iota(jnp.int32, (LANES, SSM_WIDTH), 1)
    expand = jnp.where(er == (ec >> 6), 1.0, 0.0).astype(BF16)
    dtx = _dot3(dt, expand)
    cumx = _dot3(cum, expand)
    cumx_last = cumx[L - 1:L, :]
    xc = xs * dtx
    xw = xc * jnp.exp(cumx_last - cumx)
    ecx = jnp.exp(cumx)
    dec_last = jnp.exp(cumx_last)

    half = lax.broadcasted_iota(jnp.int32, (L, LANES), 1) < HEAD_DIM
    y_parts = []
    for g in range(M_GROUPS):
        bg = bm[:, g * D_STATE:(g + 1) * D_STATE].astype(BF16)
        cg = cm[:, g * D_STATE:(g + 1) * D_STATE].astype(BF16)
        gs = slice(g * 256, (g + 1) * 256)
        scores = _dot_nt(cg, bg)
        h_prev = state[:, gs]
        y_off = _dot(cg, h_prev.astype(BF16)) * ecx[:, gs]
        st = _dot(bg.T, xw[:, gs].astype(BF16))
        state[:, gs] = h_prev * dec_last[:, gs] + st
        for pr in range(2):
            ha = g * 4 + pr * 2
            xcp = xc[:, ha * HEAD_DIM:(ha + 2) * HEAD_DIM].astype(BF16)
            outs = []
            for hh in (ha, ha + 1):
                seg = cum[:, hh:hh + 1] - cum_t[hh:hh + 1, :]
                dec = jnp.exp(jnp.where(causal, seg, -jnp.inf))
                outs.append(_dot((scores * dec).astype(BF16), xcp))
            y_parts.append(jnp.where(half, outs[0], outs[1]) + y_off[:, pr * 128:(pr + 1) * 128])
    y = jnp.concatenate(y_parts, axis=1) + dsk_ref[...] * xs
    y = y * _silu(z_ref[...])
    outs = []
    for g in range(M_GROUPS):
        yg = y[:, g * 256:(g + 1) * 256]
        outs.append(yg * lax.rsqrt(jnp.mean(yg * yg, -1, keepdims=True) + RMS_EPS))
    y_ref[...] = (jnp.concatenate(outs, axis=1) * ng_ref[...]).astype(BF16)

    @pl.when(c == nc - 1)
    def _():
        conv_ref[...] = ubuf[pl.ds(8 - (CONV_W - 1), CONV_W - 1), :]
        ssm_ref[...] = state[...].T


def _mamba_prompt(z, xbc, sm, cw, cb, dtb, alog, dsk, ng):
    t = z.shape[0]
    L = SSD_CHUNK
    row = lambda w: pl.BlockSpec((L, w), lambda i: (i, 0))
    return pl.pallas_call(
        _mamba_prompt_kernel, grid=(t // L,),
        in_specs=[row(512), row(1024), row(128), _full(cw.shape), _full(cb.shape), _full(dtb.shape),
                  _full(alog.shape), _full(dsk.shape), _full(ng.shape)],
        out_specs=(row(512), _full((CONV_W - 1, CONV_DIM)), _full((SSM_WIDTH, D_STATE))),
        out_shape=(jax.ShapeDtypeStruct((t, 512), BF16),
                   jax.ShapeDtypeStruct((CONV_W - 1, CONV_DIM), F32),
                   jax.ShapeDtypeStruct((SSM_WIDTH, D_STATE), F32)),
        scratch_shapes=[pltpu.VMEM((L + 8, CONV_DIM), F32), pltpu.VMEM((D_STATE, SSM_WIDTH), F32)],
        compiler_params=_cparams(("arbitrary",)), name="mamba_prompt",
    )(z, xbc, sm, cw, cb, dtb, alog, dsk, ng)


def _mamba_sample_kernel(z_ref, xbc_ref, sm_ref, sc_ref, h0_ref, cw_ref, cb_ref, dtb_ref, alog_ref, dsk_ref,
                         ng_ref, y_ref, conv_ref, h_ref, xct, dect, brow, crow, xcrow, decrow, xsrow, yrow):
    b = pl.program_id(0)
    nb = pl.num_programs(0)
    db = z_ref.shape[0]

    @pl.when(b == 0)
    def _():
        u = xbc_ref[...]
        acc = u * cw_ref[3:4, :] + cb_ref[...]
        for k in range(CONV_W - 1):
            acc = acc + sc_ref[k] * cw_ref[k:k + 1, :]
        conv_ref[0] = sc_ref[1]
        conv_ref[1] = sc_ref[2]
        conv_ref[2] = u
        xa = _silu(acc)
        xs = xa[:, 0:SSM_WIDTH]
        lane = lax.broadcasted_iota(jnp.int32, (db, LANES), 1)
        head_lane = lane < M_HEADS
        dt = jnp.where(head_lane, _softplus(sm_ref[...] + dtb_ref[...]), 0.0)
        da = dt * jnp.where(head_lane[0:1], -jnp.exp(alog_ref[...]), 0.0)
        er = lax.broadcasted_iota(jnp.int32, (LANES, SSM_WIDTH), 0)
        ec = lax.broadcasted_iota(jnp.int32, (LANES, SSM_WIDTH), 1)
        expand = jnp.where(er == (ec >> 6), 1.0, 0.0).astype(BF16)
        xc = xs * _dot3(dt, expand)
        dec = jnp.exp(_dot3(da, expand))
        xct[...] = xc.T
        dect[...] = dec.T
        xcrow[...] = xc
        decrow[...] = dec
        xsrow[...] = xs
        brow[...] = xa[:, SSM_WIDTH:SSM_WIDTH + 256]
        crow[...] = xa[:, SSM_WIDTH + 256:SSM_WIDTH + 512]

    r = lax.broadcasted_iota(jnp.int32, (db, LANES), 0)
    pick = jnp.where(r == b, 1.0, 0.0).astype(BF16)
    colx = _dot3(xct[...], pick)
    cold = _dot3(dect[...], pick)
    bv = brow[pl.ds(b, 1), :]
    cv = crow[pl.ds(b, 1), :]
    h0 = h0_ref[0]
    ys = []
    for g in range(M_GROUPS):
        rs = slice(g * 256, (g + 1) * 256)
        bg = bv[:, g * D_STATE:(g + 1) * D_STATE]
        cg = cv[:, g * D_STATE:(g + 1) * D_STATE]
        h_ref[0, rs, :] = cold[rs] * h0[rs] + colx[rs] * bg
        cb_s = jnp.sum(cg * bg, -1, keepdims=True)
        c8 = jnp.broadcast_to(cg, (8, D_STATE)).astype(BF16)
        y_off = _dot_nt(c8, h0[rs].astype(BF16))[0:1]
        ys.append(y_off * decrow[pl.ds(b, 1), rs] + cb_s * xcrow[pl.ds(b, 1), rs])
    yrow[pl.ds(b, 1), :] = jnp.concatenate(ys, axis=1)

    @pl.when(b == nb - 1)
    def _():
        y = (yrow[...] + dsk_ref[...] * xsrow[...]) * _silu(z_ref[...])
        outs = []
        for g in range(M_GROUPS):
            yg = y[:, g * 256:(g + 1) * 256]
            outs.append(yg * lax.rsqrt(jnp.mean(yg * yg, -1, keepdims=True) + RMS_EPS))
        y_ref[...] = (jnp.concatenate(outs, axis=1) * ng_ref[...]).astype(BF16)


def _mamba_sample(z, xbc, sm, sc_t, h0, cw, cb, dtb, alog, dsk, ng):
    db = z.shape[0]
    vm = lambda s: pltpu.VMEM(s, F32)
    return pl.pallas_call(
        _mamba_sample_kernel, grid=(db,),
        in_specs=[_full(z.shape), _full(xbc.shape), _full(sm.shape), _full(sc_t.shape),
                  pl.BlockSpec((1, SSM_WIDTH, D_STATE), lambda b: (b, 0, 0)),
                  _full(cw.shape), _full(cb.shape), _full(dtb.shape), _full(alog.shape), _full(dsk.shape),
                  _full(ng.shape)],
        out_specs=(_full((db, 512)), _full((CONV_W - 1, db, CONV_DIM)),
                   pl.BlockSpec((1, SSM_WIDTH, D_STATE), lambda b: (b, 0, 0))),
        out_shape=(jax.ShapeDtypeStruct((db, 512), BF16),
                   jax.ShapeDtypeStruct((CONV_W - 1, db, CONV_DIM), F32),
                   jax.ShapeDtypeStruct((db, SSM_WIDTH, D_STATE), F32)),
        scratch_shapes=[vm((SSM_WIDTH, db)), vm((SSM_WIDTH, db)), vm((db, 256)), vm((db, 256)),
                        vm((db, SSM_WIDTH)), vm((db, SSM_WIDTH)), vm((db, SSM_WIDTH)), vm((db, SSM_WIDTH))],
        compiler_params=_cparams(("arbitrary",)), name="mamba_sample",
    )(z, xbc, sm, sc_t, h0, cw, cb, dtb, alog, dsk, ng)


def _stride_block_proj(x_ref, n_blocks, w_ref):
    acc = jnp.zeros((n_blocks, 512), F32)
    for j in range(STRIDE):
        xj = x_ref[pl.ds(j, n_blocks, stride=STRIDE), :].astype(BF16)
        acc = acc + _dot(xj, w_ref[j])
    return acc


def _cmp_proj_kernel(xk_ref, xv_ref, wk_ref, wv_ref, hk_ref, hv_ref):
    n_blocks = xk_ref.shape[0] // STRIDE
    hk_ref[...] = _stride_block_proj(xk_ref, n_blocks, wk_ref)
    hv_ref[...] = _stride_block_proj(xv_ref, n_blocks, wv_ref)


def _cmp_const(pe_ref, w1f_ref, b1_ref):
    c = _dot(pe_ref[...], w1f_ref[...])[0:1]
    return jnp.concatenate([c, c], axis=1) + b1_ref[...]


def _cmp_mlp(h0, h1_next, const, w2_ref, b2_ref):
    pre = h0 + h1_next + const
    return _dot(_silu(pre).astype(BF16), w2_ref[...]) + b2_ref[...]


def _cmp_finish_kernel(hk_ref, hv_ref, pek, w1k, b1k, w2k, b2k, pev, w1v, b1v, w2v, b2v, kcc_ref, vcc_ref):
    n = hk_ref.shape[0]
    for h_ref, pe, w1f, b1, w2, b2, o_ref in ((hk_ref, pek, w1k, b1k, w2k, b2k, kcc_ref),
                                              (hv_ref, pev, w1v, b1v, w2v, b2v, vcc_ref)):
        h = h_ref[...]
        h1_next = pltpu.roll(h[:, 256:512], n - 1, 0)
        o_ref[...] = _cmp_mlp(h[:, 0:256], h1_next, _cmp_const(pe, w1f, b1), w2, b2).astype(BF16)


def _compress_prompt(kv4, cw):
    t = kv4.shape[0]
    rows = min(t, 2048)
    n_blocks = t // STRIDE
    hk, hv = pl.pallas_call(
        _cmp_proj_kernel, grid=(t // rows,),
        in_specs=[pl.BlockSpec((rows, LANES), lambda i: (i, 0)), pl.BlockSpec((rows, LANES), lambda i: (i, 1)),
                  _full(cw["wk"].shape), _full(cw["wv"].shape)],
        out_specs=(pl.BlockSpec((rows // STRIDE, 512), lambda i: (i, 0)),) * 2,
        out_shape=(jax.ShapeDtypeStruct((n_blocks, 512), F32),) * 2,
        compiler_params=_cparams(("arbitrary",)), name="cmp_proj",
    )(kv4, kv4, cw["wk"], cw["wv"])
    names = ("pek", "w1k", "b1k", "w2k", "b2k", "pev", "w1v", "b1v", "w2v", "b2v")
    args = [cw[k] for k in names]
    return pl.pallas_call(
        _cmp_finish_kernel, grid=(1,),
        in_specs=[_full(hk.shape), _full(hv.shape)] + [_full(a.shape) for a in args],
        out_specs=(_full((n_blocks, 128)),) * 2,
        out_shape=(jax.ShapeDtypeStruct((n_blocks, 128), BF16),) * 2,
        compiler_params=_cparams(("arbitrary",)), name="cmp_finish",
    )(hk, hv, *args)


def _softmax_rows(s):
    m = jnp.max(s, -1, keepdims=True)
    m = jnp.where(jnp.isfinite(m), m, 0.0)
    e = jnp.exp(s - m)
    return e / jnp.maximum(jnp.sum(e, -1, keepdims=True), 1e-30)


def _topk_mask_cols(sc, blkf, n_top):
    taken = jnp.zeros(sc.shape, F32)
    for _ in range(n_top):
        m = jnp.max(sc, 0, keepdims=True)
        cand = (sc == m) & (taken == 0.0)
        idx = jnp.min(jnp.where(cand, blkf, 1e9), 0, keepdims=True)
        pick = blkf == idx
        taken = jnp.where(pick, 1.0, taken)
        sc = jnp.where(pick, -jnp.inf, sc)
    return taken


def _nsa_prompt_kernel(qun_ref, qrot_ref, sm_ref, kcc_ref, vcc_ref, att_ref, wsel_ref, eall_ref, o_ref,
                       m_scr, l_scr, acc_scr):
    i = pl.program_id(0)
    s0 = i * Q_BLOCK
    ncp = kcc_ref.shape[0]
    nb = wsel_ref.shape[0]
    kt = min(SEL_KEY_TILE, att_ref.shape[0])
    n_top = min(N_SEL, nb)

    def head(ref, h):
        return ref[:, h * LANES:(h + 1) * LANES]

    colc = lax.broadcasted_iota(jnp.int32, (Q_BLOCK, ncp), 1)
    rowq = lax.broadcasted_iota(jnp.int32, (Q_BLOCK, ncp), 0) + s0
    vis = colc * STRIDE + (L_CMP - 1) <= rowq
    kcc = kcc_ref[...]
    vcc = vcc_ref[...]
    o_cmp, imps = [], []
    for kv in range(N_KV):
        imp = jnp.zeros((Q_BLOCK, ncp), F32)
        for g in range(GQA):
            s = _dot_nt(head(qun_ref, kv * GQA + g), kcc)
            p = _softmax_rows(jnp.where(vis, s, -jnp.inf))
            imp = imp + p
            o_cmp.append(_dot(p.astype(BF16), vcc))
        imps.append(imp)

    blk = lax.broadcasted_iota(jnp.int32, (nb, Q_BLOCK), 0)
    qpl = lax.broadcasted_iota(jnp.int32, (nb, Q_BLOCK), 1) + s0
    cur = qpl >> 6
    valid = blk * L_SEL <= qpl
    forced = (blk == 0) | ((blk <= cur) & (blk > cur - N_LOCAL))
    blkf = blk.astype(F32)
    wsel = wsel_ref[...]

    o_sel, o_win = [], []
    rowk = lax.broadcasted_iota(jnp.int32, (Q_BLOCK, kt), 0) + s0
    lanek = lax.broadcasted_iota(jnp.int32, (Q_BLOCK, kt), 1)
    n_tiles = (s0 + Q_BLOCK - 1) // kt + 1
    for kv in range(N_KV):
        hi, mid, lo = _split3(imps[kv])
        sc = _dot_nt(wsel, hi) + _dot_nt(wsel, mid) + _dot_nt(wsel, lo)
        sc = jnp.where(forced, jnp.inf, jnp.where(valid, sc, -jnp.inf))
        sel = _topk_mask_cols(sc, blkf, n_top).T.astype(BF16)
        for g in range(GQA):
            m_scr[g] = jnp.full((Q_BLOCK, LANES), NEG, F32)
            l_scr[g] = jnp.zeros((Q_BLOCK, LANES), F32)
            acc_scr[g] = jnp.zeros((Q_BLOCK, LANES), F32)

        def tile_body(t, carry):
            k0 = pl.multiple_of(t * kt, kt)
            ks = att_ref[pl.ds(k0, kt), 0:128]
            vs = att_ref[pl.ds(k0, kt), 128:256]
            picked = _dot(sel, eall_ref[:, pl.ds(k0, kt)])
            ok = (picked > 0.5) & (lanek + k0 <= rowk)
            for g in range(GQA):
                s = jnp.where(ok, _dot_nt(head(qrot_ref, kv * GQA + g), ks), NEG)
                m_old = m_scr[g]
                m_new = jnp.maximum(m_old, jnp.max(s, -1, keepdims=True))
                alpha = jnp.exp(m_old - m_new)
                p = jnp.exp(s - jnp.tile(m_new, (1, kt // LANES)))
                l_scr[g] = alpha * l_scr[g] + jnp.sum(p, -1, keepdims=True)
                acc_scr[g] = alpha * acc_scr[g] + _dot(p.astype(BF16), vs)
                m_scr[g] = m_new
            return carry

        lax.fori_loop(0, n_tiles, tile_body, 0)
        for g in range(GQA):
            o_sel.append(acc_scr[g] / l_scr[g])

    wk = WINDOW + Q_BLOCK
    w0 = pl.multiple_of(jnp.maximum(s0 - WINDOW, 0), Q_BLOCK)
    kw = att_ref[pl.ds(w0, wk), 256:384]
    vw = att_ref[pl.ds(w0, wk), 384:512]
    kposw = lax.broadcasted_iota(jnp.int32, (Q_BLOCK, wk), 1) + w0
    roww = lax.broadcasted_iota(jnp.int32, (Q_BLOCK, wk), 0) + s0
    okw = (kposw <= roww) & (kposw > roww - WINDOW)
    for h in range(N_HEADS):
        s = _dot_nt(head(qrot_ref, h), kw)
        p = _softmax_rows(jnp.where(okw, s, -jnp.inf))
        o_win.append(_dot(p.astype(BF16), vw))

    sm = sm_ref[...]
    lane = lax.broadcasted_iota(jnp.int32, (Q_BLOCK, LANES), 1)
    chunks = []
    for c in range(N_HEADS // 2):
        parts = []
        for h in (2 * c, 2 * c + 1):
            kv = h // GQA
            gc = M_HEADS + 3 * h
            a = sm[:, gc:gc + 1] * o_cmp[h] + sm[:, gc + 1:gc + 2] * o_sel[h] + sm[:, gc + 2:gc + 3] * o_win[h]
            a = jnp.where((lane >= kv * HEAD_DIM) & (lane < (kv + 1) * HEAD_DIM), a, 0.0)
            if kv != h % 2:
                a = pltpu.roll(a, HEAD_DIM, 1)
            parts.append(a)
        chunks.append(parts[0] + parts[1])
    o_ref[...] = jnp.concatenate(chunks, axis=1)


def _nsa_prompt(qun, qrot, sm, kcc, vcc, att, wsel, eall):
    t = qun.shape[0]
    row = lambda w: pl.BlockSpec((Q_BLOCK, w), lambda i: (i, 0))
    return pl.pallas_call(
        _nsa_prompt_kernel, grid=(t // Q_BLOCK,),
        in_specs=[row(1024), row(1024), row(128), _resident(kcc.shape), _resident(vcc.shape),
                  _resident(att.shape), _resident(wsel.shape), _resident(eall.shape)],
        out_specs=row(512),
        out_shape=jax.ShapeDtypeStruct((t, 512), F32),
        scratch_shapes=[pltpu.VMEM((GQA, Q_BLOCK, LANES), F32)] * 3,
        compiler_params=_cparams(("arbitrary",)), name="nsa_prompt",
    )(qun, qrot, sm, kcc, vcc, att, wsel, eall)


def _out_kernel(x_ref, m_ref, a_ref, ang_ref, wom_ref, woa_ref, l1g_ref, l1b_ref, w1_ref, w2_ref,
                l2g_ref, l2b_ref, y_ref):
    a = a_ref[...]
    an = a * lax.rsqrt(jnp.mean(a * a, -1, keepdims=True) + RMS_EPS) * ang_ref[...]
    mix = _dot(m_ref[...], wom_ref[...]) + _dot(an.astype(BF16), woa_ref[...])
    h = _layer_norm(DEEPNORM_ALPHA * x_ref[...] + mix, l1g_ref[...], l1b_ref[...])
    hb = h.astype(BF16)
    f = jnp.zeros(h.shape, F32)
    for c in range(D_FF // 1024):
        cs = slice(c * 1024, (c + 1) * 1024)
        u = jnp.maximum(_dot(hb, w1_ref[:, cs]), 0.0)
        f = f + _dot((u * u).astype(BF16), w2_ref[cs, :])
    y_ref[...] = _layer_norm(DEEPNORM_ALPHA * h + f, l2g_ref[...], l2b_ref[...])


def _output_block(x, m_out, a_out, att, ow, tm):
    t = x.shape[0]
    row = lambda w: pl.BlockSpec((tm, w), lambda i: (i, 0))
    names = ("l1g", "l1b", "w1", "w2", "l2g", "l2b")
    args = [att["ang"], ow["wom"], att["woa"]] + [ow[k] for k in names]
    return pl.pallas_call(
        _out_kernel, grid=(t // tm,),
        in_specs=[row(D_MODEL), row(512), row(a_out.shape[1])] + [_resident(a.shape) for a in args],
        out_specs=row(D_MODEL),
        out_shape=jax.ShapeDtypeStruct((t, D_MODEL), F32),
        compiler_params=_cparams(("arbitrary",)), name="out_block",
    )(x, m_out, a_out, *args)


def _sample_cmp_kernel(pt_ref, cache_ref, new_ref, q_ref, wk_ref, wv_ref,
                       pek, w1k, b1k, w2k, b2k, pev, w1v, b1v, w2v, b2v, wsel_ref,
                       ocmp_ref, score_ref, buf, xs, sem, *, n_pages, past_len):
    b = pl.program_id(0)
    nb = pl.num_programs(0)
    n_blocks = n_pages * PAGE_SIZE // STRIDE

    def page_copy(bb, slot, pg):
        return pltpu.make_async_copy(cache_ref.at[pt_ref[bb * n_pages + pg], pl.ds(0, 2)],
                                     buf.at[slot, pg], sem.at[slot])

    def start_all(bb, slot):
        def body(pg, carry):
            page_copy(bb, slot, pg).start()
            return carry
        lax.fori_loop(0, n_pages, body, 0)

    def wait_all(bb, slot):
        def body(pg, carry):
            page_copy(bb, slot, pg).wait()
            return carry
        lax.fori_loop(0, n_pages, body, 0)

    slot = b % 2

    @pl.when(b == 0)
    def _():
        start_all(0, 0)

    @pl.when(b + 1 < nb)
    def _():
        start_all(b + 1, 1 - slot)

    wait_all(b, slot)

    def to_rows(pg, carry):
        r0 = pl.multiple_of(pg * PAGE_SIZE, PAGE_SIZE)
        for c in range(2):
            xs[c, pl.ds(r0, PAGE_SIZE), :] = buf[slot, pg, c].T
        return carry
    lax.fori_loop(0, n_pages, to_rows, 0)

    new = jnp.broadcast_to(new_ref[0], (8, 512))
    rowi = lax.broadcasted_iota(jnp.int32, (n_blocks, 256), 0)
    toks = []
    for c, w_ref, pe, w1f, b1, w2, b2 in ((0, wk_ref, pek, w1k, b1k, w2k, b2k),
                                          (1, wv_ref, pev, w1v, b1v, w2v, b2v)):
        h = _stride_block_proj(xs.at[c], n_blocks, w_ref)
        h_new = _dot(new[:, c * LANES:(c + 1) * LANES].astype(BF16), w_ref[0][:, 256:512])[0:1]
        h1_next = jnp.where(rowi == n_blocks - 1, h_new, pltpu.roll(h[:, 256:512], n_blocks - 1, 0))
        toks.append(_cmp_mlp(h[:, 0:256], h1_next, _cmp_const(pe, w1f, b1), w2, b2).astype(BF16))
    kcc, vcc = toks

    q = q_ref[0]
    zq = jnp.zeros_like(q)
    rowh = lax.broadcasted_iota(jnp.int32, (N_HEADS, LANES), 0)
    qpad = jnp.where(rowh < GQA, jnp.concatenate([q, zq], 1), jnp.concatenate([zq, q], 1))
    s = _dot_nt(qpad, kcc)
    colc = lax.broadcasted_iota(jnp.int32, s.shape, 1)
    p = _softmax_rows(jnp.where(colc * STRIDE + (L_CMP - 1) <= past_len, s, -jnp.inf))
    o = _dot(p.astype(BF16), vcc)
    ocmp_ref[0] = jnp.where(rowh[:, 0:HEAD_DIM] < GQA, o[:, 0:HEAD_DIM], o[:, HEAD_DIM:])
    imp0 = p[0:1] + p[1:2] + p[2:3] + p[3:4]
    imp1 = p[4:5] + p[5:6] + p[6:7] + p[7:8]
    rowp = lax.broadcasted_iota(jnp.int32, p.shape, 0)
    imp = jnp.where(rowp == 0, imp0, jnp.where(rowp == 1, imp1, 0.0))
    score_ref[0] = _dot3(imp, wsel_ref[...])


def _sample_cmp(page_table, cache4, kv4_s, q8, cw, wsel_s, past_len):
    db, n_pages = page_table.shape
    names = ("pek", "w1k", "b1k", "w2k", "b2k", "pev", "w1v", "b1v", "w2v", "b2v")
    args = [cw[k] for k in names]
    nbp = wsel_s.shape[1]
    full = lambda a: pl.BlockSpec(a.shape, lambda b, pt: (0,) * a.ndim)
    grid_spec = pltpu.PrefetchScalarGridSpec(
        num_scalar_prefetch=1, grid=(db,),
        in_specs=[pl.BlockSpec(memory_space=pl.ANY),
                  pl.BlockSpec((1, 1, 512), lambda b, pt: (b, 0, 0)),
                  pl.BlockSpec((1, N_HEADS, HEAD_DIM), lambda b, pt: (b, 0, 0)),
                  full(cw["wk"]), full(cw["wv"])] + [full(a) for a in args] + [full(wsel_s)],
        out_specs=(pl.BlockSpec((1, N_HEADS, HEAD_DIM), lambda b, pt: (b, 0, 0)),
                   pl.BlockSpec((1, 8, nbp), lambda b, pt: (b, 0, 0))),
        scratch_shapes=[pltpu.VMEM((2, n_pages, 2, LANES, PAGE_SIZE), F32),
                        pltpu.VMEM((2, n_pages * PAGE_SIZE, LANES), F32),
                        pltpu.SemaphoreType.DMA((2,))],
    )
    return pl.pallas_call(
        functools.partial(_sample_cmp_kernel, n_pages=n_pages, past_len=past_len),
        grid_spec=grid_spec,
        out_shape=(jax.ShapeDtypeStruct((db, N_HEADS, HEAD_DIM), F32), jax.ShapeDtypeStruct((db, 8, nbp), F32)),
        compiler_params=_cparams(("arbitrary",)), name="sample_cmp",
    )(page_table.reshape(-1), cache4, kv4_s.reshape(db, 1, 512), q8, cw["wk"], cw["wv"], *args, wsel_s)


def _sample_topk_kernel(score_ref, idx_ref, *, n_slc, qp):
    sc = score_ref[...]
    blk = lax.broadcasted_iota(jnp.int32, sc.shape, 1)
    cur = qp // L_SEL
    valid = (blk * L_SEL <= qp) & (blk < n_slc)
    forced = (blk == 0) | ((blk <= cur) & (blk > cur - N_LOCAL))
    sc = jnp.where(forced & (blk < n_slc), jnp.inf, jnp.where(valid, sc, -jnp.inf))
    real = blk < n_slc
    blkf = blk.astype(F32)
    taken = jnp.where(real, 0.0, 1.0)
    out = jnp.zeros(idx_ref.shape, F32)
    lane = lax.broadcasted_iota(jnp.int32, idx_ref.shape, 1)
    for it in range(min(N_SEL, n_slc)):
        m = jnp.max(jnp.where(taken == 0.0, sc, -jnp.inf), -1, keepdims=True)
        cand = (sc == m) & (taken == 0.0)
        idx = jnp.min(jnp.where(cand, blkf, 1e9), -1, keepdims=True)
        pick = blkf == idx
        taken = jnp.where(pick, 1.0, taken)
        out = jnp.where(lane == it, idx, out)
    idx_ref[...] = out.astype(jnp.int32)


def _sample_topk(scores, n_slc, qp):
    rows, nbp = scores.shape
    return pl.pallas_call(
        functools.partial(_sample_topk_kernel, n_slc=n_slc, qp=qp), grid=(1,),
        in_specs=[_full(scores.shape)], out_specs=_full((rows, LANES)),
        out_shape=jax.ShapeDtypeStruct((rows, LANES), jnp.int32),
        compiler_params=_cparams(("arbitrary",)), name="sample_topk",
    )(scores)


def _sample_sel_kernel(pt_ref, idx_ref, cache_ref, q_ref, new_ref, osel_ref, buf, sem, *, n_pages, past_len, n_top):
    b = pl.program_id(0)
    nb = pl.num_programs(0)
    per_page = PAGE_SIZE // L_SEL
    n_past_blocks = past_len // L_SEL
    n_cp = N_KV * n_top

    def blk_copy(bb, slot, j):
        kv = j // n_top
        n = j - kv * n_top
        blk = jnp.minimum(idx_ref[bb * n_cp + j], n_past_blocks - 1)
        page = pt_ref[bb * n_pages + blk // per_page]
        return pltpu.make_async_copy(
            cache_ref.at[page, pl.ds(2, 2), pl.ds(pl.multiple_of(kv * HEAD_DIM, HEAD_DIM), HEAD_DIM), :],
            buf.at[slot, kv, :, :, pl.ds(pl.multiple_of(n * PAGE_SIZE, PAGE_SIZE), PAGE_SIZE)], sem.at[slot])

    def start_all(bb, slot):
        def body(j, c):
            blk_copy(bb, slot, j).start()
            return c
        lax.fori_loop(0, n_cp, body, 0)

    def wait_all(bb, slot):
        def body(j, c):
            blk_copy(bb, slot, j).wait()
            return c
        lax.fori_loop(0, n_cp, body, 0)

    slot = b % 2

    @pl.when(b == 0)
    def _():
        start_all(0, 0)

    @pl.when(b + 1 < nb)
    def _():
        start_all(b + 1, 1 - slot)

    wait_all(b, slot)

    q = q_ref[0]
    new = jnp.broadcast_to(new_ref[0], (8, 512))
    nk = n_top * PAGE_SIZE
    lane = lax.broadcasted_iota(jnp.int32, (N_HEADS, nk), 1)
    outs = []
    for kv in range(N_KV):
        ks_new = new[:, 256 + kv * HEAD_DIM:256 + (kv + 1) * HEAD_DIM].astype(BF16)
        vs_new = new[:, 384 + kv * HEAD_DIM:384 + (kv + 1) * HEAD_DIM].astype(BF16).astype(F32)
        s_new = _dot_nt(q, ks_new)[:, 0:1]
        kt = buf[slot, kv, 0].astype(BF16)
        vt = buf[slot, kv, 1].astype(BF16)
        blk_of_lane = jnp.zeros((N_HEADS, nk), jnp.int32)
        for n in range(n_top):
            blk_of_lane = jnp.where((lane >> 7) == n, idx_ref[b * n_cp + kv * n_top + n], blk_of_lane)
        in_block = ((lane & (PAGE_SIZE - 1)) >> 6) == (blk_of_lane & (per_page - 1))
        ok = in_block & (blk_of_lane < n_past_blocks)
        s = jnp.where(ok, _dot(q, kt), -jnp.inf)
        m = jnp.maximum(jnp.max(s, -1, keepdims=True), s_new)
        e = jnp.exp(s - m)
        e_new = jnp.exp(s_new - m)
        denom = jnp.sum(e, -1, keepdims=True) + e_new
        o = _dot_nt((e / denom).astype(BF16), vt) + (e_new / denom).astype(BF16).astype(F32) * vs_new
        outs.append(o)
    row = lax.broadcasted_iota(jnp.int32, (N_HEADS, HEAD_DIM), 0)
    osel_ref[0] = jnp.where(row < GQA, outs[0], outs[1])


def _sample_sel(page_table, idx, cache4, qrot8, kv4_s, past_len, n_top):
    db, n_pages = page_table.shape
    assert PAGE_SIZE // L_SEL == 2
    hb = pl.BlockSpec((1, N_HEADS, HEAD_DIM), lambda b, pt, ix: (b, 0, 0))
    grid_spec = pltpu.PrefetchScalarGridSpec(
        num_scalar_prefetch=2, grid=(db,),
        in_specs=[pl.BlockSpec(memory_space=pl.ANY), hb,
                  pl.BlockSpec((1, 1, 512), lambda b, pt, ix: (b, 0, 0))],
        out_specs=hb,
        scratch_shapes=[pltpu.VMEM((2, N_KV, 2, HEAD_DIM, n_top * PAGE_SIZE), F32), pltpu.SemaphoreType.DMA((2,))],
    )
    return pl.pallas_call(
        functools.partial(_sample_sel_kernel, n_pages=n_pages, past_len=past_len, n_top=n_top),
        grid_spec=grid_spec,
        out_shape=jax.ShapeDtypeStruct((db, N_HEADS, HEAD_DIM), F32),
        compiler_params=_cparams(("arbitrary",)), name="sample_sel",
    )(page_table.reshape(-1), idx.reshape(-1), cache4, qrot8, kv4_s.reshape(db, 1, 512))


def _sample_win_kernel(st_ref, newrow_ref, newall_ref, q_ref, g_ref, ocmp_ref, osel_ref, a_ref, wout_ref):
    i = pl.program_id(0)
    tb, wb = st_ref.shape[0], st_ref.shape[3]
    db = newall_ref.shape[0]
    new_t = newall_ref[...].T
    rowb = lax.broadcasted_iota(jnp.int32, (db, LANES), 0)
    lane_w = lax.broadcasted_iota(jnp.int32, (HEAD_DIM, wb), 1)
    lane_s = lax.broadcasted_iota(jnp.int32, (N_HEADS, wb), 1)
    rowh = lax.broadcasted_iota(jnp.int32, (N_HEADS, HEAD_DIM), 0)
    for j in range(tb):
        pick = jnp.where(rowb == i * tb + j, 1.0, 0.0).astype(BF16)
        col = _dot3(new_t, pick)
        for ckv in range(2 * N_KV):
            colw = jnp.tile(col[ckv * HEAD_DIM:(ckv + 1) * HEAD_DIM], (1, wb // LANES))
            wout_ref[j, ckv] = jnp.where(lane_w == wb - 1, colw, pltpu.roll(st_ref[j, ckv], wb - 1, 1))
        new = jnp.broadcast_to(newrow_ref[j], (8, 256))
        q = q_ref[j]
        outs = []
        for kv in range(N_KV):
            kt = st_ref[j, kv].astype(BF16)
            vt = st_ref[j, N_KV + kv].astype(BF16)
            kw_new = new[:, kv * HEAD_DIM:(kv + 1) * HEAD_DIM].astype(BF16)
            vw_new = new[:, 128 + kv * HEAD_DIM:128 + (kv + 1) * HEAD_DIM].astype(BF16).astype(F32)
            s_new = _dot_nt(q, kw_new)[:, 0:1]
            s = jnp.where(lane_s >= 1, _dot(q, kt), -jnp.inf)
            m = jnp.maximum(jnp.max(s, -1, keepdims=True), s_new)
            e = jnp.exp(s - m)
            e_new = jnp.exp(s_new - m)
            denom = jnp.sum(e, -1, keepdims=True) + e_new
            outs.append(_dot_nt((e / denom).astype(BF16), vt) + (e_new / denom).astype(BF16).astype(F32) * vw_new)
        o_win = jnp.where(rowh < GQA, outs[0], outs[1])
        g = g_ref[j]
        a_ref[j] = g[:, 0:1] * ocmp_ref[j] + g[:, 1:2] * osel_ref[j] + g[:, 2:3] * o_win


def _sample_win(state4, win_new, qrot8, gates8, ocmp, osel, tb):
    db, _, _, wb = state4.shape
    b3 = lambda s: pl.BlockSpec((tb,) + s, lambda i: (i, 0, 0))
    b4 = pl.BlockSpec((tb, 2 * N_KV, HEAD_DIM, wb), lambda i: (i, 0, 0, 0))
    hd = (N_HEADS, HEAD_DIM)
    return pl.pallas_call(
        _sample_win_kernel, grid=(db // tb,),
        in_specs=[b4, b3((1, 256)), _full(win_new.shape), b3(hd), b3((8, 128)), b3(hd), b3(hd)],
        out_specs=(b3(hd), b4),
        out_shape=(jax.ShapeDtypeStruct((db,) + hd, F32), jax.ShapeDtypeStruct(state4.shape, F32)),
        compiler_params=_cparams(("arbitrary",)), name="sample_win",
    )(state4, win_new.reshape(db, 1, 256), win_new, qrot8, gates8, ocmp, osel)


def _rope_tables(pos):
    half = ROT_DIM // 2
    inv = ROPE_THETA ** (-jnp.arange(half, dtype=F32) * 2.0 / ROT_DIM)
    ang = pos.astype(F32)[:, None] * inv[None, :]
    cos, sin = jnp.cos(ang), jnp.sin(ang)
    n = pos.shape[0]
    rest = HEAD_DIM - ROT_DIM
    one, zero, z8 = jnp.ones((n, rest), F32), jnp.zeros((n, rest), F32), jnp.zeros((n, half), F32)
    c = jnp.concatenate([cos, cos, one], 1)
    s1 = jnp.concatenate([z8, sin, zero], 1)
    s2 = jnp.concatenate([-sin, z8, zero], 1)
    return tuple(jnp.tile(a, (1, 2)) for a in (c, s1, s2))


def _proj_weights(w_in):
    sizes = (SSM_WIDTH, CONV_DIM, M_HEADS, ATT_WIDTH, 6 * N_KV * HEAD_DIM, 3 * N_HEADS)
    parts, start = [], 0
    for s in sizes:
        parts.append(w_in[:, start:start + s])
        start += s
    wz, wxbc, wdt, wq, wkv, wg = parts
    zeros = jnp.zeros((D_MODEL, HEAD_DIM), w_in.dtype)
    chunks = []
    for h in range(N_HEADS):
        wh = wq[:, h * HEAD_DIM:(h + 1) * HEAD_DIM]
        chunks += [wh, zeros] if h // GQA == 0 else [zeros, wh]
    wq_pad = jnp.concatenate(chunks, 1)
    wsm = jnp.concatenate([wdt, wg, jnp.zeros((D_MODEL, LANES - M_HEADS - 3 * N_HEADS), w_in.dtype)], 1)
    wz, wxbc, wq, wq_pad, wkv, wsm = (w.astype(BF16) for w in (wz, wxbc, wq, wq_pad, wkv, wsm))
    return (wz, wxbc, wq_pad, wkv, wsm), (wz, wxbc, wq, wkv, wsm)


def _cmp_weights(pe_cmp, w1_cmp, b1_cmp, w2_cmp, b2_cmp):
    out = {}
    for i, nm in enumerate("kv"):
        w1 = w1_cmp[i].reshape(2, STRIDE, HEAD_DIM, CMP_HIDDEN)
        wbig = jnp.zeros((STRIDE, LANES, 512), F32)
        for r in range(2):
            for kv in range(N_KV):
                c0 = r * 256 + kv * CMP_HIDDEN
                wbig = wbig.at[:, kv * HEAD_DIM:(kv + 1) * HEAD_DIM, c0:c0 + CMP_HIDDEN].set(w1[r])
        out["w" + nm] = wbig.astype(BF16)
        out["pe" + nm] = jnp.broadcast_to(pe_cmp[i].reshape(1, L_CMP * HEAD_DIM), (8, L_CMP * HEAD_DIM)).astype(BF16)
        out["w1" + nm] = w1_cmp[i].reshape(L_CMP * HEAD_DIM, CMP_HIDDEN).astype(BF16)
        out["b1" + nm] = jnp.tile(b1_cmp[i].reshape(1, CMP_HIDDEN), (1, 2))
        w2 = jnp.zeros((2 * CMP_HIDDEN, LANES), F32)
        for kv in range(N_KV):
            w2 = w2.at[kv * CMP_HIDDEN:(kv + 1) * CMP_HIDDEN, kv * HEAD_DIM:(kv + 1) * HEAD_DIM].set(w2_cmp[i])
        out["w2" + nm] = w2.astype(BF16)
        out["b2" + nm] = jnp.tile(b2_cmp[i].reshape(1, HEAD_DIM), (1, 2))
    return out


def _select_weights(n_blocks, n_cmp_padded):
    b = jnp.arange(n_blocks)[:, None]
    c = jnp.arange(n_cmp_padded)[None, :]
    k = c - 4 * b + 1
    w = jnp.where((k == 0) | (k == 4), 16.0, jnp.where((k >= 1) & (k <= 3), 32.0, 0.0))
    return w.astype(BF16)


def _pad_row(v, n=LANES):
    v = v.reshape(1, -1).astype(F32)
    return jnp.concatenate([v, jnp.zeros((1, n - v.shape[1]), F32)], 1)


def kernel(x_prompt, x_sample, cache_kv, state_win, state_ssm, state_conv, page_table, w_in, conv_w, conv_b,
           dt_bias, a_log, d_skip, ssm_norm_g, pe_cmp, w1_cmp, b1_cmp, w2_cmp, b2_cmp, att_norm_g, w_out,
           ln1_g, ln1_b, w_ff1, w_ff2, ln2_g, ln2_b):
    bsz, seq, _ = x_prompt.shape
    db, dseq, _ = x_sample.shape
    n_pages = page_table.shape[1]
    past_len = n_pages * PAGE_SIZE
    wb = state_win.shape[1]
    assert bsz == 1 and dseq == 1 and wb == WINDOW and past_len % L_SEL == 0 and seq % SEL_KEY_TILE == 0

    pw, pw_flat = _proj_weights(w_in)
    cw = _cmp_weights(pe_cmp, w1_cmp, b1_cmp, w2_cmp, b2_cmp)
    cb = conv_b.reshape(1, CONV_DIM)
    dtb, alog = _pad_row(dt_bias), _pad_row(a_log)
    dsk = jnp.repeat(d_skip, HEAD_DIM).reshape(1, SSM_WIDTH)
    ng = ssm_norm_g.reshape(1, SSM_WIDTH)
    ow = dict(wom=w_out[0:SSM_WIDTH].astype(BF16), l1g=ln1_g.reshape(1, -1),
              l1b=ln1_b.reshape(1, -1), w1=w_ff1.astype(BF16), w2=w_ff2.astype(BF16),
              l2g=ln2_g.reshape(1, -1), l2b=ln2_b.reshape(1, -1))
    woa = w_out[SSM_WIDTH:D_MODEL]
    ang = att_norm_g.reshape(ATT_WIDTH, 1)
    att_flat = dict(ang=ang.reshape(1, -1), woa=woa.astype(BF16))

    xp = x_prompt.reshape(seq, D_MODEL)
    tm = min(256, seq)
    z, xbc, qun, qrot, kv4, win, att, sm = _project(
        xp, pw, _rope_tables(jnp.arange(seq, dtype=jnp.int32)), tm, min(WINDOW, seq))
    m_out, conv_p, ssm_p = _mamba_prompt(z, xbc, sm, conv_w, cb, dtb, alog, dsk, ng)
    kcc, vcc = _compress_prompt(kv4, cw)
    n_slc = seq // L_SEL
    wsel = _select_weights(n_slc, seq // STRIDE)
    eall = (jnp.arange(n_slc)[:, None] == (jnp.arange(seq) // L_SEL)[None, :]).astype(BF16)
    a_out = _nsa_prompt(qun, qrot, sm, kcc, vcc, att, wsel, eall)
    y_prompt = _output_block(xp, m_out, a_out, att_flat, ow, tm).reshape(bsz, seq, D_MODEL)
    kv_prompt = kv4.reshape(bsz, seq, 4, N_KV, HEAD_DIM)
    win_prompt = win.reshape(bsz, min(WINDOW, seq), 2, N_KV, HEAD_DIM)
    ssm_prompt = ssm_p.reshape(bsz, M_HEADS, HEAD_DIM, D_STATE)
    conv_prompt = conv_p.reshape(bsz, CONV_W - 1, CONV_DIM)

    xs_ = x_sample.reshape(db, D_MODEL)
    pos_s = jnp.full((db,), past_len, jnp.int32)
    z, xbc, qun, qrot, kv4_s, win_s, _, sm = _project(xs_, pw_flat, _rope_tables(pos_s), db, db)
    m_out_s, conv_s, ssm_s = _mamba_sample(
        z, xbc, sm, jnp.transpose(state_conv, (1, 0, 2)), state_ssm.reshape(db, SSM_WIDTH, D_STATE),
        conv_w, cb, dtb, alog, dsk, ng)
    cache4 = jnp.transpose(cache_kv, (0, 2, 3, 4, 1)).reshape(cache_kv.shape[0], 4, N_KV * HEAD_DIM, PAGE_SIZE)
    state4 = jnp.transpose(state_win, (0, 2, 3, 4, 1)).reshape(db, 2 * N_KV, HEAD_DIM, wb)
    n_slc_s = -(-(past_len + dseq) // L_SEL)
    nbp = -(-n_slc_s // LANES) * LANES
    wsel_s = jnp.transpose(_select_weights(nbp, past_len // STRIDE))
    qun8 = qun.reshape(db, N_HEADS, HEAD_DIM)
    qrot8 = qrot.reshape(db, N_HEADS, HEAD_DIM)
    o_cmp, scores = _sample_cmp(page_table, cache4, kv4_s, qun8, cw, wsel_s, past_len)
    n_top = min(N_SEL, n_slc_s)
    idx = _sample_topk(scores[:, 0:N_KV, :].reshape(db * N_KV, nbp), n_slc_s, past_len)[:, 0:n_top]
    o_sel = _sample_sel(page_table, idx, cache4, qrot8, kv4_s, past_len, n_top)
    gates8 = jnp.pad(sm[:, M_HEADS:M_HEADS + 3 * N_HEADS].reshape(db, N_HEADS, 3), ((0, 0), (0, 0), (0, LANES - 3)))
    a8, win4 = _sample_win(state4, win_s, qrot8, gates8, o_cmp, o_sel, min(8, db))
    y_sample = _output_block(xs_, m_out_s, a8.reshape(db, ATT_WIDTH), att_flat, ow, db).reshape(db, dseq, D_MODEL)
    kv_sample = kv4_s.reshape(db, dseq, 4, N_KV, HEAD_DIM)
    win_sample = jnp.transpose(win4.reshape(db, 2, N_KV, HEAD_DIM, wb), (0, 4, 1, 2, 3))
    ssm_sample = ssm_s.reshape(db, M_HEADS, HEAD_DIM, D_STATE)
    conv_sample = jnp.transpose(conv_s, (1, 0, 2))

    return (y_prompt, y_sample, kv_prompt, win_prompt, ssm_prompt, conv_prompt,
            kv_sample, win_sample, ssm_sample, conv_sample)
```

```python
import functools
import math

import jax
import jax.numpy as jnp
from jax import lax
from jax.experimental import pallas as pl
from jax.experimental.pallas import tpu as pltpu

F32 = jnp.float32
BF16 = jnp.bfloat16

D_MODEL = 1024
HEAD_DIM = 64
SSM_WIDTH = 512
ATT_WIDTH = 512
M_HEADS = 8
M_GROUPS = 2
D_STATE = 128
CONV_W = 4
CONV_DIM = SSM_WIDTH + 2 * M_GROUPS * D_STATE
SSD_CHUNK = 256
RMS_EPS = 1e-6
N_HEADS = 8
N_KV = 2
GQA = N_HEADS // N_KV
ROT_DIM = 16
ROPE_THETA = 500000.0
L_CMP = 32
STRIDE = 16
CMP_HIDDEN = 128
L_SEL = 64
N_SEL = 16
N_LOCAL = 2
WINDOW = 512
Q_BLOCK = 128
ATT_SCALE = HEAD_DIM ** -0.5
D_FF = 4 * D_MODEL
LN_EPS = 1e-5
DEEPNORM_ALPHA = 2.0 ** 0.25
PAGE_SIZE = 128
LANES = 128
NEG = -1e30
SEL_KEY_TILE = 1024
SOFTMAX_ROW_GROUP = 16
VMEM_LIMIT = 56 * 1024 * 1024


def _cparams(sem):
    return pltpu.CompilerParams(dimension_semantics=sem, vmem_limit_bytes=VMEM_LIMIT)


def _dot(a, b):
    return jnp.dot(a, b, preferred_element_type=F32)


def _dot_nt(a, b):
    return lax.dot_general(a, b, (((1,), (1,)), ((), ())), preferred_element_type=F32)


def _split3(x):
    hi = x.astype(BF16)
    r = x - hi.astype(F32)
    mid = r.astype(BF16)
    lo = (r - mid.astype(F32)).astype(BF16)
    return hi, mid, lo


def _dot3(x, w):
    hi, mid, lo = _split3(x)
    return _dot(hi, w) + _dot(mid, w) + _dot(lo, w)


def _sigmoid(x):
    return 1.0 / (1.0 + jnp.exp(-x))


def _silu(x):
    return x * _sigmoid(x)


def _softplus(x):
    return jnp.maximum(x, 0.0) + jnp.log1p(jnp.exp(-jnp.abs(x)))


def _layer_norm(x, g, b):
    mu = jnp.mean(x, -1, keepdims=True)
    xc = x - mu
    var = jnp.mean(xc * xc, -1, keepdims=True)
    return xc * lax.rsqrt(var + LN_EPS) * g + b


def _rope128(x, c, s1, s2):
    return x * c + pltpu.roll(x, 8, 1) * s1 + pltpu.roll(x, LANES - 8, 1) * s2


def _full(shape):
    nd = len(shape)
    return pl.BlockSpec(shape, lambda *a: (0,) * nd)


def _resident(shape):
    nd = len(shape)
    return pl.BlockSpec(shape, lambda *a: (0,) * nd, pipeline_mode=pl.Buffered(1))


def _proj_kernel(x_ref, wz_ref, wxbc_ref, wq_ref, wkv_ref, wsm_ref, c_ref, s1_ref, s2_ref,
                 z_ref, xbc_ref, qun_ref, qrot_ref, kv4_ref, win_ref, att_ref, sm_ref):
    x = x_ref[...].astype(BF16)
    c, s1, s2 = c_ref[...], s1_ref[...], s2_ref[...]
    z_ref[...] = _dot(x, wz_ref[...])
    xbc_ref[...] = _dot(x, wxbc_ref[...])
    q = _dot(x, wq_ref[...]) * ATT_SCALE
    qun_ref[...] = q.astype(BF16)
    for h in range(q.shape[1] // LANES):
        sl = slice(h * LANES, (h + 1) * LANES)
        qrot_ref[:, sl] = _rope128(q[:, sl], c, s1, s2).astype(BF16)
    kv = _dot(x, wkv_ref[...])
    ks = _rope128(kv[:, 256:384], c, s1, s2)
    kw = _rope128(kv[:, 512:640], c, s1, s2)
    kv4_ref[:, 0:256] = kv[:, 0:256]
    kv4_ref[:, 256:384] = ks
    kv4_ref[:, 384:512] = kv[:, 384:512]
    win_ref[:, 0:128] = kw
    win_ref[:, 128:256] = kv[:, 640:768]
    att_ref[:, 0:128] = ks.astype(BF16)
    att_ref[:, 128:256] = kv[:, 384:512].astype(BF16)
    att_ref[:, 256:384] = kw.astype(BF16)
    att_ref[:, 384:512] = kv[:, 640:768].astype(BF16)
    sm = _dot(x, wsm_ref[...])
    lane = lax.broadcasted_iota(jnp.int32, sm.shape, 1)
    is_gate = (lane >= M_HEADS) & (lane < M_HEADS + 3 * N_HEADS)
    sm_ref[...] = jnp.where(is_gate, _sigmoid(sm), sm)


def _project(x, weights, tables, tm, win_rows):
    t = x.shape[0]
    nt = t // tm
    nwb = win_rows // tm
    wz, wxbc, wq, wkv, wsm = weights
    qw = wq.shape[1]
    row = lambda w: pl.BlockSpec((tm, w), lambda i: (i, 0))
    out_shape = (
        jax.ShapeDtypeStruct((t, 512), F32),
        jax.ShapeDtypeStruct((t, 1024), F32),
        jax.ShapeDtypeStruct((t, qw), BF16),
        jax.ShapeDtypeStruct((t, qw), BF16),
        jax.ShapeDtypeStruct((t, 512), F32),
        jax.ShapeDtypeStruct((win_rows, 256), F32),
        jax.ShapeDtypeStruct((t, 512), BF16),
        jax.ShapeDtypeStruct((t, 128), F32),
    )
    out_specs = (row(512), row(1024), row(qw), row(qw), row(512),
                 pl.BlockSpec((tm, 256), lambda i: (jnp.maximum(i - (nt - nwb), 0), 0)),
                 row(512), row(128))
    in_specs = [row(D_MODEL), _full(wz.shape), _full(wxbc.shape), _full(wq.shape), _full(wkv.shape),
                _full(wsm.shape), row(128), row(128), row(128)]
    return pl.pallas_call(
        _proj_kernel, grid=(nt,), in_specs=in_specs, out_specs=out_specs, out_shape=out_shape,
        compiler_params=_cparams(("arbitrary",)), name="proj",
    )(x, wz, wxbc, wq, wkv, wsm, *tables)


def _mamba_prompt_kernel(z_ref, xbc_ref, sm_ref, cw_ref, cb_ref, dtb_ref, alog_ref, dsk_ref, ng_ref,
                         y_ref, conv_ref, ssm_ref, ubuf, state):
    c = pl.program_id(0)
    nc = pl.num_programs(0)
    L = SSD_CHUNK

    @pl.when(c == 0)
    def _():
        ubuf[0:8, :] = jnp.zeros((8, CONV_DIM), F32)
        state[...] = jnp.zeros(state.shape, F32)

    u = xbc_ref[...]
    ubuf[8:8 + L, :] = u
    acc = u * cw_ref[3:4, :] + cb_ref[...]
    for k in range(CONV_W - 1):
        acc = acc + ubuf[pl.ds(8 - (CONV_W - 1) + k, L), :] * cw_ref[k:k + 1, :]
    tail = ubuf[L:L + 8, :]
    ubuf[0:8, :] = tail
    xa = _silu(acc)
    xs = xa[:, 0:SSM_WIDTH]
    bm = xa[:, SSM_WIDTH:SSM_WIDTH + 256]
    cm = xa[:, SSM_WIDTH + 256:SSM_WIDTH + 512]

    lane = lax.broadcasted_iota(jnp.int32, (L, LANES), 1)
    head_lane = lane < M_HEADS
    dt = jnp.where(head_lane, _softplus(sm_ref[...] + dtb_ref[...]), 0.0)
    a_row = jnp.where(head_lane[0:1], -jnp.exp(alog_ref[...]), 0.0)
    da = dt * a_row
    ri = lax.broadcasted_iota(jnp.int32, (L, L), 0)
    ci = lax.broadcasted_iota(jnp.int32, (L, L), 1)
    causal = ri >= ci
    tril = jnp.where(causal, 1.0, 0.0).astype(BF16)
    da_hi, da_mid, da_lo = _split3(da)
    cum = _dot(tril, da_hi) + _dot(tril, da_mid) + _dot(tril, da_lo)
    cum_t = cum.T
    er = lax.broadcasted_iota(jnp.int32, (LANES, SSM_WIDTH), 0)
    ec = lax.broadcasted_iota(jnp.int32, (LANES, SSM_WIDTH), 1)
    expand = jnp.where(er == (ec >> 6), 1.0, 0.0).astype(BF16)
    dtx = _dot3(dt, expand)
    cumx = _dot3(cum, expand)
    cumx_last = cumx[L - 1:L, :]
    xc = xs * dtx
    xw = xc * jnp.exp(cumx_last - cumx)
    ecx = jnp.exp(cumx)
    dec_last = jnp.exp(cumx_last)

    half = lax.broadcasted_iota(jnp.int32, (L, LANES), 1) < HEAD_DIM
    y_parts = []
    for g in range(M_GROUPS):
        bg = bm[:, g * D_STATE:(g + 1) * D_STATE].astype(BF16)
        cg = cm[:, g * D_STATE:(g + 1) * D_STATE].astype(BF16)
        gs = slice(g * 256, (g + 1) * 256)
        scores = _dot_nt(cg, bg)
        h_prev = state[:, gs]
        y_off = _dot(cg, h_prev.astype(BF16)) * ecx[:, gs]
        st = _dot(bg.T, xw[:, gs].astype(BF16))
        state[:, gs] = h_prev * dec_last[:, gs] + st
        for pr in range(2):
            ha = g * 4 + pr * 2
            xcp = xc[:, ha * HEAD_DIM:(ha + 2) * HEAD_DIM].astype(BF16)
            outs = []
            for hh in (ha, ha + 1):
                seg = cum[:, hh:hh + 1] - cum_t[hh:hh + 1, :]
                dec = jnp.exp(jnp.where(causal, seg, -jnp.inf))
                outs.append(_dot((scores * dec).astype(BF16), xcp))
            y_parts.append(jnp.where(half, outs[0], outs[1]) + y_off[:, pr * 128:(pr + 1) * 128])
    y = jnp.concatenate(y_parts, axis=1) + dsk_ref[...] * xs
    y = y * _silu(z_ref[...])
    outs = []
    for g in range(M_GROUPS):
        yg = y[:, g * 256:(g + 1) * 256]
        outs.append(yg * lax.rsqrt(jnp.mean(yg * yg, -1, keepdims=True) + RMS_EPS))
    y_ref[...] = (jnp.concatenate(outs, axis=1) * ng_ref[...]).astype(BF16)

    @pl.when(c == nc - 1)
    def _():
        conv_ref[...] = ubuf[pl.ds(8 - (CONV_W - 1), CONV_W - 1), :]
        ssm_ref[...] = state[...].T


def _mamba_prompt(z, xbc, sm, cw, cb, dtb, alog, dsk, ng):
    t = z.shape[0]
    L = SSD_CHUNK
    row = lambda w: pl.BlockSpec((L, w), lambda i: (i, 0))
    return pl.pallas_call(
        _mamba_prompt_kernel, grid=(t // L,),
        in_specs=[row(512), row(1024), row(128), _full(cw.shape), _full(cb.shape), _full(dtb.shape),
                  _full(alog.shape), _full(dsk.shape), _full(ng.shape)],
        out_specs=(row(512), _full((CONV_W - 1, CONV_DIM)), _full((SSM_WIDTH, D_STATE))),
        out_shape=(jax.ShapeDtypeStruct((t, 512), BF16),
                   jax.ShapeDtypeStruct((CONV_W - 1, CONV_DIM), F32),
                   jax.ShapeDtypeStruct((SSM_WIDTH, D_STATE), F32)),
        scratch_shapes=[pltpu.VMEM((L + 8, CONV_DIM), F32), pltpu.VMEM((D_STATE, SSM_WIDTH), F32)],
        compiler_params=_cparams(("arbitrary",)), name="mamba_prompt",
    )(z, xbc, sm, cw, cb, dtb, alog, dsk, ng)


def _mamba_sample_kernel(z_ref, xbc_ref, sm_ref, sc_ref, h0_ref, cw_ref, cb_ref, dtb_ref, alog_ref, dsk_ref,
                         ng_ref, y_ref, conv_ref, h_ref, xct, dect, brow, crow, xcrow, decrow, xsrow, yrow):
    b = pl.program_id(0)
    nb = pl.num_programs(0)
    db = z_ref.shape[0]

    @pl.when(b == 0)
    def _():
        u = xbc_ref[...]
        acc = u * cw_ref[3:4, :] + cb_ref[...]
        for k in range(CONV_W - 1):
            acc = acc + sc_ref[k] * cw_ref[k:k + 1, :]
        conv_ref[0] = sc_ref[1]
        conv_ref[1] = sc_ref[2]
        conv_ref[2] = u
        xa = _silu(acc)
        xs = xa[:, 0:SSM_WIDTH]
        lane = lax.broadcasted_iota(jnp.int32, (db, LANES), 1)
        head_lane = lane < M_HEADS
        dt = jnp.where(head_lane, _softplus(sm_ref[...] + dtb_ref[...]), 0.0)
        da = dt * jnp.where(head_lane[0:1], -jnp.exp(alog_ref[...]), 0.0)
        er = lax.broadcasted_iota(jnp.int32, (LANES, SSM_WIDTH), 0)
        ec = lax.broadcasted_iota(jnp.int32, (LANES, SSM_WIDTH), 1)
        expand = jnp.where(er == (ec >> 6), 1.0, 0.0).astype(BF16)
        xc = xs * _dot3(dt, expand)
        dec = jnp.exp(_dot3(da, expand))
        xct[...] = xc.T
        dect[...] = dec.T
        xcrow[...] = xc
        decrow[...] = dec
        xsrow[...] = xs
        brow[...] = xa[:, SSM_WIDTH:SSM_WIDTH + 256]
        crow[...] = xa[:, SSM_WIDTH + 256:SSM_WIDTH + 512]

    r = lax.broadcasted_iota(jnp.int32, (db, LANES), 0)
    pick = jnp.where(r == b, 1.0, 0.0).astype(BF16)
    colx = _dot3(xct[...], pick)
    cold = _dot3(dect[...], pick)
    bv = brow[pl.ds(b, 1), :]
    cv = crow[pl.ds(b, 1), :]
    h0 = h0_ref[0]
    ys = []
    for g in range(M_GROUPS):
        rs = slice(g * 256, (g + 1) * 256)
        bg = bv[:, g * D_STATE:(g + 1) * D_STATE]
        cg = cv[:, g * D_STATE:(g + 1) * D_STATE]
        h_ref[0, rs, :] = cold[rs] * h0[rs] + colx[rs] * bg
        cb_s = jnp.sum(cg * bg, -1, keepdims=True)
        c8 = jnp.broadcast_to(cg, (8, D_STATE)).astype(BF16)
        y_off = _dot_nt(c8, h0[rs].astype(BF16))[0:1]
        ys.append(y_off * decrow[pl.ds(b, 1), rs] + cb_s * xcrow[pl.ds(b, 1), rs])
    yrow[pl.ds(b, 1), :] = jnp.concatenate(ys, axis=1)

    @pl.when(b == nb - 1)
    def _():
        y = (yrow[...] + dsk_ref[...] * xsrow[...]) * _silu(z_ref[...])
        outs = []
        for g in range(M_GROUPS):
            yg = y[:, g * 256:(g + 1) * 256]
            outs.append(yg * lax.rsqrt(jnp.mean(yg * yg, -1, keepdims=True) + RMS_EPS))
        y_ref[...] = (jnp.concatenate(outs, axis=1) * ng_ref[...]).astype(BF16)


def _mamba_sample(z, xbc, sm, sc_t, h0, cw, cb, dtb, alog, dsk, ng):
    db = z.shape[0]
    vm = lambda s: pltpu.VMEM(s, F32)
    return pl.pallas_call(
        _mamba_sample_kernel, grid=(db,),
        in_specs=[_full(z.shape), _full(xbc.shape), _full(sm.shape), _full(sc_t.shape),
                  pl.BlockSpec((1, SSM_WIDTH, D_STATE), lambda b: (b, 0, 0)),
                  _full(cw.shape), _full(cb.shape), _full(dtb.shape), _full(alog.shape), _full(dsk.shape),
                  _full(ng.shape)],
        out_specs=(_full((db, 512)), _full((CONV_W - 1, db, CONV_DIM)),
                   pl.BlockSpec((1, SSM_WIDTH, D_STATE), lambda b: (b, 0, 0))),
        out_shape=(jax.ShapeDtypeStruct((db, 512), BF16),
                   jax.ShapeDtypeStruct((CONV_W - 1, db, CONV_DIM), F32),
                   jax.ShapeDtypeStruct((db, SSM_WIDTH, D_STATE), F32)),
        scratch_shapes=[vm((SSM_WIDTH, db)), vm((SSM_WIDTH, db)), vm((db, 256)), vm((db, 256)),
                        vm((db, SSM_WIDTH)), vm((db, SSM_WIDTH)), vm((db, SSM_WIDTH)), vm((db, SSM_WIDTH))],
        compiler_params=_cparams(("arbitrary",)), name="mamba_sample",
    )(z, xbc, sm, sc_t, h0, cw, cb, dtb, alog, dsk, ng)


def _stride_block_proj(x_ref, n_blocks, w_ref):
    acc = jnp.zeros((n_blocks, 512), F32)
    for j in range(0, STRIDE, 2):
        xj = jnp.concatenate([x_ref[pl.ds(j + d, n_blocks, stride=STRIDE), :].astype(BF16) for d in range(2)], axis=1)
        acc = acc + _dot(xj, w_ref[pl.ds(j * LANES, 2 * LANES), :])
    return acc


def _cmp_proj_kernel(xk_ref, xv_ref, wk_ref, wv_ref, hk_ref, hv_ref):
    n_blocks = xk_ref.shape[0] // STRIDE
    hk_ref[...] = _stride_block_proj(xk_ref, n_blocks, wk_ref)
    hv_ref[...] = _stride_block_proj(xv_ref, n_blocks, wv_ref)


def _cmp_const(pe_ref, w1f_ref, b1_ref):
    c = _dot(pe_ref[...], w1f_ref[...])[0:1]
    return jnp.concatenate([c, c], axis=1) + b1_ref[...]


def _cmp_mlp(h0, h1_next, const, w2_ref, b2_ref):
    pre = h0 + h1_next + const
    return _dot(_silu(pre).astype(BF16), w2_ref[...]) + b2_ref[...]


def _cmp_finish_kernel(hk_ref, hv_ref, pek, w1k, b1k, w2k, b2k, pev, w1v, b1v, w2v, b2v, kcc_ref, vcc_ref):
    n = hk_ref.shape[0]
    for h_ref, pe, w1f, b1, w2, b2, o_ref in ((hk_ref, pek, w1k, b1k, w2k, b2k, kcc_ref),
                                              (hv_ref, pev, w1v, b1v, w2v, b2v, vcc_ref)):
        h = h_ref[...]
        h1_next = pltpu.roll(h[:, 256:512], n - 1, 0)
        o_ref[...] = _cmp_mlp(h[:, 0:256], h1_next, _cmp_const(pe, w1f, b1), w2, b2).astype(BF16)


def _compress_prompt(kv4, cw):
    t = kv4.shape[0]
    rows = min(t, 2048)
    n_blocks = t // STRIDE
    hk, hv = pl.pallas_call(
        _cmp_proj_kernel, grid=(t // rows,),
        in_specs=[pl.BlockSpec((rows, LANES), lambda i: (i, 0)), pl.BlockSpec((rows, LANES), lambda i: (i, 1)),
                  _full(cw["wk"].shape), _full(cw["wv"].shape)],
        out_specs=(pl.BlockSpec((rows // STRIDE, 512), lambda i: (i, 0)),) * 2,
        out_shape=(jax.ShapeDtypeStruct((n_blocks, 512), F32),) * 2,
        compiler_params=_cparams(("arbitrary",)), name="cmp_proj",
    )(kv4, kv4, cw["wk"], cw["wv"])
    names = ("pek", "w1k", "b1k", "w2k", "b2k", "pev", "w1v", "b1v", "w2v", "b2v")
    args = [cw[k] for k in names]
    return pl.pallas_call(
        _cmp_finish_kernel, grid=(1,),
        in_specs=[_full(hk.shape), _full(hv.shape)] + [_full(a.shape) for a in args],
        out_specs=(_full((n_blocks, 128)),) * 2,
        out_shape=(jax.ShapeDtypeStruct((n_blocks, 128), BF16),) * 2,
        compiler_params=_cparams(("arbitrary",)), name="cmp_finish",
    )(hk, hv, *args)


def _softmax_rows(s):
    m = jnp.max(s, -1, keepdims=True)
    m = jnp.where(jnp.isfinite(m), m, 0.0)
    e = jnp.exp(s - m)
    return e / jnp.maximum(jnp.sum(e, -1, keepdims=True), 1e-30)


def _knock_out_top(sc, blkf, n_pick):
    for _ in range(n_pick):
        m = jnp.max(sc, 0, keepdims=True)
        idx = jnp.min(jnp.where(sc == m, blkf, 1e9), 0, keepdims=True)
        sc = jnp.where(blkf == idx, -jnp.inf, sc)
    return sc


def _nsa_prompt_kernel(qun_ref, qrot_ref, sm_ref, kcc_ref, vcc_ref, att_ref, wsel_ref, eall_ref, o_ref,
                       m_scr, l_scr, acc_scr, sel_scr, bias_scr, p_scr, imp_scr):
    i = pl.program_id(0)
    s0 = i * Q_BLOCK
    ncp = kcc_ref.shape[0]
    nb = wsel_ref.shape[0]
    kt = min(SEL_KEY_TILE, att_ref.shape[0])
    n_top = min(N_SEL, nb)

    def head(ref, h):
        return ref[:, h * LANES:(h + 1) * LANES]

    def stacked(ref, h0, n):
        return jnp.concatenate([head(ref, h0 + j) for j in range(n)], axis=0)

    def rows(x, j):
        return x[j * Q_BLOCK:(j + 1) * Q_BLOCK]

    colc = lax.broadcasted_iota(jnp.int32, (Q_BLOCK, ncp), 1)
    rowq = lax.broadcasted_iota(jnp.int32, (Q_BLOCK, ncp), 0) + s0
    vis = colc * STRIDE + (L_CMP - 1) <= rowq
    rg = SOFTMAX_ROW_GROUP
    bias_scr[:, 0:ncp] = jnp.where(vis, 0.0, -jnp.inf)
    s_all = _dot_nt(stacked(qun_ref, 0, N_HEADS), kcc_ref[...])
    for kv in range(N_KV):
        for r in range(Q_BLOCK // rg):
            bias = bias_scr[r * rg:(r + 1) * rg, 0:ncp]
            imp = None
            for g in range(GQA):
                r0 = (kv * GQA + g) * Q_BLOCK + r * rg
                p = _softmax_rows(s_all[r0:r0 + rg] + bias)
                p_scr[r0:r0 + rg, 0:ncp] = p.astype(BF16)
                imp = p if imp is None else imp + p
            imp_scr[kv, r * rg:(r + 1) * rg, :] = imp
    o_all = _dot(p_scr[:, 0:ncp], vcc_ref[...])
    o_cmp = [rows(o_all, h) for h in range(N_HEADS)]

    blk = lax.broadcasted_iota(jnp.int32, (nb, Q_BLOCK), 0)
    qpl = lax.broadcasted_iota(jnp.int32, (nb, Q_BLOCK), 1) + s0
    cur = qpl >> 6
    valid = blk * L_SEL <= qpl
    forced = (blk == 0) | ((blk <= cur) & (blk > cur - N_LOCAL))
    blkf = blk.astype(F32)
    wsel = wsel_ref[...]

    o_sel, o_win = [], []
    rowk = lax.broadcasted_iota(jnp.int32, (Q_BLOCK, kt), 0) + s0
    lanek = lax.broadcasted_iota(jnp.int32, (Q_BLOCK, kt), 1)
    n_tiles = (s0 + Q_BLOCK - 1) // kt + 1
    nch = kt // LANES
    bgrp = min(LANES, nb)

    def lane_chunks(x):
        return [x[:, c * LANES:(c + 1) * LANES] for c in range(nch)]

    for kv in range(N_KV):
        hi, mid, lo = _split3(imp_scr[kv])
        sc = _dot_nt(wsel, hi) + _dot_nt(wsel, mid) + _dot_nt(wsel, lo)
        sc = _knock_out_top(jnp.where(forced, -jnp.inf, jnp.where(valid, sc, -jnp.inf)), blkf, n_top - 1 - N_LOCAL)
        sel_scr[...] = jnp.where(valid & (sc == -jnp.inf), 0.0, NEG).T.astype(BF16)
        for g in range(GQA):
            m_scr[g] = jnp.full((Q_BLOCK, LANES), NEG, F32)
            l_scr[g] = jnp.zeros((Q_BLOCK, LANES), F32)
            acc_scr[g] = jnp.zeros((Q_BLOCK, LANES), F32)
        q4 = stacked(qrot_ref, kv * GQA, GQA)

        def tile_body(t, carry, diagonal):
            k0 = pl.multiple_of(t * kt, kt)
            ks = att_ref[pl.ds(k0, kt), 0:128]
            vs = att_ref[pl.ds(k0, kt), 128:256]
            b0 = pl.multiple_of(((t * (kt // L_SEL)) // bgrp) * bgrp, bgrp)
            bias = _dot(sel_scr[:, pl.ds(b0, bgrp)], eall_ref[pl.ds(b0, bgrp), pl.ds(k0, kt)])
            if diagonal:
                bias = jnp.where(lanek + k0 <= rowk, bias, NEG)
            s4 = _dot_nt(q4, ks)
            ps, alphas = [], []
            for g in range(GQA):
                s = rows(s4, g) + bias
                m_old = m_scr[g]
                m_new = jnp.maximum(m_old, jnp.max(functools.reduce(jnp.maximum, lane_chunks(s)), -1, keepdims=True))
                alpha = jnp.exp(m_old - m_new)
                p = jnp.exp(s - jnp.tile(m_new, (1, nch)))
                l_scr[g] = alpha * l_scr[g] + functools.reduce(jnp.add, lane_chunks(p))
                m_scr[g] = m_new
                ps.append(p.astype(BF16))
                alphas.append(alpha)
            pv = _dot(jnp.concatenate(ps, axis=0), vs)
            for g in range(GQA):
                acc_scr[g] = alphas[g] * acc_scr[g] + rows(pv, g)
            return carry

        lax.fori_loop(0, n_tiles - 1, functools.partial(tile_body, diagonal=False), 0)
        tile_body(n_tiles - 1, 0, diagonal=True)
        for g in range(GQA):
            o_sel.append(acc_scr[g] / jnp.sum(l_scr[g], -1, keepdims=True))

    wk = WINDOW + Q_BLOCK
    w0 = pl.multiple_of(jnp.maximum(s0 - WINDOW, 0), Q_BLOCK)
    kw = att_ref[pl.ds(w0, wk), 256:384]
    vw = att_ref[pl.ds(w0, wk), 384:512]
    kposw = lax.broadcasted_iota(jnp.int32, (Q_BLOCK, wk), 1) + w0
    roww = lax.broadcasted_iota(jnp.int32, (Q_BLOCK, wk), 0) + s0
    okw = (kposw <= roww) & (kposw > roww - WINDOW)
    bias_scr[:, 0:wk] = jnp.where(okw, 0.0, -jnp.inf)
    s_all = _dot_nt(stacked(qrot_ref, 0, N_HEADS), kw)
    for h in range(N_HEADS):
        for r in range(Q_BLOCK // rg):
            r0 = h * Q_BLOCK + r * rg
            p = _softmax_rows(s_all[r0:r0 + rg] + bias_scr[r * rg:(r + 1) * rg, 0:wk])
            p_scr[r0:r0 + rg, 0:wk] = p.astype(BF16)
    o_all = _dot(p_scr[:, 0:wk], vw)
    o_win = [rows(o_all, h) for h in range(N_HEADS)]

    sm = sm_ref[...]
    lane = lax.broadcasted_iota(jnp.int32, (Q_BLOCK, LANES), 1)
    chunks = []
    for c in range(N_HEADS // 2):
        parts = []
        for h in (2 * c, 2 * c + 1):
            kv = h // GQA
            gc = M_HEADS + 3 * h
            a = sm[:, gc:gc + 1] * o_cmp[h] + sm[:, gc + 1:gc + 2] * o_sel[h] + sm[:, gc + 2:gc + 3] * o_win[h]
            a = jnp.where((lane >= kv * HEAD_DIM) & (lane < (kv + 1) * HEAD_DIM), a, 0.0)
            if kv != h % 2:
                a = pltpu.roll(a, HEAD_DIM, 1)
            parts.append(a)
        chunks.append(parts[0] + parts[1])
    o_ref[...] = jnp.concatenate(chunks, axis=1)


def _nsa_prompt(qun, qrot, sm, kcc, vcc, att, wsel, eall):
    t = qun.shape[0]
    ncp = kcc.shape[0]
    width = max(ncp, WINDOW + Q_BLOCK)
    row = lambda w: pl.BlockSpec((Q_BLOCK, w), lambda i: (i, 0))
    return pl.pallas_call(
        _nsa_prompt_kernel, grid=(t // Q_BLOCK,),
        in_specs=[row(1024), row(1024), row(128), _resident(kcc.shape), _resident(vcc.shape),
                  _resident(att.shape), _resident(wsel.shape), _resident(eall.shape)],
        out_specs=row(512),
        out_shape=jax.ShapeDtypeStruct((t, 512), F32),
        scratch_shapes=[pltpu.VMEM((GQA, Q_BLOCK, LANES), F32)] * 3 + [
            pltpu.VMEM((Q_BLOCK, wsel.shape[0]), BF16),
            pltpu.VMEM((Q_BLOCK, width), F32),
            pltpu.VMEM((N_HEADS * Q_BLOCK, width), BF16),
            pltpu.VMEM((N_KV, Q_BLOCK, ncp), F32)],
        compiler_params=_cparams(("arbitrary",)), name="nsa_prompt",
    )(qun, qrot, sm, kcc, vcc, att, wsel, eall)


def _out_kernel(x_ref, m_ref, a_ref, ang_ref, wom_ref, woa_ref, l1g_ref, l1b_ref, w1_ref, w2_ref,
                l2g_ref, l2b_ref, y_ref):
    a = a_ref[...]
    an = a * lax.rsqrt(jnp.mean(a * a, -1, keepdims=True) + RMS_EPS) * ang_ref[...]
    mix = _dot(m_ref[...], wom_ref[...]) + _dot(an.astype(BF16), woa_ref[...])
    h = _layer_norm(DEEPNORM_ALPHA * x_ref[...] + mix, l1g_ref[...], l1b_ref[...])
    hb = h.astype(BF16)
    f = jnp.zeros(h.shape, F32)
    for c in range(D_FF // 1024):
        cs = slice(c * 1024, (c + 1) * 1024)
        u = jnp.maximum(_dot(hb, w1_ref[:, cs]), 0.0)
        f = f + _dot((u * u).astype(BF16), w2_ref[cs, :])
    y_ref[...] = _layer_norm(DEEPNORM_ALPHA * h + f, l2g_ref[...], l2b_ref[...])


def _output_block(x, m_out, a_out, att, ow, tm):
    t = x.shape[0]
    row = lambda w: pl.BlockSpec((tm, w), lambda i: (i, 0))
    names = ("l1g", "l1b", "w1", "w2", "l2g", "l2b")
    args = [att["ang"], ow["wom"], att["woa"]] + [ow[k] for k in names]
    return pl.pallas_call(
        _out_kernel, grid=(t // tm,),
        in_specs=[row(D_MODEL), row(512), row(a_out.shape[1])] + [_resident(a.shape) for a in args],
        out_specs=row(D_MODEL),
        out_shape=jax.ShapeDtypeStruct((t, D_MODEL), F32),
        compiler_params=_cparams(("arbitrary",)), name="out_block",
    )(x, m_out, a_out, *args)


def _sample_cmp_kernel(pt_ref, cache_ref, new_ref, q_ref, wk_ref, wv_ref,
                       pek, w1k, b1k, w2k, b2k, pev, w1v, b1v, w2v, b2v, wsel_ref,
                       ocmp_ref, score_ref, buf, xs, sem, *, n_pages, past_len):
    b = pl.program_id(0)
    nb = pl.num_programs(0)
    n_blocks = n_pages * PAGE_SIZE // STRIDE

    def page_copy(bb, slot, pg):
        return pltpu.make_async_copy(cache_ref.at[pt_ref[bb * n_pages + pg], pl.ds(0, 2)],
                                     buf.at[slot, pg], sem.at[slot])

    def start_all(bb, slot):
        def body(pg, carry):
            page_copy(bb, slot, pg).start()
            return carry
        lax.fori_loop(0, n_pages, body, 0)

    def wait_all(bb, slot):
        def body(pg, carry):
            page_copy(bb, slot, pg).wait()
            return carry
        lax.fori_loop(0, n_pages, body, 0)

    slot = b % 2

    @pl.when(b == 0)
    def _():
        start_all(0, 0)

    @pl.when(b + 1 < nb)
    def _():
        start_all(b + 1, 1 - slot)

    wait_all(b, slot)

    def to_rows(pg, carry):
        r0 = pl.multiple_of(pg * PAGE_SIZE, PAGE_SIZE)
        for c in range(2):
            xs[c, pl.ds(r0, PAGE_SIZE), :] = buf[slot, pg, c].T
        return carry
    lax.fori_loop(0, n_pages, to_rows, 0, unroll=math.gcd(n_pages, 8))

    new = jnp.broadcast_to(new_ref[0], (8, 512))
    rowi = lax.broadcasted_iota(jnp.int32, (n_blocks, 256), 0)
    toks = []
    for c, w_ref, pe, w1f, b1, w2, b2 in ((0, wk_ref, pek, w1k, b1k, w2k, b2k),
                                          (1, wv_ref, pev, w1v, b1v, w2v, b2v)):
        h = _stride_block_proj(xs.at[c], n_blocks, w_ref)
        h_new = _dot(new[:, c * LANES:(c + 1) * LANES].astype(BF16), w_ref[0:LANES, 256:512])[0:1]
        h1_next = jnp.where(rowi == n_blocks - 1, h_new, pltpu.roll(h[:, 256:512], n_blocks - 1, 0))
        toks.append(_cmp_mlp(h[:, 0:256], h1_next, _cmp_const(pe, w1f, b1), w2, b2).astype(BF16))
    kcc, vcc = toks

    q = q_ref[0]
    zq = jnp.zeros_like(q)
    rowh = lax.broadcasted_iota(jnp.int32, (N_HEADS, LANES), 0)
    qpad = jnp.where(rowh < GQA, jnp.concatenate([q, zq], 1), jnp.concatenate([zq, q], 1))
    s = _dot_nt(qpad, kcc)
    colc = lax.broadcasted_iota(jnp.int32, s.shape, 1)
    p = _softmax_rows(jnp.where(colc * STRIDE + (L_CMP - 1) <= past_len, s, -jnp.inf))
    o = _dot(p.astype(BF16), vcc)
    ocmp_ref[0] = jnp.where(rowh[:, 0:HEAD_DIM] < GQA, o[:, 0:HEAD_DIM], o[:, HEAD_DIM:])
    imp0 = p[0:1] + p[1:2] + p[2:3] + p[3:4]
    imp1 = p[4:5] + p[5:6] + p[6:7] + p[7:8]
    rowp = lax.broadcasted_iota(jnp.int32, p.shape, 0)
    imp = jnp.where(rowp == 0, imp0, jnp.where(rowp == 1, imp1, 0.0))
    score_ref[0] = _dot3(imp, wsel_ref[...])


def _sample_cmp(page_table, cache4, kv4_s, q8, cw, wsel_s, past_len):
    db, n_pages = page_table.shape
    names = ("pek", "w1k", "b1k", "w2k", "b2k", "pev", "w1v", "b1v", "w2v", "b2v")
    args = [cw[k] for k in names]
    nbp = wsel_s.shape[1]
    full = lambda a: pl.BlockSpec(a.shape, lambda b, pt: (0,) * a.ndim)
    grid_spec = pltpu.PrefetchScalarGridSpec(
        num_scalar_prefetch=1, grid=(db,),
        in_specs=[pl.BlockSpec(memory_space=pl.ANY),
                  pl.BlockSpec((1, 1, 512), lambda b, pt: (b, 0, 0)),
                  pl.BlockSpec((1, N_HEADS, HEAD_DIM), lambda b, pt: (b, 0, 0)),
                  full(cw["wk"]), full(cw["wv"])] + [full(a) for a in args] + [full(wsel_s)],
        out_specs=(pl.BlockSpec((1, N_HEADS, HEAD_DIM), lambda b, pt: (b, 0, 0)),
                   pl.BlockSpec((1, 8, nbp), lambda b, pt: (b, 0, 0))),
        scratch_shapes=[pltpu.VMEM((2, n_pages, 2, LANES, PAGE_SIZE), F32),
                        pltpu.VMEM((2, n_pages * PAGE_SIZE, LANES), F32),
                        pltpu.SemaphoreType.DMA((2,))],
    )
    return pl.pallas_call(
        functools.partial(_sample_cmp_kernel, n_pages=n_pages, past_len=past_len),
        grid_spec=grid_spec,
        out_shape=(jax.ShapeDtypeStruct((db, N_HEADS, HEAD_DIM), F32), jax.ShapeDtypeStruct((db, 8, nbp), F32)),
        compiler_params=_cparams(("arbitrary",)), name="sample_cmp",
    )(page_table.reshape(-1), cache4, kv4_s.reshape(db, 1, 512), q8, cw["wk"], cw["wv"], *args, wsel_s)


def _sample_topk_kernel(score_ref, idx_ref, *, n_slc, qp):
    sc = score_ref[...]
    blk = lax.broadcasted_iota(jnp.int32, sc.shape, 1)
    cur = qp // L_SEL
    valid = (blk * L_SEL <= qp) & (blk < n_slc)
    forced = (blk == 0) | ((blk <= cur) & (blk > cur - N_LOCAL))
    sc = jnp.where(forced & (blk < n_slc), jnp.inf, jnp.where(valid, sc, -jnp.inf))
    real = blk < n_slc
    blkf = blk.astype(F32)
    taken = jnp.where(real, 0.0, 1.0)
    out = jnp.zeros(idx_ref.shape, F32)
    lane = lax.broadcasted_iota(jnp.int32, idx_ref.shape, 1)
    for it in range(min(N_SEL, n_slc)):
        m = jnp.max(jnp.where(taken == 0.0, sc, -jnp.inf), -1, keepdims=True)
        cand = (sc == m) & (taken == 0.0)
        idx = jnp.min(jnp.where(cand, blkf, 1e9), -1, keepdims=True)
        pick = blkf == idx
        taken = jnp.where(pick, 1.0, taken)
        out = jnp.where(lane == it, idx, out)
    idx_ref[...] = out.astype(jnp.int32)


def _sample_topk(scores, n_slc, qp):
    rows, nbp = scores.shape
    return pl.pallas_call(
        functools.partial(_sample_topk_kernel, n_slc=n_slc, qp=qp), grid=(1,),
        in_specs=[_full(scores.shape)], out_specs=_full((rows, LANES)),
        out_shape=jax.ShapeDtypeStruct((rows, LANES), jnp.int32),
        compiler_params=_cparams(("arbitrary",)), name="sample_topk",
    )(scores)


def _sample_sel_kernel(pt_ref, idx_ref, cache_ref, q_ref, new_ref, osel_ref, buf, sem, *, n_pages, past_len, n_top):
    b = pl.program_id(0)
    nb = pl.num_programs(0)
    per_page = PAGE_SIZE // L_SEL
    n_past_blocks = past_len // L_SEL
    n_cp = N_KV * n_top

    def blk_copy(bb, slot, j):
        kv = j // n_top
        n = j - kv * n_top
        blk = jnp.minimum(idx_ref[bb * n_cp + j], n_past_blocks - 1)
        page = pt_ref[bb * n_pages + blk // per_page]
        return pltpu.make_async_copy(
            cache_ref.at[page, pl.ds(2, 2), pl.ds(pl.multiple_of(kv * HEAD_DIM, HEAD_DIM), HEAD_DIM), :],
            buf.at[slot, kv, :, :, pl.ds(pl.multiple_of(n * PAGE_SIZE, PAGE_SIZE), PAGE_SIZE)], sem.at[slot])

    def start_all(bb, slot):
        def body(j, c):
            blk_copy(bb, slot, j).start()
            return c
        lax.fori_loop(0, n_cp, body, 0)

    def wait_all(bb, slot):
        def body(j, c):
            blk_copy(bb, slot, j).wait()
            return c
        lax.fori_loop(0, n_cp, body, 0)

    slot = b % 2

    @pl.when(b == 0)
    def _():
        start_all(0, 0)

    @pl.when(b + 1 < nb)
    def _():
        start_all(b + 1, 1 - slot)

    wait_all(b, slot)

    q = q_ref[0]
    new = jnp.broadcast_to(new_ref[0], (8, 512))
    nk = n_top * PAGE_SIZE
    lane = lax.broadcasted_iota(jnp.int32, (N_HEADS, nk), 1)
    outs = []
    for kv in range(N_KV):
        ks_new = new[:, 256 + kv * HEAD_DIM:256 + (kv + 1) * HEAD_DIM].astype(BF16)
        vs_new = new[:, 384 + kv * HEAD_DIM:384 + (kv + 1) * HEAD_DIM].astype(BF16).astype(F32)
        s_new = _dot_nt(q, ks_new)[:, 0:1]
        kt = buf[slot, kv, 0].astype(BF16)
        vt = buf[slot, kv, 1].astype(BF16)
        blk_of_lane = jnp.zeros((N_HEADS, nk), jnp.int32)
        for n in range(n_top):
            blk_of_lane = jnp.where((lane >> 7) == n, idx_ref[b * n_cp + kv * n_top + n], blk_of_lane)
        in_block = ((lane & (PAGE_SIZE - 1)) >> 6) == (blk_of_lane & (per_page - 1))
        ok = in_block & (blk_of_lane < n_past_blocks)
        s = jnp.where(ok, _dot(q, kt), -jnp.inf)
        m = jnp.maximum(jnp.max(s, -1, keepdims=True), s_new)
        e = jnp.exp(s - m)
        e_new = jnp.exp(s_new - m)
        denom = jnp.sum(e, -1, keepdims=True) + e_new
        o = _dot_nt((e / denom).astype(BF16), vt) + (e_new / denom).astype(BF16).astype(F32) * vs_new
        outs.append(o)
    row = lax.broadcasted_iota(jnp.int32, (N_HEADS, HEAD_DIM), 0)
    osel_ref[0] = jnp.where(row < GQA, outs[0], outs[1])


def _sample_sel(page_table, idx, cache4, qrot8, kv4_s, past_len, n_top):
    db, n_pages = page_table.shape
    assert PAGE_SIZE // L_SEL == 2
    hb = pl.BlockSpec((1, N_HEADS, HEAD_DIM), lambda b, pt, ix: (b, 0, 0))
    grid_spec = pltpu.PrefetchScalarGridSpec(
        num_scalar_prefetch=2, grid=(db,),
        in_specs=[pl.BlockSpec(memory_space=pl.ANY), hb,
                  pl.BlockSpec((1, 1, 512), lambda b, pt, ix: (b, 0, 0))],
        out_specs=hb,
        scratch_shapes=[pltpu.VMEM((2, N_KV, 2, HEAD_DIM, n_top * PAGE_SIZE), F32), pltpu.SemaphoreType.DMA((2,))],
    )
    return pl.pallas_call(
        functools.partial(_sample_sel_kernel, n_pages=n_pages, past_len=past_len, n_top=n_top),
        grid_spec=grid_spec,
        out_shape=jax.ShapeDtypeStruct((db, N_HEADS, HEAD_DIM), F32),
        compiler_params=_cparams(("arbitrary",)), name="sample_sel",
    )(page_table.reshape(-1), idx.reshape(-1), cache4, qrot8, kv4_s.reshape(db, 1, 512))


def _sample_win_kernel(st_ref, newrow_ref, newall_ref, q_ref, g_ref, ocmp_ref, osel_ref, a_ref, wout_ref):
    i = pl.program_id(0)
    tb, wb = st_ref.shape[0], st_ref.shape[3]
    db = newall_ref.shape[0]
    new_t = newall_ref[...].T
    rowb = lax.broadcasted_iota(jnp.int32, (db, LANES), 0)
    lane_w = lax.broadcasted_iota(jnp.int32, (HEAD_DIM, wb), 1)
    lane_s = lax.broadcasted_iota(jnp.int32, (N_HEADS, wb), 1)
    rowh = lax.broadcasted_iota(jnp.int32, (N_HEADS, HEAD_DIM), 0)
    for j in range(tb):
        pick = jnp.where(rowb == i * tb + j, 1.0, 0.0).astype(BF16)
        col = _dot3(new_t, pick)
        for ckv in range(2 * N_KV):
            colw = jnp.tile(col[ckv * HEAD_DIM:(ckv + 1) * HEAD_DIM], (1, wb // LANES))
            wout_ref[j, ckv] = jnp.where(lane_w == wb - 1, colw, pltpu.roll(st_ref[j, ckv], wb - 1, 1))
        new = jnp.broadcast_to(newrow_ref[j], (8, 256))
        q = q_ref[j]
        outs = []
        for kv in range(N_KV):
            kt = st_ref[j, kv].astype(BF16)
            vt = st_ref[j, N_KV + kv].astype(BF16)
            kw_new = new[:, kv * HEAD_DIM:(kv + 1) * HEAD_DIM].astype(BF16)
            vw_new = new[:, 128 + kv * HEAD_DIM:128 + (kv + 1) * HEAD_DIM].astype(BF16).astype(F32)
            s_new = _dot_nt(q, kw_new)[:, 0:1]
            s = jnp.where(lane_s >= 1, _dot(q, kt), -jnp.inf)
            m = jnp.maximum(jnp.max(s, -1, keepdims=True), s_new)
            e = jnp.exp(s - m)
            e_new = jnp.exp(s_new - m)
            denom = jnp.sum(e, -1, keepdims=True) + e_new
            outs.append(_dot_nt((e / denom).astype(BF16), vt) + (e_new / denom).astype(BF16).astype(F32) * vw_new)
        o_win = jnp.where(rowh < GQA, outs[0], outs[1])
        g = g_ref[j]
        a_ref[j] = g[:, 0:1] * ocmp_ref[j] + g[:, 1:2] * osel_ref[j] + g[:, 2:3] * o_win


def _sample_win(state4, win_new, qrot8, gates8, ocmp, osel, tb):
    db, _, _, wb = state4.shape
    b3 = lambda s: pl.BlockSpec((tb,) + s, lambda i: (i, 0, 0))
    b4 = pl.BlockSpec((tb, 2 * N_KV, HEAD_DIM, wb), lambda i: (i, 0, 0, 0))
    hd = (N_HEADS, HEAD_DIM)
    return pl.pallas_call(
        _sample_win_kernel, grid=(db // tb,),
        in_specs=[b4, b3((1, 256)), _full(win_new.shape), b3(hd), b3((8, 128)), b3(hd), b3(hd)],
        out_specs=(b3(hd), b4),
        out_shape=(jax.ShapeDtypeStruct((db,) + hd, F32), jax.ShapeDtypeStruct(state4.shape, F32)),
        compiler_params=_cparams(("arbitrary",)), name="sample_win",
    )(state4, win_new.reshape(db, 1, 256), win_new, qrot8, gates8, ocmp, osel)


def _rope_tables(pos):
    half = ROT_DIM // 2
    inv = ROPE_THETA ** (-jnp.arange(half, dtype=F32) * 2.0 / ROT_DIM)
    ang = pos.astype(F32)[:, None] * inv[None, :]
    cos, sin = jnp.cos(ang), jnp.sin(ang)
    n = pos.shape[0]
    rest = HEAD_DIM - ROT_DIM
    one, zero, z8 = jnp.ones((n, rest), F32), jnp.zeros((n, rest), F32), jnp.zeros((n, half), F32)
    c = jnp.concatenate([cos, cos, one], 1)
    s1 = jnp.concatenate([z8, sin, zero], 1)
    s2 = jnp.concatenate([-sin, z8, zero], 1)
    return tuple(jnp.tile(a, (1, 2)) for a in (c, s1, s2))


def _proj_weights(w_in):
    sizes = (SSM_WIDTH, CONV_DIM, M_HEADS, ATT_WIDTH, 6 * N_KV * HEAD_DIM, 3 * N_HEADS)
    parts, start = [], 0
    for s in sizes:
        parts.append(w_in[:, start:start + s])
        start += s
    wz, wxbc, wdt, wq, wkv, wg = parts
    zeros = jnp.zeros((D_MODEL, HEAD_DIM), w_in.dtype)
    chunks = []
    for h in range(N_HEADS):
        wh = wq[:, h * HEAD_DIM:(h + 1) * HEAD_DIM]
        chunks += [wh, zeros] if h // GQA == 0 else [zeros, wh]
    wq_pad = jnp.concatenate(chunks, 1)
    wsm = jnp.concatenate([wdt, wg, jnp.zeros((D_MODEL, LANES - M_HEADS - 3 * N_HEADS), w_in.dtype)], 1)
    wz, wxbc, wq, wq_pad, wkv, wsm = (w.astype(BF16) for w in (wz, wxbc, wq, wq_pad, wkv, wsm))
    return (wz, wxbc, wq_pad, wkv, wsm), (wz, wxbc, wq, wkv, wsm)


def _cmp_weights(pe_cmp, w1_cmp, b1_cmp, w2_cmp, b2_cmp):
    out = {}
    for i, nm in enumerate("kv"):
        w1 = w1_cmp[i].reshape(2, STRIDE, HEAD_DIM, CMP_HIDDEN)
        wbig = jnp.zeros((STRIDE, LANES, 512), F32)
        for r in range(2):
            for kv in range(N_KV):
                c0 = r * 256 + kv * CMP_HIDDEN
                wbig = wbig.at[:, kv * HEAD_DIM:(kv + 1) * HEAD_DIM, c0:c0 + CMP_HIDDEN].set(w1[r])
        out["w" + nm] = wbig.reshape(STRIDE * LANES, 512).astype(BF16)
        out["pe" + nm] = jnp.broadcast_to(pe_cmp[i].reshape(1, L_CMP * HEAD_DIM), (8, L_CMP * HEAD_DIM)).astype(BF16)
        out["w1" + nm] = w1_cmp[i].reshape(L_CMP * HEAD_DIM, CMP_HIDDEN).astype(BF16)
        out["b1" + nm] = jnp.tile(b1_cmp[i].reshape(1, CMP_HIDDEN), (1, 2))
        w2 = jnp.zeros((2 * CMP_HIDDEN, LANES), F32)
        for kv in range(N_KV):
            w2 = w2.at[kv * CMP_HIDDEN:(kv + 1) * CMP_HIDDEN, kv * HEAD_DIM:(kv + 1) * HEAD_DIM].set(w2_cmp[i])
        out["w2" + nm] = w2.astype(BF16)
        out["b2" + nm] = jnp.tile(b2_cmp[i].reshape(1, HEAD_DIM), (1, 2))
    return out


def _select_weights(n_blocks, n_cmp_padded):
    b = jnp.arange(n_blocks)[:, None]
    c = jnp.arange(n_cmp_padded)[None, :]
    k = c - 4 * b + 1
    w = jnp.where((k == 0) | (k == 4), 16.0, jnp.where((k >= 1) & (k <= 3), 32.0, 0.0))
    return w.astype(BF16)


def _pad_row(v, n=LANES):
    v = v.reshape(1, -1).astype(F32)
    return jnp.concatenate([v, jnp.zeros((1, n - v.shape[1]), F32)], 1)


def kernel(x_prompt, x_sample, cache_kv, state_win, state_ssm, state_conv, page_table, w_in, conv_w, conv_b,
           dt_bias, a_log, d_skip, ssm_norm_g, pe_cmp, w1_cmp, b1_cmp, w2_cmp, b2_cmp, att_norm_g, w_out,
           ln1_g, ln1_b, w_ff1, w_ff2, ln2_g, ln2_b):
    bsz, seq, _ = x_prompt.shape
    db, dseq, _ = x_sample.shape
    n_pages = page_table.shape[1]
    past_len = n_pages * PAGE_SIZE
    wb = state_win.shape[1]
    assert bsz == 1 and dseq == 1 and wb == WINDOW and past_len % L_SEL == 0 and seq % SEL_KEY_TILE == 0

    pw, pw_flat = _proj_weights(w_in)
    cw = _cmp_weights(pe_cmp, w1_cmp, b1_cmp, w2_cmp, b2_cmp)
    cb = conv_b.reshape(1, CONV_DIM)
    dtb, alog = _pad_row(dt_bias), _pad_row(a_log)
    dsk = jnp.repeat(d_skip, HEAD_DIM).reshape(1, SSM_WIDTH)
    ng = ssm_norm_g.reshape(1, SSM_WIDTH)
    ow = dict(wom=w_out[0:SSM_WIDTH].astype(BF16), l1g=ln1_g.reshape(1, -1),
              l1b=ln1_b.reshape(1, -1), w1=w_ff1.astype(BF16), w2=w_ff2.astype(BF16),
              l2g=ln2_g.reshape(1, -1), l2b=ln2_b.reshape(1, -1))
    woa = w_out[SSM_WIDTH:D_MODEL]
    ang = att_norm_g.reshape(ATT_WIDTH, 1)
    att_flat = dict(ang=ang.reshape(1, -1), woa=woa.astype(BF16))

    xp = x_prompt.reshape(seq, D_MODEL)
    tm = min(256, seq)
    z, xbc, qun, qrot, kv4, win, att, sm = _project(
        xp, pw, _rope_tables(jnp.arange(seq, dtype=jnp.int32)), tm, min(WINDOW, seq))
    m_out, conv_p, ssm_p = _mamba_prompt(z, xbc, sm, conv_w, cb, dtb, alog, dsk, ng)
    kcc, vcc = _compress_prompt(kv4, cw)
    n_slc = seq // L_SEL
    wsel = _select_weights(n_slc, seq // STRIDE)
    eall = (jnp.arange(n_slc)[:, None] == (jnp.arange(seq) // L_SEL)[None, :]).astype(BF16)
    a_out = _nsa_prompt(qun, qrot, sm, kcc, vcc, att, wsel, eall)
    y_prompt = _output_block(xp, m_out, a_out, att_flat, ow, tm).reshape(bsz, seq, D_MODEL)
    kv_prompt = kv4.reshape(bsz, seq, 4, N_KV, HEAD_DIM)
    win_prompt = win.reshape(bsz, min(WINDOW, seq), 2, N_KV, HEAD_DIM)
    ssm_prompt = ssm_p.reshape(bsz, M_HEADS, HEAD_DIM, D_STATE)
    conv_prompt = conv_p.reshape(bsz, CONV_W - 1, CONV_DIM)

    xs_ = x_sample.reshape(db, D_MODEL)
    pos_s = jnp.full((db,), past_len, jnp.int32)
    z, xbc, qun, qrot, kv4_s, win_s, _, sm = _project(xs_, pw_flat, _rope_tables(pos_s), db, db)
    m_out_s, conv_s, ssm_s = _mamba_sample(
        z, xbc, sm, jnp.transpose(state_conv, (1, 0, 2)), state_ssm.reshape(db, SSM_WIDTH, D_STATE),
        conv_w, cb, dtb, alog, dsk, ng)
    cache4 = jnp.transpose(cache_kv, (0, 2, 3, 4, 1)).reshape(cache_kv.shape[0], 4, N_KV * HEAD_DIM, PAGE_SIZE)
    state4 = jnp.transpose(state_win, (0, 2, 3, 4, 1)).reshape(db, 2 * N_KV, HEAD_DIM, wb)
    n_slc_s = -(-(past_len + dseq) // L_SEL)
    nbp = -(-n_slc_s // LANES) * LANES
    wsel_s = jnp.transpose(_select_weights(nbp, past_len // STRIDE))
    qun8 = qun.reshape(db, N_HEADS, HEAD_DIM)
    qrot8 = qrot.reshape(db, N_HEADS, HEAD_DIM)
    o_cmp, scores = _sample_cmp(page_table, cache4, kv4_s, qun8, cw, wsel_s, past_len)
    n_top = min(N_SEL, n_slc_s)
    idx = _sample_topk(scores[:, 0:N_KV, :].reshape(db * N_KV, nbp), n_slc_s, past_len)[:, 0:n_top]
    o_sel = _sample_sel(page_table, idx, cache4, qrot8, kv4_s, past_len, n_top)
    gates8 = jnp.pad(sm[:, M_HEADS:M_HEADS + 3 * N_HEADS].reshape(db, N_HEADS, 3), ((0, 0), (0, 0), (0, LANES - 3)))
    a8, win4 = _sample_win(state4, win_s, qrot8, gates8, o_cmp, o_sel, min(8, db))
    y_sample = _output_block(xs_, m_out_s, a8.reshape(db, ATT_WIDTH), att_flat, ow, db).reshape(db, dseq, D_MODEL)
    kv_sample = kv4_s.reshape(db, dseq, 4, N_KV, HEAD_DIM)
    win_sample = jnp.transpose(win4.reshape(db, 2, N_KV, HEAD_DIM, wb), (0, 4, 1, 2, 3))
    ssm_sample = ssm_s.reshape(db, M_HEADS, HEAD_DIM, D_STATE)
    conv_sample = jnp.transpose(conv_s, (1, 0, 2))

    return (y_prompt, y_sample, kv_prompt, win_prompt, ssm_prompt, conv_prompt,
            kv_sample, win_sample, ssm_sample, conv_sample)
```

```python
import functools
import math

import jax
import jax.numpy as jnp
from jax import lax
from jax.experimental import pallas as pl
from jax.experimental.pallas import tpu as pltpu

F32 = jnp.float32
BF16 = jnp.bfloat16

D_MODEL = 1024
HEAD_DIM = 64
SSM_WIDTH = 512
ATT_WIDTH = 512
M_HEADS = 8
M_GROUPS = 2
D_STATE = 128
CONV_W = 4
CONV_DIM = SSM_WIDTH + 2 * M_GROUPS * D_STATE
SSD_CHUNK = 256
RMS_EPS = 1e-6
N_HEADS = 8
N_KV = 2
GQA = N_HEADS // N_KV
ROT_DIM = 16
ROPE_THETA = 500000.0
L_CMP = 32
STRIDE = 16
CMP_HIDDEN = 128
L_SEL = 64
N_SEL = 16
N_LOCAL = 2
WINDOW = 512
Q_BLOCK = 128
ATT_SCALE = HEAD_DIM ** -0.5
D_FF = 4 * D_MODEL
LN_EPS = 1e-5
DEEPNORM_ALPHA = 2.0 ** 0.25
PAGE_SIZE = 128
LANES = 128
NEG = -1e30
SEL_KEY_TILE = 1024
SOFTMAX_ROW_GROUP = 16
VMEM_LIMIT = 56 * 1024 * 1024


def _cparams(sem):
    return pltpu.CompilerParams(dimension_semantics=sem, vmem_limit_bytes=VMEM_LIMIT)


def _dot(a, b):
    return jnp.dot(a, b, preferred_element_type=F32)


def _dot_nt(a, b):
    return lax.dot_general(a, b, (((1,), (1,)), ((), ())), preferred_element_type=F32)


def _split3(x):
    hi = x.astype(BF16)
    r = x - hi.astype(F32)
    mid = r.astype(BF16)
    lo = (r - mid.astype(F32)).astype(BF16)
    return hi, mid, lo


def _dot3(x, w):
    hi, mid, lo = _split3(x)
    return _dot(hi, w) + _dot(mid, w) + _dot(lo, w)


def _sigmoid(x):
    return 1.0 / (1.0 + jnp.exp(-x))


def _silu(x):
    return x * _sigmoid(x)


def _softplus(x):
    return jnp.maximum(x, 0.0) + jnp.log1p(jnp.exp(-jnp.abs(x)))


def _layer_norm(x, g, b):
    mu = jnp.mean(x, -1, keepdims=True)
    xc = x - mu
    var = jnp.mean(xc * xc, -1, keepdims=True)
    return xc * lax.rsqrt(var + LN_EPS) * g + b


def _rope128(x, c, s1, s2):
    return x * c + pltpu.roll(x, 8, 1) * s1 + pltpu.roll(x, LANES - 8, 1) * s2


def _full(shape):
    nd = len(shape)
    return pl.BlockSpec(shape, lambda *a: (0,) * nd)


def _resident(shape):
    nd = len(shape)
    return pl.BlockSpec(shape, lambda *a: (0,) * nd, pipeline_mode=pl.Buffered(1))


def _proj_kernel(x_ref, wz_ref, wxbc_ref, wq_ref, wkv_ref, wsm_ref, c_ref, s1_ref, s2_ref,
                 z_ref, xbc_ref, qun_ref, qrot_ref, kv4_ref, win_ref, att_ref, sm_ref):
    x = x_ref[...].astype(BF16)
    c, s1, s2 = c_ref[...], s1_ref[...], s2_ref[...]
    z_ref[...] = _dot(x, wz_ref[...])
    xbc_ref[...] = _dot(x, wxbc_ref[...])
    q = _dot(x, wq_ref[...]) * ATT_SCALE
    qun_ref[...] = q.astype(BF16)
    for h in range(q.shape[1] // LANES):
        sl = slice(h * LANES, (h + 1) * LANES)
        qrot_ref[:, sl] = _rope128(q[:, sl], c, s1, s2).astype(BF16)
    kv = _dot(x, wkv_ref[...])
    ks = _rope128(kv[:, 256:384], c, s1, s2)
    kw = _rope128(kv[:, 512:640], c, s1, s2)
    kv4_ref[:, 0:256] = kv[:, 0:256]
    kv4_ref[:, 256:384] = ks
    kv4_ref[:, 384:512] = kv[:, 384:512]
    win_ref[:, 0:128] = kw
    win_ref[:, 128:256] = kv[:, 640:768]
    att_ref[:, 0:128] = ks.astype(BF16)
    att_ref[:, 128:256] = kv[:, 384:512].astype(BF16)
    att_ref[:, 256:384] = kw.astype(BF16)
    att_ref[:, 384:512] = kv[:, 640:768].astype(BF16)
    sm = _dot(x, wsm_ref[...])
    lane = lax.broadcasted_iota(jnp.int32, sm.shape, 1)
    is_gate = (lane >= M_HEADS) & (lane < M_HEADS + 3 * N_HEADS)
    sm_ref[...] = jnp.where(is_gate, _sigmoid(sm), sm)


def _project(x, weights, tables, tm, win_rows):
    t = x.shape[0]
    nt = t // tm
    nwb = win_rows // tm
    wz, wxbc, wq, wkv, wsm = weights
    qw = wq.shape[1]
    row = lambda w: pl.BlockSpec((tm, w), lambda i: (i, 0))
    out_shape = (
        jax.ShapeDtypeStruct((t, 512), F32),
        jax.ShapeDtypeStruct((t, 1024), F32),
        jax.ShapeDtypeStruct((t, qw), BF16),
        jax.ShapeDtypeStruct((t, qw), BF16),
        jax.ShapeDtypeStruct((t, 512), F32),
        jax.ShapeDtypeStruct((win_rows, 256), F32),
        jax.ShapeDtypeStruct((t, 512), BF16),
        jax.ShapeDtypeStruct((t, 128), F32),
    )
    out_specs = (row(512), row(1024), row(qw), row(qw), row(512),
                 pl.BlockSpec((tm, 256), lambda i: (jnp.maximum(i - (nt - nwb), 0), 0)),
                 row(512), row(128))
    in_specs = [row(D_MODEL), _full(wz.shape), _full(wxbc.shape), _full(wq.shape), _full(wkv.shape),
                _full(wsm.shape), row(128), row(128), row(128)]
    return pl.pallas_call(
        _proj_kernel, grid=(nt,), in_specs=in_specs, out_specs=out_specs, out_shape=out_shape,
        compiler_params=_cparams(("arbitrary",)), name="proj",
    )(x, wz, wxbc, wq, wkv, wsm, *tables)


def _mamba_prompt_kernel(z_ref, xbc_ref, sm_ref, cw_ref, cb_ref, dtb_ref, alog_ref, dsk_ref, ng_ref,
                         y_ref, conv_ref, ssm_ref, ubuf, state):
    c = pl.program_id(0)
    nc = pl.num_programs(0)
    L = SSD_CHUNK

    @pl.when(c == 0)
    def _():
        ubuf[0:8, :] = jnp.zeros((8, CONV_DIM), F32)
        state[...] = jnp.zeros(state.shape, F32)

    u = xbc_ref[...]
    ubuf[8:8 + L, :] = u
    acc = u * cw_ref[3:4, :] + cb_ref[...]
    for k in range(CONV_W - 1):
        acc = acc + ubuf[pl.ds(8 - (CONV_W - 1) + k, L), :] * cw_ref[k:k + 1, :]
    tail = ubuf[L:L + 8, :]
    ubuf[0:8, :] = tail
    xa = _silu(acc)
    xs = xa[:, 0:SSM_WIDTH]
    bm = xa[:, SSM_WIDTH:SSM_WIDTH + 256]
    cm = xa[:, SSM_WIDTH + 256:SSM_WIDTH + 512]

    lane = lax.broadcasted_iota(jnp.int32, (L, LANES), 1)
    head_lane = lane < M_HEADS
    dt = jnp.where(head_lane, _softplus(sm_ref[...] + dtb_ref[...]), 0.0)
    a_row = jnp.where(head_lane[0:1], -jnp.exp(alog_ref[...]), 0.0)
    da = dt * a_row
    ri = lax.broadcasted_iota(jnp.int32, (L, L), 0)
    ci = lax.broadcasted_iota(jnp.int32, (L, L), 1)
    causal = ri >= ci
    tril = jnp.where(causal, 1.0, 0.0).astype(BF16)
    da_hi, da_mid, da_lo = _split3(da)
    cum = _dot(tril, da_hi) + _dot(tril, da_mid) + _dot(tril, da_lo)
    cum_t = cum.T
    er = lax.broadcasted_iota(jnp.int32, (LANES, SSM_WIDTH), 0)
    ec = lax.broadcasted_iota(jnp.int32, (LANES, SSM_WIDTH), 1)
    expand = jnp.where(er == (ec >> 6), 1.0, 0.0).astype(BF16)
    dtx = _dot3(dt, expand)
    cumx = _dot3(cum, expand)
    cumx_last = cumx[L - 1:L, :]
    xc = xs * dtx
    xw = xc * jnp.exp(cumx_last - cumx)
    ecx = jnp.exp(cumx)
    dec_last = jnp.exp(cumx_last)

    half = lax.broadcasted_iota(jnp.int32, (L, LANES), 1) < HEAD_DIM
    y_parts = []
    for g in range(M_GROUPS):
        bg = bm[:, g * D_STATE:(g + 1) * D_STATE].astype(BF16)
        cg = cm[:, g * D_STATE:(g + 1) * D_STATE].astype(BF16)
        gs = slice(g * 256, (g + 1) * 256)
        scores = _dot_nt(cg, bg)
        h_prev = state[:, gs]
        y_off = _dot(cg, h_prev.astype(BF16)) * ecx[:, gs]
        st = _dot(bg.T, xw[:, gs].astype(BF16))
        state[:, gs] = h_prev * dec_last[:, gs] + st
        for pr in range(2):
            ha = g * 4 + pr * 2
            xcp = xc[:, ha * HEAD_DIM:(ha + 2) * HEAD_DIM].astype(BF16)
            outs = []
            for hh in (ha, ha + 1):
                seg = cum[:, hh:hh + 1] - cum_t[hh:hh + 1, :]
                dec = jnp.exp(jnp.where(causal, seg, -jnp.inf))
                outs.append(_dot((scores * dec).astype(BF16), xcp))
            y_parts.append(jnp.where(half, outs[0], outs[1]) + y_off[:, pr * 128:(pr + 1) * 128])
    y = jnp.concatenate(y_parts, axis=1) + dsk_ref[...] * xs
    y = y * _silu(z_ref[...])
    outs = []
    for g in range(M_GROUPS):
        yg = y[:, g * 256:(g + 1) * 256]
        outs.append(yg * lax.rsqrt(jnp.mean(yg * yg, -1, keepdims=True) + RMS_EPS))
    y_ref[...] = (jnp.concatenate(outs, axis=1) * ng_ref[...]).astype(BF16)

    @pl.when(c == nc - 1)
    def _():
        conv_ref[...] = ubuf[pl.ds(8 - (CONV_W - 1), CONV_W - 1), :]
        ssm_ref[...] = state[...].T


def _mamba_prompt(z, xbc, sm, cw, cb, dtb, alog, dsk, ng):
    t = z.shape[0]
    L = SSD_CHUNK
    row = lambda w: pl.BlockSpec((L, w), lambda i: (i, 0))
    return pl.pallas_call(
        _mamba_prompt_kernel, grid=(t // L,),
        in_specs=[row(512), row(1024), row(128), _full(cw.shape), _full(cb.shape), _full(dtb.shape),
                  _full(alog.shape), _full(dsk.shape), _full(ng.shape)],
        out_specs=(row(512), _full((CONV_W - 1, CONV_DIM)), _full((SSM_WIDTH, D_STATE))),
        out_shape=(jax.ShapeDtypeStruct((t, 512), BF16),
                   jax.ShapeDtypeStruct((CONV_W - 1, CONV_DIM), F32),
                   jax.ShapeDtypeStruct((SSM_WIDTH, D_STATE), F32)),
        scratch_shapes=[pltpu.VMEM((L + 8, CONV_DIM), F32), pltpu.VMEM((D_STATE, SSM_WIDTH), F32)],
        compiler_params=_cparams(("arbitrary",)), name="mamba_prompt",
    )(z, xbc, sm, cw, cb, dtb, alog, dsk, ng)


def _mamba_sample_kernel(z_ref, xbc_ref, sm_ref, sc_ref, h0_ref, cw_ref, cb_ref, dtb_ref, alog_ref, dsk_ref,
                         ng_ref, y_ref, conv_ref, h_ref, xct, dect, brow, crow, xcrow, decrow, xsrow, yrow):
    b = pl.program_id(0)
    nb = pl.num_programs(0)
    db = z_ref.shape[0]

    @pl.when(b == 0)
    def _():
        u = xbc_ref[...]
        acc = u * cw_ref[3:4, :] + cb_ref[...]
        for k in range(CONV_W - 1):
            acc = acc + sc_ref[k] * cw_ref[k:k + 1, :]
        conv_ref[0] = sc_ref[1]
        conv_ref[1] = sc_ref[2]
        conv_ref[2] = u
        xa = _silu(acc)
        xs = xa[:, 0:SSM_WIDTH]
        lane = lax.broadcasted_iota(jnp.int32, (db, LANES), 1)
        head_lane = lane < M_HEADS
        dt = jnp.where(head_lane, _softplus(sm_ref[...] + dtb_ref[...]), 0.0)
        da = dt * jnp.where(head_lane[0:1], -jnp.exp(alog_ref[...]), 0.0)
        er = lax.broadcasted_iota(jnp.int32, (LANES, SSM_WIDTH), 0)
        ec = lax.broadcasted_iota(jnp.int32, (LANES, SSM_WIDTH), 1)
        expand = jnp.where(er == (ec >> 6), 1.0, 0.0).astype(BF16)
        xc = xs * _dot3(dt, expand)
        dec = jnp.exp(_dot3(da, expand))
        xct[...] = xc.T
        dect[...] = dec.T
        xcrow[...] = xc
        decrow[...] = dec
        xsrow[...] = xs
        brow[...] = xa[:, SSM_WIDTH:SSM_WIDTH + 256]
        crow[...] = xa[:, SSM_WIDTH + 256:SSM_WIDTH + 512]

    r = lax.broadcasted_iota(jnp.int32, (db, LANES), 0)
    pick = jnp.where(r == b, 1.0, 0.0).astype(BF16)
    colx = _dot3(xct[...], pick)
    cold = _dot3(dect[...], pick)
    bv = brow[pl.ds(b, 1), :]
    cv = crow[pl.ds(b, 1), :]
    h0 = h0_ref[0]
    ys = []
    for g in range(M_GROUPS):
        rs = slice(g * 256, (g + 1) * 256)
        bg = bv[:, g * D_STATE:(g + 1) * D_STATE]
        cg = cv[:, g * D_STATE:(g + 1) * D_STATE]
        h_ref[0, rs, :] = cold[rs] * h0[rs] + colx[rs] * bg
        cb_s = jnp.sum(cg * bg, -1, keepdims=True)
        c8 = jnp.broadcast_to(cg, (8, D_STATE)).astype(BF16)
        y_off = _dot_nt(c8, h0[rs].astype(BF16))[0:1]
        ys.append(y_off * decrow[pl.ds(b, 1), rs] + cb_s * xcrow[pl.ds(b, 1), rs])
    yrow[pl.ds(b, 1), :] = jnp.concatenate(ys, axis=1)

    @pl.when(b == nb - 1)
    def _():
        y = (yrow[...] + dsk_ref[...] * xsrow[...]) * _silu(z_ref[...])
        outs = []
        for g in range(M_GROUPS):
            yg = y[:, g * 256:(g + 1) * 256]
            outs.append(yg * lax.rsqrt(jnp.mean(yg * yg, -1, keepdims=True) + RMS_EPS))
        y_ref[...] = (jnp.concatenate(outs, axis=1) * ng_ref[...]).astype(BF16)


def _mamba_sample(z, xbc, sm, sc_t, h0, cw, cb, dtb, alog, dsk, ng):
    db = z.shape[0]
    vm = lambda s: pltpu.VMEM(s, F32)
    return pl.pallas_call(
        _mamba_sample_kernel, grid=(db,),
        in_specs=[_full(z.shape), _full(xbc.shape), _full(sm.shape), _full(sc_t.shape),
                  pl.BlockSpec((1, SSM_WIDTH, D_STATE), lambda b: (b, 0, 0)),
                  _full(cw.shape), _full(cb.shape), _full(dtb.shape), _full(alog.shape), _full(dsk.shape),
                  _full(ng.shape)],
        out_specs=(_full((db, 512)), _full((CONV_W - 1, db, CONV_DIM)),
                   pl.BlockSpec((1, SSM_WIDTH, D_STATE), lambda b: (b, 0, 0))),
        out_shape=(jax.ShapeDtypeStruct((db, 512), BF16),
                   jax.ShapeDtypeStruct((CONV_W - 1, db, CONV_DIM), F32),
                   jax.ShapeDtypeStruct((db, SSM_WIDTH, D_STATE), F32)),
        scratch_shapes=[vm((SSM_WIDTH, db)), vm((SSM_WIDTH, db)), vm((db, 256)), vm((db, 256)),
                        vm((db, SSM_WIDTH)), vm((db, SSM_WIDTH)), vm((db, SSM_WIDTH)), vm((db, SSM_WIDTH))],
        compiler_params=_cparams(("arbitrary",)), name="mamba_sample",
    )(z, xbc, sm, sc_t, h0, cw, cb, dtb, alog, dsk, ng)


def _stride_block_proj(x_ref, n_blocks, w_ref):
    acc = jnp.zeros((n_blocks, 512), F32)
    for j in range(0, STRIDE, 2):
        xj = jnp.concatenate([x_ref[pl.ds(j + d, n_blocks, stride=STRIDE), :].astype(BF16) for d in range(2)], axis=1)
        acc = acc + _dot(xj, w_ref[pl.ds(j * LANES, 2 * LANES), :])
    return acc


def _cmp_proj_kernel(xk_ref, xv_ref, wk_ref, wv_ref, hk_ref, hv_ref):
    n_blocks = xk_ref.shape[0] // STRIDE
    hk_ref[...] = _stride_block_proj(xk_ref, n_blocks, wk_ref)
    hv_ref[...] = _stride_block_proj(xv_ref, n_blocks, wv_ref)


def _cmp_const(pe_ref, w1f_ref, b1_ref):
    c = _dot(pe_ref[...], w1f_ref[...])[0:1]
    return jnp.concatenate([c, c], axis=1) + b1_ref[...]


def _cmp_mlp(h0, h1_next, const, w2_ref, b2_ref):
    pre = h0 + h1_next + const
    return _dot(_silu(pre).astype(BF16), w2_ref[...]) + b2_ref[...]


def _cmp_finish_kernel(hk_ref, hv_ref, pek, w1k, b1k, w2k, b2k, pev, w1v, b1v, w2v, b2v, kcc_ref, vcc_ref):
    n = hk_ref.shape[0]
    for h_ref, pe, w1f, b1, w2, b2, o_ref in ((hk_ref, pek, w1k, b1k, w2k, b2k, kcc_ref),
                                              (hv_ref, pev, w1v, b1v, w2v, b2v, vcc_ref)):
        h = h_ref[...]
        h1_next = pltpu.roll(h[:, 256:512], n - 1, 0)
        o_ref[...] = _cmp_mlp(h[:, 0:256], h1_next, _cmp_const(pe, w1f, b1), w2, b2).astype(BF16)


def _compress_prompt(kv4, cw):
    t = kv4.shape[0]
    rows = min(t, 2048)
    n_blocks = t // STRIDE
    hk, hv = pl.pallas_call(
        _cmp_proj_kernel, grid=(t // rows,),
        in_specs=[pl.BlockSpec((rows, LANES), lambda i: (i, 0)), pl.BlockSpec((rows, LANES), lambda i: (i, 1)),
                  _full(cw["wk"].shape), _full(cw["wv"].shape)],
        out_specs=(pl.BlockSpec((rows // STRIDE, 512), lambda i: (i, 0)),) * 2,
        out_shape=(jax.ShapeDtypeStruct((n_blocks, 512), F32),) * 2,
        compiler_params=_cparams(("arbitrary",)), name="cmp_proj",
    )(kv4, kv4, cw["wk"], cw["wv"])
    names = ("pek", "w1k", "b1k", "w2k", "b2k", "pev", "w1v", "b1v", "w2v", "b2v")
    args = [cw[k] for k in names]
    return pl.pallas_call(
        _cmp_finish_kernel, grid=(1,),
        in_specs=[_full(hk.shape), _full(hv.shape)] + [_full(a.shape) for a in args],
        out_specs=(_full((n_blocks, 128)),) * 2,
        out_shape=(jax.ShapeDtypeStruct((n_blocks, 128), BF16),) * 2,
        compiler_params=_cparams(("arbitrary",)), name="cmp_finish",
    )(hk, hv, *args)


def _softmax_rows(s):
    m = jnp.max(s, -1, keepdims=True)
    m = jnp.where(jnp.isfinite(m), m, 0.0)
    e = jnp.exp(s - m)
    return e / jnp.maximum(jnp.sum(e, -1, keepdims=True), 1e-30)


def _knock_out_top(sc, blkf, n_pick):
    for _ in range(n_pick):
        m = jnp.max(sc, 0, keepdims=True)
        idx = jnp.min(jnp.where(sc == m, blkf, 1e9), 0, keepdims=True)
        sc = jnp.where(blkf == idx, -jnp.inf, sc)
    return sc


def _nsa_prompt_kernel(qun_ref, qrot_ref, sm_ref, kcc_ref, vcc_ref, att_ref, wsel_ref, eall_ref, o_ref,
                       m_scr, l_scr, acc_scr, sel_scr, bias_scr, p_scr, imp_scr, biasw_scr, pw_scr, owin_scr):
    i = pl.program_id(0)
    s0 = i * Q_BLOCK
    ncp = kcc_ref.shape[0]
    nb = wsel_ref.shape[0]
    kt = min(SEL_KEY_TILE, att_ref.shape[0])
    n_top = min(N_SEL, nb)

    def head(ref, h):
        return ref[:, h * LANES:(h + 1) * LANES]

    def stacked(ref, h0, n):
        return jnp.concatenate([head(ref, h0 + j) for j in range(n)], axis=0)

    def rows(x, j):
        return x[j * Q_BLOCK:(j + 1) * Q_BLOCK]

    colc = lax.broadcasted_iota(jnp.int32, (Q_BLOCK, ncp), 1)
    rowq = lax.broadcasted_iota(jnp.int32, (Q_BLOCK, ncp), 0) + s0
    vis = colc * STRIDE + (L_CMP - 1) <= rowq
    rg = SOFTMAX_ROW_GROUP
    bias_scr[:, 0:ncp] = jnp.where(vis, 0.0, -jnp.inf)
    s_all = _dot_nt(stacked(qun_ref, 0, N_HEADS), kcc_ref[...])
    for kv in range(N_KV):
        for r in range(Q_BLOCK // rg):
            bias = bias_scr[r * rg:(r + 1) * rg, 0:ncp]
            imp = None
            for g in range(GQA):
                r0 = (kv * GQA + g) * Q_BLOCK + r * rg
                p = _softmax_rows(s_all[r0:r0 + rg] + bias)
                p_scr[r0:r0 + rg, 0:ncp] = p.astype(BF16)
                imp = p if imp is None else imp + p
            imp_scr[kv, r * rg:(r + 1) * rg, :] = imp
    o_all = _dot(p_scr[:, 0:ncp], vcc_ref[...])
    o_cmp = [rows(o_all, h) for h in range(N_HEADS)]

    wk = WINDOW + Q_BLOCK
    w0 = pl.multiple_of(jnp.maximum(s0 - WINDOW, 0), Q_BLOCK)
    kw = att_ref[pl.ds(w0, wk), 256:384]
    vw = att_ref[pl.ds(w0, wk), 384:512]
    kposw = lax.broadcasted_iota(jnp.int32, (Q_BLOCK, wk), 1) + w0
    roww = lax.broadcasted_iota(jnp.int32, (Q_BLOCK, wk), 0) + s0
    okw = (kposw <= roww) & (kposw > roww - WINDOW)
    biasw_scr[...] = jnp.where(okw, 0.0, -jnp.inf)
    s_all = _dot_nt(stacked(qrot_ref, 0, N_HEADS), kw)
    for h in range(N_HEADS):
        for r in range(Q_BLOCK // rg):
            r0 = h * Q_BLOCK + r * rg
            p = _softmax_rows(s_all[r0:r0 + rg] + biasw_scr[r * rg:(r + 1) * rg, :])
            pw_scr[r0:r0 + rg, :] = p.astype(BF16)
    owin_scr[...] = _dot(pw_scr[...], vw)

    blk = lax.broadcasted_iota(jnp.int32, (nb, Q_BLOCK), 0)
    qpl = lax.broadcasted_iota(jnp.int32, (nb, Q_BLOCK), 1) + s0
    cur = qpl >> 6
    valid = blk * L_SEL <= qpl
    forced = (blk == 0) | ((blk <= cur) & (blk > cur - N_LOCAL))
    blkf = blk.astype(F32)
    wsel = wsel_ref[...]

    o_sel, o_win = [], []
    rowk = lax.broadcasted_iota(jnp.int32, (Q_BLOCK, kt), 0) + s0
    lanek = lax.broadcasted_iota(jnp.int32, (Q_BLOCK, kt), 1)
    n_tiles = (s0 + Q_BLOCK - 1) // kt + 1
    nch = kt // LANES
    bgrp = min(LANES, nb)

    def lane_chunks(x):
        return [x[:, c * LANES:(c + 1) * LANES] for c in range(nch)]

    scs = []
    for kv in range(N_KV):
        hi, mid, lo = _split3(imp_scr[kv])
        sc = _dot_nt(wsel, hi) + _dot_nt(wsel, mid) + _dot_nt(wsel, lo)
        scs.append(jnp.where(forced, -jnp.inf, jnp.where(valid, sc, -jnp.inf)))
    sc2 = _knock_out_top(jnp.concatenate(scs, axis=1), jnp.concatenate([blkf, blkf], axis=1), n_top - 1 - N_LOCAL)
    for kv in range(N_KV):
        picked = valid & (sc2[:, kv * Q_BLOCK:(kv + 1) * Q_BLOCK] == -jnp.inf)
        sel_scr[kv * Q_BLOCK:(kv + 1) * Q_BLOCK, :] = jnp.where(picked, 0.0, NEG).T.astype(BF16)
    for h in range(N_HEADS):
        m_scr[h] = jnp.full((Q_BLOCK, LANES), NEG, F32)
        l_scr[h] = jnp.zeros((Q_BLOCK, LANES), F32)
        acc_scr[h] = jnp.zeros((Q_BLOCK, LANES), F32)
    q8 = stacked(qrot_ref, 0, N_HEADS)

    def tile_body(t, carry, diagonal):
        k0 = pl.multiple_of(t * kt, kt)
        ks = att_ref[pl.ds(k0, kt), 0:128]
        vs = att_ref[pl.ds(k0, kt), 128:256]
        b0 = pl.multiple_of(((t * (kt // L_SEL)) // bgrp) * bgrp, bgrp)
        bias2 = _dot(sel_scr[:, pl.ds(b0, bgrp)], eall_ref[pl.ds(b0, bgrp), pl.ds(k0, kt)])
        biases = [rows(bias2, kv) for kv in range(N_KV)]
        if diagonal:
            biases = [jnp.where(lanek + k0 <= rowk, b, NEG) for b in biases]
        s8 = _dot_nt(q8, ks)
        ps, alphas = [], []
        for h in range(N_HEADS):
            s = rows(s8, h) + biases[h // GQA]
            m_old = m_scr[h]
            m_new = jnp.maximum(m_old, jnp.max(functools.reduce(jnp.maximum, lane_chunks(s)), -1, keepdims=True))
            alpha = jnp.exp(m_old - m_new)
            p = jnp.exp(s - jnp.tile(m_new, (1, nch)))
            l_scr[h] = alpha * l_scr[h] + functools.reduce(jnp.add, lane_chunks(p))
            m_scr[h] = m_new
            ps.append(p.astype(BF16))
            alphas.append(alpha)
        pv = _dot(jnp.concatenate(ps, axis=0), vs)
        for h in range(N_HEADS):
            acc_scr[h] = alphas[h] * acc_scr[h] + rows(pv, h)
        return carry

    lax.fori_loop(0, n_tiles - 1, functools.partial(tile_body, diagonal=False), 0)
    tile_body(n_tiles - 1, 0, diagonal=True)
    for h in range(N_HEADS):
        o_sel.append(acc_scr[h] / jnp.sum(l_scr[h], -1, keepdims=True))

    o_win = [rows(owin_scr[...], h) for h in range(N_HEADS)]

    sm = sm_ref[...]
    lane = lax.broadcasted_iota(jnp.int32, (Q_BLOCK, LANES), 1)
    chunks = []
    for c in range(N_HEADS // 2):
        parts = []
        for h in (2 * c, 2 * c + 1):
            kv = h // GQA
            gc = M_HEADS + 3 * h
            a = sm[:, gc:gc + 1] * o_cmp[h] + sm[:, gc + 1:gc + 2] * o_sel[h] + sm[:, gc + 2:gc + 3] * o_win[h]
            a = jnp.where((lane >= kv * HEAD_DIM) & (lane < (kv + 1) * HEAD_DIM), a, 0.0)
            if kv != h % 2:
                a = pltpu.roll(a, HEAD_DIM, 1)
            parts.append(a)
        chunks.append(parts[0] + parts[1])
    o_ref[...] = jnp.concatenate(chunks, axis=1)


def _nsa_prompt(qun, qrot, sm, kcc, vcc, att, wsel, eall):
    t = qun.shape[0]
    ncp = kcc.shape[0]
    wk = WINDOW + Q_BLOCK
    row = lambda w: pl.BlockSpec((Q_BLOCK, w), lambda i: (i, 0))
    return pl.pallas_call(
        _nsa_prompt_kernel, grid=(t // Q_BLOCK,),
        in_specs=[row(1024), row(1024), row(128), _resident(kcc.shape), _resident(vcc.shape),
                  _resident(att.shape), _resident(wsel.shape), _resident(eall.shape)],
        out_specs=row(512),
        out_shape=jax.ShapeDtypeStruct((t, 512), F32),
        scratch_shapes=[pltpu.VMEM((N_HEADS, Q_BLOCK, LANES), F32)] * 3 + [
            pltpu.VMEM((N_KV * Q_BLOCK, wsel.shape[0]), BF16),
            pltpu.VMEM((Q_BLOCK, ncp), F32),
            pltpu.VMEM((N_HEADS * Q_BLOCK, ncp), BF16),
            pltpu.VMEM((N_KV, Q_BLOCK, ncp), F32),
            pltpu.VMEM((Q_BLOCK, wk), F32),
            pltpu.VMEM((N_HEADS * Q_BLOCK, wk), BF16),
            pltpu.VMEM((N_HEADS * Q_BLOCK, LANES), F32)],
        compiler_params=_cparams(("arbitrary",)), name="nsa_prompt",
    )(qun, qrot, sm, kcc, vcc, att, wsel, eall)


def _out_kernel(x_ref, m_ref, a_ref, ang_ref, wom_ref, woa_ref, l1g_ref, l1b_ref, w1_ref, w2_ref,
                l2g_ref, l2b_ref, y_ref):
    a = a_ref[...]
    an = a * lax.rsqrt(jnp.mean(a * a, -1, keepdims=True) + RMS_EPS) * ang_ref[...]
    mix = _dot(m_ref[...], wom_ref[...]) + _dot(an.astype(BF16), woa_ref[...])
    h = _layer_norm(DEEPNORM_ALPHA * x_ref[...] + mix, l1g_ref[...], l1b_ref[...])
    hb = h.astype(BF16)
    f = jnp.zeros(h.shape, F32)
    for c in range(D_FF // 1024):
        cs = slice(c * 1024, (c + 1) * 1024)
        u = jnp.maximum(_dot(hb, w1_ref[:, cs]), 0.0)
        f = f + _dot((u * u).astype(BF16), w2_ref[cs, :])
    y_ref[...] = _layer_norm(DEEPNORM_ALPHA * h + f, l2g_ref[...], l2b_ref[...])


def _output_block(x, m_out, a_out, att, ow, tm):
    t = x.shape[0]
    row = lambda w: pl.BlockSpec((tm, w), lambda i: (i, 0))
    names = ("l1g", "l1b", "w1", "w2", "l2g", "l2b")
    args = [att["ang"], ow["wom"], att["woa"]] + [ow[k] for k in names]
    return pl.pallas_call(
        _out_kernel, grid=(t // tm,),
        in_specs=[row(D_MODEL), row(512), row(a_out.shape[1])] + [_resident(a.shape) for a in args],
        out_specs=row(D_MODEL),
        out_shape=jax.ShapeDtypeStruct((t, D_MODEL), F32),
        compiler_params=_cparams(("arbitrary",)), name="out_block",
    )(x, m_out, a_out, *args)


def _sample_cmp_kernel(pt_ref, cache_ref, new_ref, q_ref, wk_ref, wv_ref,
                       pek, w1k, b1k, w2k, b2k, pev, w1v, b1v, w2v, b2v, wsel_ref,
                       ocmp_ref, score_ref, buf, xs, sem, *, n_pages, past_len):
    b = pl.program_id(0)
    nb = pl.num_programs(0)
    n_blocks = n_pages * PAGE_SIZE // STRIDE

    def page_copy(bb, slot, pg):
        return pltpu.make_async_copy(cache_ref.at[pt_ref[bb * n_pages + pg], pl.ds(0, 2)],
                                     buf.at[slot, pg], sem.at[slot])

    def start_all(bb, slot):
        def body(pg, carry):
            page_copy(bb, slot, pg).start()
            return carry
        lax.fori_loop(0, n_pages, body, 0)

    def wait_all(bb, slot):
        def body(pg, carry):
            page_copy(bb, slot, pg).wait()
            return carry
        lax.fori_loop(0, n_pages, body, 0)

    slot = b % 2

    @pl.when(b == 0)
    def _():
        start_all(0, 0)

    @pl.when(b + 1 < nb)
    def _():
        start_all(b + 1, 1 - slot)

    wait_all(b, slot)

    def to_rows(pg, carry):
        r0 = pl.multiple_of(pg * PAGE_SIZE, PAGE_SIZE)
        for c in range(2):
            xs[c, pl.ds(r0, PAGE_SIZE), :] = buf[slot, pg, c].T
        return carry
    lax.fori_loop(0, n_pages, to_rows, 0, unroll=math.gcd(n_pages, 8))

    new = jnp.broadcast_to(new_ref[0], (8, 512))
    rowi = lax.broadcasted_iota(jnp.int32, (n_blocks, 256), 0)
    toks = []
    for c, w_ref, pe, w1f, b1, w2, b2 in ((0, wk_ref, pek, w1k, b1k, w2k, b2k),
                                          (1, wv_ref, pev, w1v, b1v, w2v, b2v)):
        h = _stride_block_proj(xs.at[c], n_blocks, w_ref)
        h_new = _dot(new[:, c * LANES:(c + 1) * LANES].astype(BF16), w_ref[0:LANES, 256:512])[0:1]
        h1_next = jnp.where(rowi == n_blocks - 1, h_new, pltpu.roll(h[:, 256:512], n_blocks - 1, 0))
        toks.append(_cmp_mlp(h[:, 0:256], h1_next, _cmp_const(pe, w1f, b1), w2, b2).astype(BF16))
    kcc, vcc = toks

    q = q_ref[0]
    zq = jnp.zeros_like(q)
    rowh = lax.broadcasted_iota(jnp.int32, (N_HEADS, LANES), 0)
    qpad = jnp.where(rowh < GQA, jnp.concatenate([q, zq], 1), jnp.concatenate([zq, q], 1))
    s = _dot_nt(qpad, kcc)
    colc = lax.broadcasted_iota(jnp.int32, s.shape, 1)
    p = _softmax_rows(jnp.where(colc * STRIDE + (L_CMP - 1) <= past_len, s, -jnp.inf))
    o = _dot(p.astype(BF16), vcc)
    ocmp_ref[0] = jnp.where(rowh[:, 0:HEAD_DIM] < GQA, o[:, 0:HEAD_DIM], o[:, HEAD_DIM:])
    imp0 = p[0:1] + p[1:2] + p[2:3] + p[3:4]
    imp1 = p[4:5] + p[5:6] + p[6:7] + p[7:8]
    rowp = lax.broadcasted_iota(jnp.int32, p.shape, 0)
    imp = jnp.where(rowp == 0, imp0, jnp.where(rowp == 1, imp1, 0.0))
    score_ref[0] = _dot3(imp, wsel_ref[...])


def _sample_cmp(page_table, cache4, kv4_s, q8, cw, wsel_s, past_len):
    db, n_pages = page_table.shape
    names = ("pek", "w1k", "b1k", "w2k", "b2k", "pev", "w1v", "b1v", "w2v", "b2v")
    args = [cw[k] for k in names]
    nbp = wsel_s.shape[1]
    full = lambda a: pl.BlockSpec(a.shape, lambda b, pt: (0,) * a.ndim)
    grid_spec = pltpu.PrefetchScalarGridSpec(
        num_scalar_prefetch=1, grid=(db,),
        in_specs=[pl.BlockSpec(memory_space=pl.ANY),
                  pl.BlockSpec((1, 1, 512), lambda b, pt: (b, 0, 0)),
                  pl.BlockSpec((1, N_HEADS, HEAD_DIM), lambda b, pt: (b, 0, 0)),
                  full(cw["wk"]), full(cw["wv"])] + [full(a) for a in args] + [full(wsel_s)],
        out_specs=(pl.BlockSpec((1, N_HEADS, HEAD_DIM), lambda b, pt: (b, 0, 0)),
                   pl.BlockSpec((1, 8, nbp), lambda b, pt: (b, 0, 0))),
        scratch_shapes=[pltpu.VMEM((2, n_pages, 2, LANES, PAGE_SIZE), F32),
                        pltpu.VMEM((2, n_pages * PAGE_SIZE, LANES), F32),
                        pltpu.SemaphoreType.DMA((2,))],
    )
    return pl.pallas_call(
        functools.partial(_sample_cmp_kernel, n_pages=n_pages, past_len=past_len),
        grid_spec=grid_spec,
        out_shape=(jax.ShapeDtypeStruct((db, N_HEADS, HEAD_DIM), F32), jax.ShapeDtypeStruct((db, 8, nbp), F32)),
        compiler_params=_cparams(("arbitrary",)), name="sample_cmp",
    )(page_table.reshape(-1), cache4, kv4_s.reshape(db, 1, 512), q8, cw["wk"], cw["wv"], *args, wsel_s)


def _sample_topk_kernel(score_ref, idx_ref, *, n_slc, qp):
    sc = score_ref[...]
    blk = lax.broadcasted_iota(jnp.int32, sc.shape, 1)
    cur = qp // L_SEL
    valid = (blk * L_SEL <= qp) & (blk < n_slc)
    forced = (blk == 0) | ((blk <= cur) & (blk > cur - N_LOCAL))
    sc = jnp.where(forced & (blk < n_slc), jnp.inf, jnp.where(valid, sc, -jnp.inf))
    real = blk < n_slc
    blkf = blk.astype(F32)
    taken = jnp.where(real, 0.0, 1.0)
    out = jnp.zeros(idx_ref.shape, F32)
    lane = lax.broadcasted_iota(jnp.int32, idx_ref.shape, 1)
    for it in range(min(N_SEL, n_slc)):
        m = jnp.max(jnp.where(taken == 0.0, sc, -jnp.inf), -1, keepdims=True)
        cand = (sc == m) & (taken == 0.0)
        idx = jnp.min(jnp.where(cand, blkf, 1e9), -1, keepdims=True)
        pick = blkf == idx
        taken = jnp.where(pick, 1.0, taken)
        out = jnp.where(lane == it, idx, out)
    idx_ref[...] = out.astype(jnp.int32)


def _sample_topk(scores, n_slc, qp):
    rows, nbp = scores.shape
    return pl.pallas_call(
        functools.partial(_sample_topk_kernel, n_slc=n_slc, qp=qp), grid=(1,),
        in_specs=[_full(scores.shape)], out_specs=_full((rows, LANES)),
        out_shape=jax.ShapeDtypeStruct((rows, LANES), jnp.int32),
        compiler_params=_cparams(("arbitrary",)), name="sample_topk",
    )(scores)


def _sample_sel_kernel(pt_ref, idx_ref, cache_ref, q_ref, new_ref, osel_ref, buf, sem, *, n_pages, past_len, n_top):
    b = pl.program_id(0)
    nb = pl.num_programs(0)
    per_page = PAGE_SIZE // L_SEL
    n_past_blocks = past_len // L_SEL
    n_cp = N_KV * n_top

    def blk_copy(bb, slot, j):
        kv = j // n_top
        n = j - kv * n_top
        blk = jnp.minimum(idx_ref[bb * n_cp + j], n_past_blocks - 1)
        page = pt_ref[bb * n_pages + blk // per_page]
        return pltpu.make_async_copy(
            cache_ref.at[page, pl.ds(2, 2), pl.ds(pl.multiple_of(kv * HEAD_DIM, HEAD_DIM), HEAD_DIM), :],
            buf.at[slot, kv, :, :, pl.ds(pl.multiple_of(n * PAGE_SIZE, PAGE_SIZE), PAGE_SIZE)], sem.at[slot])

    def start_all(bb, slot):
        def body(j, c):
            blk_copy(bb, slot, j).start()
            return c
        lax.fori_loop(0, n_cp, body, 0)

    def wait_all(bb, slot):
        def body(j, c):
            blk_copy(bb, slot, j).wait()
            return c
        lax.fori_loop(0, n_cp, body, 0)

    slot = b % 2

    @pl.when(b == 0)
    def _():
        start_all(0, 0)

    @pl.when(b + 1 < nb)
    def _():
        start_all(b + 1, 1 - slot)

    wait_all(b, slot)

    q = q_ref[0]
    new = jnp.broadcast_to(new_ref[0], (8, 512))
    nk = n_top * PAGE_SIZE
    lane = lax.broadcasted_iota(jnp.int32, (N_HEADS, nk), 1)
    outs = []
    for kv in range(N_KV):
        ks_new = new[:, 256 + kv * HEAD_DIM:256 + (kv + 1) * HEAD_DIM].astype(BF16)
        vs_new = new[:, 384 + kv * HEAD_DIM:384 + (kv + 1) * HEAD_DIM].astype(BF16).astype(F32)
        s_new = _dot_nt(q, ks_new)[:, 0:1]
        kt = buf[slot, kv, 0].astype(BF16)
        vt = buf[slot, kv, 1].astype(BF16)
        blk_of_lane = jnp.zeros((N_HEADS, nk), jnp.int32)
        for n in range(n_top):
            blk_of_lane = jnp.where((lane >> 7) == n, idx_ref[b * n_cp + kv * n_top + n], blk_of_lane)
        in_block = ((lane & (PAGE_SIZE - 1)) >> 6) == (blk_of_lane & (per_page - 1))
        ok = in_block & (blk_of_lane < n_past_blocks)
        s = jnp.where(ok, _dot(q, kt), -jnp.inf)
        m = jnp.maximum(jnp.max(s, -1, keepdims=True), s_new)
        e = jnp.exp(s - m)
        e_new = jnp.exp(s_new - m)
        denom = jnp.sum(e, -1, keepdims=True) + e_new
        o = _dot_nt((e / denom).astype(BF16), vt) + (e_new / denom).astype(BF16).astype(F32) * vs_new
        outs.append(o)
    row = lax.broadcasted_iota(jnp.int32, (N_HEADS, HEAD_DIM), 0)
    osel_ref[0] = jnp.where(row < GQA, outs[0], outs[1])


def _sample_sel(page_table, idx, cache4, qrot8, kv4_s, past_len, n_top):
    db, n_pages = page_table.shape
    assert PAGE_SIZE // L_SEL == 2
    hb = pl.BlockSpec((1, N_HEADS, HEAD_DIM), lambda b, pt, ix: (b, 0, 0))
    grid_spec = pltpu.PrefetchScalarGridSpec(
        num_scalar_prefetch=2, grid=(db,),
        in_specs=[pl.BlockSpec(memory_space=pl.ANY), hb,
                  pl.BlockSpec((1, 1, 512), lambda b, pt, ix: (b, 0, 0))],
        out_specs=hb,
        scratch_shapes=[pltpu.VMEM((2, N_KV, 2, HEAD_DIM, n_top * PAGE_SIZE), F32), pltpu.SemaphoreType.DMA((2,))],
    )
    return pl.pallas_call(
        functools.partial(_sample_sel_kernel, n_pages=n_pages, past_len=past_len, n_top=n_top),
        grid_spec=grid_spec,
        out_shape=jax.ShapeDtypeStruct((db, N_HEADS, HEAD_DIM), F32),
        compiler_params=_cparams(("arbitrary",)), name="sample_sel",
    )(page_table.reshape(-1), idx.reshape(-1), cache4, qrot8, kv4_s.reshape(db, 1, 512))


def _sample_win_kernel(st_ref, newrow_ref, newall_ref, q_ref, g_ref, ocmp_ref, osel_ref, a_ref, wout_ref):
    i = pl.program_id(0)
    tb, wb = st_ref.shape[0], st_ref.shape[3]
    db = newall_ref.shape[0]
    new_t = newall_ref[...].T
    rowb = lax.broadcasted_iota(jnp.int32, (db, LANES), 0)
    lane_w = lax.broadcasted_iota(jnp.int32, (HEAD_DIM, wb), 1)
    lane_s = lax.broadcasted_iota(jnp.int32, (N_HEADS, wb), 1)
    rowh = lax.broadcasted_iota(jnp.int32, (N_HEADS, HEAD_DIM), 0)
    for j in range(tb):
        pick = jnp.where(rowb == i * tb + j, 1.0, 0.0).astype(BF16)
        col = _dot3(new_t, pick)
        for ckv in range(2 * N_KV):
            colw = jnp.tile(col[ckv * HEAD_DIM:(ckv + 1) * HEAD_DIM], (1, wb // LANES))
            wout_ref[j, ckv] = jnp.where(lane_w == wb - 1, colw, pltpu.roll(st_ref[j, ckv], wb - 1, 1))
        new = jnp.broadcast_to(newrow_ref[j], (8, 256))
        q = q_ref[j]
        outs = []
        for kv in range(N_KV):
            kt = st_ref[j, kv].astype(BF16)
            vt = st_ref[j, N_KV + kv].astype(BF16)
            kw_new = new[:, kv * HEAD_DIM:(kv + 1) * HEAD_DIM].astype(BF16)
            vw_new = new[:, 128 + kv * HEAD_DIM:128 + (kv + 1) * HEAD_DIM].astype(BF16).astype(F32)
            s_new = _dot_nt(q, kw_new)[:, 0:1]
            s = jnp.where(lane_s >= 1, _dot(q, kt), -jnp.inf)
            m = jnp.maximum(jnp.max(s, -1, keepdims=True), s_new)
            e = jnp.exp(s - m)
            e_new = jnp.exp(s_new - m)
            denom = jnp.sum(e, -1, keepdims=True) + e_new
            outs.append(_dot_nt((e / denom).astype(BF16), vt) + (e_new / denom).astype(BF16).astype(F32) * vw_new)
        o_win = jnp.where(rowh < GQA, outs[0], outs[1])
        g = g_ref[j]
        a_ref[j] = g[:, 0:1] * ocmp_ref[j] + g[:, 1:2] * osel_ref[j] + g[:, 2:3] * o_win


def _sample_win(state4, win_new, qrot8, gates8, ocmp, osel, tb):
    db, _, _, wb = state4.shape
    b3 = lambda s: pl.BlockSpec((tb,) + s, lambda i: (i, 0, 0))
    b4 = pl.BlockSpec((tb, 2 * N_KV, HEAD_DIM, wb), lambda i: (i, 0, 0, 0))
    hd = (N_HEADS, HEAD_DIM)
    return pl.pallas_call(
        _sample_win_kernel, grid=(db // tb,),
        in_specs=[b4, b3((1, 256)), _full(win_new.shape), b3(hd), b3((8, 128)), b3(hd), b3(hd)],
        out_specs=(b3(hd), b4),
        out_shape=(jax.ShapeDtypeStruct((db,) + hd, F32), jax.ShapeDtypeStruct(state4.shape, F32)),
        compiler_params=_cparams(("arbitrary",)), name="sample_win",
    )(state4, win_new.reshape(db, 1, 256), win_new, qrot8, gates8, ocmp, osel)


def _rope_tables(pos):
    half = ROT_DIM // 2
    inv = ROPE_THETA ** (-jnp.arange(half, dtype=F32) * 2.0 / ROT_DIM)
    ang = pos.astype(F32)[:, None] * inv[None, :]
    cos, sin = jnp.cos(ang), jnp.sin(ang)
    n = pos.shape[0]
    rest = HEAD_DIM - ROT_DIM
    one, zero, z8 = jnp.ones((n, rest), F32), jnp.zeros((n, rest), F32), jnp.zeros((n, half), F32)
    c = jnp.concatenate([cos, cos, one], 1)
    s1 = jnp.concatenate([z8, sin, zero], 1)
    s2 = jnp.concatenate([-sin, z8, zero], 1)
    return tuple(jnp.tile(a, (1, 2)) for a in (c, s1, s2))


def _proj_weights(w_in):
    sizes = (SSM_WIDTH, CONV_DIM, M_HEADS, ATT_WIDTH, 6 * N_KV * HEAD_DIM, 3 * N_HEADS)
    parts, start = [], 0
    for s in sizes:
        parts.append(w_in[:, start:start + s])
        start += s
    wz, wxbc, wdt, wq, wkv, wg = parts
    zeros = jnp.zeros((D_MODEL, HEAD_DIM), w_in.dtype)
    chunks = []
    for h in range(N_HEADS):
        wh = wq[:, h * HEAD_DIM:(h + 1) * HEAD_DIM]
        chunks += [wh, zeros] if h // GQA == 0 else [zeros, wh]
    wq_pad = jnp.concatenate(chunks, 1)
    wsm = jnp.concatenate([wdt, wg, jnp.zeros((D_MODEL, LANES - M_HEADS - 3 * N_HEADS), w_in.dtype)], 1)
    wz, wxbc, wq, wq_pad, wkv, wsm = (w.astype(BF16) for w in (wz, wxbc, wq, wq_pad, wkv, wsm))
    return (wz, wxbc, wq_pad, wkv, wsm), (wz, wxbc, wq, wkv, wsm)


def _cmp_weights(pe_cmp, w1_cmp, b1_cmp, w2_cmp, b2_cmp):
    out = {}
    for i, nm in enumerate("kv"):
        w1 = w1_cmp[i].reshape(2, STRIDE, HEAD_DIM, CMP_HIDDEN)
        wbig = jnp.zeros((STRIDE, LANES, 512), F32)
        for r in range(2):
            for kv in range(N_KV):
                c0 = r * 256 + kv * CMP_HIDDEN
                wbig = wbig.at[:, kv * HEAD_DIM:(kv + 1) * HEAD_DIM, c0:c0 + CMP_HIDDEN].set(w1[r])
        out["w" + nm] = wbig.reshape(STRIDE * LANES, 512).astype(BF16)
        out["pe" + nm] = jnp.broadcast_to(pe_cmp[i].reshape(1, L_CMP * HEAD_DIM), (8, L_CMP * HEAD_DIM)).astype(BF16)
        out["w1" + nm] = w1_cmp[i].reshape(L_CMP * HEAD_DIM, CMP_HIDDEN).astype(BF16)
        out["b1" + nm] = jnp.tile(b1_cmp[i].reshape(1, CMP_HIDDEN), (1, 2))
        w2 = jnp.zeros((2 * CMP_HIDDEN, LANES), F32)
        for kv in range(N_KV):
            w2 = w2.at[kv * CMP_HIDDEN:(kv + 1) * CMP_HIDDEN, kv * HEAD_DIM:(kv + 1) * HEAD_DIM].set(w2_cmp[i])
        out["w2" + nm] = w2.astype(BF16)
        out["b2" + nm] = jnp.tile(b2_cmp[i].reshape(1, HEAD_DIM), (1, 2))
    return out


def _select_weights(n_blocks, n_cmp_padded):
    b = jnp.arange(n_blocks)[:, None]
    c = jnp.arange(n_cmp_padded)[None, :]
    k = c - 4 * b + 1
    w = jnp.where((k == 0) | (k == 4), 16.0, jnp.where((k >= 1) & (k <= 3), 32.0, 0.0))
    return w.astype(BF16)


def _pad_row(v, n=LANES):
    v = v.reshape(1, -1).astype(F32)
    return jnp.concatenate([v, jnp.zeros((1, n - v.shape[1]), F32)], 1)


def kernel(x_prompt, x_sample, cache_kv, state_win, state_ssm, state_conv, page_table, w_in, conv_w, conv_b,
           dt_bias, a_log, d_skip, ssm_norm_g, pe_cmp, w1_cmp, b1_cmp, w2_cmp, b2_cmp, att_norm_g, w_out,
           ln1_g, ln1_b, w_ff1, w_ff2, ln2_g, ln2_b):
    bsz, seq, _ = x_prompt.shape
    db, dseq, _ = x_sample.shape
    n_pages = page_table.shape[1]
    past_len = n_pages * PAGE_SIZE
    wb = state_win.shape[1]
    assert bsz == 1 and dseq == 1 and wb == WINDOW and past_len % L_SEL == 0 and seq % SEL_KEY_TILE == 0

    pw, pw_flat = _proj_weights(w_in)
    cw = _cmp_weights(pe_cmp, w1_cmp, b1_cmp, w2_cmp, b2_cmp)
    cb = conv_b.reshape(1, CONV_DIM)
    dtb, alog = _pad_row(dt_bias), _pad_row(a_log)
    dsk = jnp.repeat(d_skip, HEAD_DIM).reshape(1, SSM_WIDTH)
    ng = ssm_norm_g.reshape(1, SSM_WIDTH)
    ow = dict(wom=w_out[0:SSM_WIDTH].astype(BF16), l1g=ln1_g.reshape(1, -1),
              l1b=ln1_b.reshape(1, -1), w1=w_ff1.astype(BF16), w2=w_ff2.astype(BF16),
              l2g=ln2_g.reshape(1, -1), l2b=ln2_b.reshape(1, -1))
    woa = w_out[SSM_WIDTH:D_MODEL]
    ang = att_norm_g.reshape(ATT_WIDTH, 1)
    att_flat = dict(ang=ang.reshape(1, -1), woa=woa.astype(BF16))

    xp = x_prompt.reshape(seq, D_MODEL)
    tm = min(256, seq)
    z, xbc, qun, qrot, kv4, win, att, sm = _project(
        xp, pw, _rope_tables(jnp.arange(seq, dtype=jnp.int32)), tm, min(WINDOW, seq))
    m_out, conv_p, ssm_p = _mamba_prompt(z, xbc, sm, conv_w, cb, dtb, alog, dsk, ng)
    kcc, vcc = _compress_prompt(kv4, cw)
    n_slc = seq // L_SEL
    wsel = _select_weights(n_slc, seq // STRIDE)
    eall = (jnp.arange(n_slc)[:, None] == (jnp.arange(seq) // L_SEL)[None, :]).astype(BF16)
    a_out = _nsa_prompt(qun, qrot, sm, kcc, vcc, att, wsel, eall)
    y_prompt = _output_block(xp, m_out, a_out, att_flat, ow, tm).reshape(bsz, seq, D_MODEL)
    kv_prompt = kv4.reshape(bsz, seq, 4, N_KV, HEAD_DIM)
    win_prompt = win.reshape(bsz, min(WINDOW, seq), 2, N_KV, HEAD_DIM)
    ssm_prompt = ssm_p.reshape(bsz, M_HEADS, HEAD_DIM, D_STATE)
    conv_prompt = conv_p.reshape(bsz, CONV_W - 1, CONV_DIM)

    xs_ = x_sample.reshape(db, D_MODEL)
    pos_s = jnp.full((db,), past_len, jnp.int32)
    z, xbc, qun, qrot, kv4_s, win_s, _, sm = _project(xs_, pw_flat, _rope_tables(pos_s), db, db)
    m_out_s, conv_s, ssm_s = _mamba_sample(
        z, xbc, sm, jnp.transpose(state_conv, (1, 0, 2)), state_ssm.reshape(db, SSM_WIDTH, D_STATE),
        conv_w, cb, dtb, alog, dsk, ng)
    cache4 = jnp.transpose(cache_kv, (0, 2, 3, 4, 1)).reshape(cache_kv.shape[0], 4, N_KV * HEAD_DIM, PAGE_SIZE)
    state4 = jnp.transpose(state_win, (0, 2, 3, 4, 1)).reshape(db, 2 * N_KV, HEAD_DIM, wb)
    n_slc_s = -(-(past_len + dseq) // L_SEL)
    nbp = -(-n_slc_s // LANES) * LANES
    wsel_s = jnp.transpose(_select_weights(nbp, past_len // STRIDE))
    qun8 = qun.reshape(db, N_HEADS, HEAD_DIM)
    qrot8 = qrot.reshape(db, N_HEADS, HEAD_DIM)
    o_cmp, scores = _sample_cmp(page_table, cache4, kv4_s, qun8, cw, wsel_s, past_len)
    n_top = min(N_SEL, n_slc_s)
    idx = _sample_topk(scores[:, 0:N_KV, :].reshape(db * N_KV, nbp), n_slc_s, past_len)[:, 0:n_top]
    o_sel = _sample_sel(page_table, idx, cache4, qrot8, kv4_s, past_len, n_top)
    gates8 = jnp.pad(sm[:, M_HEADS:M_HEADS + 3 * N_HEADS].reshape(db, N_HEADS, 3), ((0, 0), (0, 0), (0, LANES - 3)))
    a8, win4 = _sample_win(state4, win_s, qrot8, gates8, o_cmp, o_sel, min(8, db))
    y_sample = _output_block(xs_, m_out_s, a8.reshape(db, ATT_WIDTH), att_flat, ow, db).reshape(db, dseq, D_MODEL)
    kv_sample = kv4_s.reshape(db, dseq, 4, N_KV, HEAD_DIM)
    win_sample = jnp.transpose(win4.reshape(db, 2, N_KV, HEAD_DIM, wb), (0, 4, 1, 2, 3))
    ssm_sample = ssm_s.reshape(db, M_HEADS, HEAD_DIM, D_STATE)
    conv_sample = jnp.transpose(conv_s, (1, 0, 2))

    return (y_prompt, y_sample, kv_prompt, win_prompt, ssm_prompt, conv_prompt,
            kv_sample, win_sample, ssm_sample, conv_sample)
```

```python
import functools
import math

import jax
import jax.numpy as jnp
from jax import lax
from jax.experimental import pallas as pl
from jax.experimental.pallas import tpu as pltpu

F32 = jnp.float32
BF16 = jnp.bfloat16

D_MODEL = 1024
HEAD_DIM = 64
SSM_WIDTH = 512
ATT_WIDTH = 512
M_HEADS = 8
M_GROUPS = 2
D_STATE = 128
CONV_W = 4
CONV_DIM = SSM_WIDTH + 2 * M_GROUPS * D_STATE
SSD_CHUNK = 256
RMS_EPS = 1e-6
N_HEADS = 8
N_KV = 2
GQA = N_HEADS // N_KV
ROT_DIM = 16
ROPE_THETA = 500000.0
L_CMP = 32
STRIDE = 16
CMP_HIDDEN = 128
L_SEL = 64
N_SEL = 16
N_LOCAL = 2
WINDOW = 512
Q_BLOCK = 128
ATT_SCALE = HEAD_DIM ** -0.5
D_FF = 4 * D_MODEL
LN_EPS = 1e-5
DEEPNORM_ALPHA = 2.0 ** 0.25
PAGE_SIZE = 128
LANES = 128
NEG = -1e30
SEL_KEY_TILE = 1024
SOFTMAX_ROW_GROUP = 16
VMEM_LIMIT = 56 * 1024 * 1024


def _cparams(sem):
    return pltpu.CompilerParams(dimension_semantics=sem, vmem_limit_bytes=VMEM_LIMIT)


def _dot(a, b):
    return jnp.dot(a, b, preferred_element_type=F32)


def _dot_nt(a, b):
    return lax.dot_general(a, b, (((1,), (1,)), ((), ())), preferred_element_type=F32)


def _split3(x):
    hi = x.astype(BF16)
    r = x - hi.astype(F32)
    mid = r.astype(BF16)
    lo = (r - mid.astype(F32)).astype(BF16)
    return hi, mid, lo


def _dot3(x, w):
    hi, mid, lo = _split3(x)
    return _dot(hi, w) + _dot(mid, w) + _dot(lo, w)


def _sigmoid(x):
    return 1.0 / (1.0 + jnp.exp(-x))


def _silu(x):
    return x * _sigmoid(x)


def _softplus(x):
    return jnp.maximum(x, 0.0) + jnp.log1p(jnp.exp(-jnp.abs(x)))


def _layer_norm(x, g, b):
    mu = jnp.mean(x, -1, keepdims=True)
    xc = x - mu
    var = jnp.mean(xc * xc, -1, keepdims=True)
    return xc * lax.rsqrt(var + LN_EPS) * g + b


def _rope128(x, c, s1, s2):
    return x * c + pltpu.roll(x, 8, 1) * s1 + pltpu.roll(x, LANES - 8, 1) * s2


def _full(shape):
    nd = len(shape)
    return pl.BlockSpec(shape, lambda *a: (0,) * nd)


def _resident(shape):
    nd = len(shape)
    return pl.BlockSpec(shape, lambda *a: (0,) * nd, pipeline_mode=pl.Buffered(1))


def _proj_kernel(x_ref, wz_ref, wxbc_ref, wq_ref, wkv_ref, wsm_ref, c_ref, s1_ref, s2_ref,
                 z_ref, xbc_ref, qun_ref, qrot_ref, kv4_ref, win_ref, att_ref, sm_ref):
    x = x_ref[...].astype(BF16)
    c, s1, s2 = c_ref[...], s1_ref[...], s2_ref[...]
    z_ref[...] = _dot(x, wz_ref[...])
    xbc_ref[...] = _dot(x, wxbc_ref[...])
    q = _dot(x, wq_ref[...]) * ATT_SCALE
    qun_ref[...] = q.astype(BF16)
    for h in range(q.shape[1] // LANES):
        sl = slice(h * LANES, (h + 1) * LANES)
        qrot_ref[:, sl] = _rope128(q[:, sl], c, s1, s2).astype(BF16)
    kv = _dot(x, wkv_ref[...])
    ks = _rope128(kv[:, 256:384], c, s1, s2)
    kw = _rope128(kv[:, 512:640], c, s1, s2)
    kv4_ref[:, 0:256] = kv[:, 0:256]
    kv4_ref[:, 256:384] = ks
    kv4_ref[:, 384:512] = kv[:, 384:512]
    win_ref[:, 0:128] = kw
    win_ref[:, 128:256] = kv[:, 640:768]
    att_ref[:, 0:128] = ks.astype(BF16)
    att_ref[:, 128:256] = kv[:, 384:512].astype(BF16)
    att_ref[:, 256:384] = kw.astype(BF16)
    att_ref[:, 384:512] = kv[:, 640:768].astype(BF16)
    sm = _dot(x, wsm_ref[...])
    lane = lax.broadcasted_iota(jnp.int32, sm.shape, 1)
    is_gate = (lane >= M_HEADS) & (lane < M_HEADS + 3 * N_HEADS)
    sm_ref[...] = jnp.where(is_gate, _sigmoid(sm), sm)


def _project(x, weights, tables, tm, win_rows):
    t = x.shape[0]
    nt = t // tm
    nwb = win_rows // tm
    wz, wxbc, wq, wkv, wsm = weights
    qw = wq.shape[1]
    row = lambda w: pl.BlockSpec((tm, w), lambda i: (i, 0))
    out_shape = (
        jax.ShapeDtypeStruct((t, 512), F32),
        jax.ShapeDtypeStruct((t, 1024), F32),
        jax.ShapeDtypeStruct((t, qw), BF16),
        jax.ShapeDtypeStruct((t, qw), BF16),
        jax.ShapeDtypeStruct((t, 512), F32),
        jax.ShapeDtypeStruct((win_rows, 256), F32),
        jax.ShapeDtypeStruct((t, 512), BF16),
        jax.ShapeDtypeStruct((t, 128), F32),
    )
    out_specs = (row(512), row(1024), row(qw), row(qw), row(512),
                 pl.BlockSpec((tm, 256), lambda i: (jnp.maximum(i - (nt - nwb), 0), 0)),
                 row(512), row(128))
    in_specs = [row(D_MODEL), _full(wz.shape), _full(wxbc.shape), _full(wq.shape), _full(wkv.shape),
                _full(wsm.shape), row(128), row(128), row(128)]
    return pl.pallas_call(
        _proj_kernel, grid=(nt,), in_specs=in_specs, out_specs=out_specs, out_shape=out_shape,
        compiler_params=_cparams(("arbitrary",)), name="proj",
    )(x, wz, wxbc, wq, wkv, wsm, *tables)


def _mamba_prompt_kernel(z_ref, xbc_ref, sm_ref, cw_ref, cb_ref, dtb_ref, alog_ref, dsk_ref, ng_ref,
                         y_ref, conv_ref, ssm_ref, ubuf, state):
    c = pl.program_id(0)
    nc = pl.num_programs(0)
    L = SSD_CHUNK

    @pl.when(c == 0)
    def _():
        ubuf[0:8, :] = jnp.zeros((8, CONV_DIM), F32)
        state[...] = jnp.zeros(state.shape, F32)

    u = xbc_ref[...]
    ubuf[8:8 + L, :] = u
    acc = u * cw_ref[3:4, :] + cb_ref[...]
    for k in range(CONV_W - 1):
        acc = acc + ubuf[pl.ds(8 - (CONV_W - 1) + k, L), :] * cw_ref[k:k + 1, :]
    tail = ubuf[L:L + 8, :]
    ubuf[0:8, :] = tail
    xa = _silu(acc)
    xs = xa[:, 0:SSM_WIDTH]
    bm = xa[:, SSM_WIDTH:SSM_WIDTH + 256]
    cm = xa[:, SSM_WIDTH + 256:SSM_WIDTH + 512]

    lane = lax.broadcasted_iota(jnp.int32, (L, LANES), 1)
    head_lane = lane < M_HEADS
    dt = jnp.where(head_lane, _softplus(sm_ref[...] + dtb_ref[...]), 0.0)
    a_row = jnp.where(head_lane[0:1], -jnp.exp(alog_ref[...]), 0.0)
    da = dt * a_row
    ri = lax.broadcasted_iota(jnp.int32, (L, L), 0)
    ci = lax.broadcasted_iota(jnp.int32, (L, L), 1)
    causal = ri >= ci
    tril = jnp.where(causal, 1.0, 0.0).astype(BF16)
    da_hi, da_mid, da_lo = _split3(da)
    cum = _dot(tril, da_hi) + _dot(tril, da_mid) + _dot(tril, da_lo)
    cum_t = cum.T
    er = lax.broadcasted_iota(jnp.int32, (LANES, SSM_WIDTH), 0)
    ec = lax.broadcasted_iota(jnp.int32, (LANES, SSM_WIDTH), 1)
    expand = jnp.where(er == (ec >> 6), 1.0, 0.0).astype(BF16)
    dtx = _dot3(dt, expand)
    cumx = _dot3(cum, expand)
    cumx_last = cumx[L - 1:L, :]
    xc = xs * dtx
    xw = xc * jnp.exp(cumx_last - cumx)
    ecx = jnp.exp(cumx)
    dec_last = jnp.exp(cumx_last)

    half = lax.broadcasted_iota(jnp.int32, (L, LANES), 1) < HEAD_DIM
    y_parts = []
    for g in range(M_GROUPS):
        bg = bm[:, g * D_STATE:(g + 1) * D_STATE].astype(BF16)
        cg = cm[:, g * D_STATE:(g + 1) * D_STATE].astype(BF16)
        gs = slice(g * 256, (g + 1) * 256)
        scores = _dot_nt(cg, bg)
        h_prev = state[:, gs]
        y_off = _dot(cg, h_prev.astype(BF16)) * ecx[:, gs]
        st = _dot(bg.T, xw[:, gs].astype(BF16))
        state[:, gs] = h_prev * dec_last[:, gs] + st
        for pr in range(2):
            ha = g * 4 + pr * 2
            xcp = xc[:, ha * HEAD_DIM:(ha + 2) * HEAD_DIM].astype(BF16)
            outs = []
            for hh in (ha, ha + 1):
                seg = cum[:, hh:hh + 1] - cum_t[hh:hh + 1, :]
                dec = jnp.exp(jnp.where(causal, seg, -jnp.inf))
                outs.append(_dot((scores * dec).astype(BF16), xcp))
            y_parts.append(jnp.where(half, outs[0], outs[1]) + y_off[:, pr * 128:(pr + 1) * 128])
    y = jnp.concatenate(y_parts, axis=1) + dsk_ref[...] * xs
    y = y * _silu(z_ref[...])
    outs = []
    for g in range(M_GROUPS):
        yg = y[:, g * 256:(g + 1) * 256]
        outs.append(yg * lax.rsqrt(jnp.mean(yg * yg, -1, keepdims=True) + RMS_EPS))
    y_ref[...] = (jnp.concatenate(outs, axis=1) * ng_ref[...]).astype(BF16)

    @pl.when(c == nc - 1)
    def _():
        conv_ref[...] = ubuf[pl.ds(8 - (CONV_W - 1), CONV_W - 1), :]
        ssm_ref[...] = state[...].T


def _mamba_prompt(z, xbc, sm, cw, cb, dtb, alog, dsk, ng):
    t = z.shape[0]
    L = SSD_CHUNK
    row = lambda w: pl.BlockSpec((L, w), lambda i: (i, 0))
    return pl.pallas_call(
        _mamba_prompt_kernel, grid=(t // L,),
        in_specs=[row(512), row(1024), row(128), _full(cw.shape), _full(cb.shape), _full(dtb.shape),
                  _full(alog.shape), _full(dsk.shape), _full(ng.shape)],
        out_specs=(row(512), _full((CONV_W - 1, CONV_DIM)), _full((SSM_WIDTH, D_STATE))),
        out_shape=(jax.ShapeDtypeStruct((t, 512), BF16),
                   jax.ShapeDtypeStruct((CONV_W - 1, CONV_DIM), F32),
                   jax.ShapeDtypeStruct((SSM_WIDTH, D_STATE), F32)),
        scratch_shapes=[pltpu.VMEM((L + 8, CONV_DIM), F32), pltpu.VMEM((D_STATE, SSM_WIDTH), F32)],
        compiler_params=_cparams(("arbitrary",)), name="mamba_prompt",
    )(z, xbc, sm, cw, cb, dtb, alog, dsk, ng)


def _mamba_sample_kernel(z_ref, xbc_ref, sm_ref, sc_ref, h0_ref, cw_ref, cb_ref, dtb_ref, alog_ref, dsk_ref,
                         ng_ref, y_ref, conv_ref, h_ref, xct, dect, brow, crow, xcrow, decrow, xsrow, yrow):
    b = pl.program_id(0)
    nb = pl.num_programs(0)
    db = z_ref.shape[0]

    @pl.when(b == 0)
    def _():
        u = xbc_ref[...]
        acc = u * cw_ref[3:4, :] + cb_ref[...]
        for k in range(CONV_W - 1):
            acc = acc + sc_ref[k] * cw_ref[k:k + 1, :]
        conv_ref[0] = sc_ref[1]
        conv_ref[1] = sc_ref[2]
        conv_ref[2] = u
        xa = _silu(acc)
        xs = xa[:, 0:SSM_WIDTH]
        lane = lax.broadcasted_iota(jnp.int32, (db, LANES), 1)
        head_lane = lane < M_HEADS
        dt = jnp.where(head_lane, _softplus(sm_ref[...] + dtb_ref[...]), 0.0)
        da = dt * jnp.where(head_lane[0:1], -jnp.exp(alog_ref[...]), 0.0)
        er = lax.broadcasted_iota(jnp.int32, (LANES, SSM_WIDTH), 0)
        ec = lax.broadcasted_iota(jnp.int32, (LANES, SSM_WIDTH), 1)
        expand = jnp.where(er == (ec >> 6), 1.0, 0.0).astype(BF16)
        xc = xs * _dot3(dt, expand)
        dec = jnp.exp(_dot3(da, expand))
        xct[...] = xc.T
        dect[...] = dec.T
        xcrow[...] = xc
        decrow[...] = dec
        xsrow[...] = xs
        brow[...] = xa[:, SSM_WIDTH:SSM_WIDTH + 256]
        crow[...] = xa[:, SSM_WIDTH + 256:SSM_WIDTH + 512]

    r = lax.broadcasted_iota(jnp.int32, (db, LANES), 0)
    pick = jnp.where(r == b, 1.0, 0.0).astype(BF16)
    colx = _dot3(xct[...], pick)
    cold = _dot3(dect[...], pick)
    bv = brow[pl.ds(b, 1), :]
    cv = crow[pl.ds(b, 1), :]
    h0 = h0_ref[0]
    ys = []
    for g in range(M_GROUPS):
        rs = slice(g * 256, (g + 1) * 256)
        bg = bv[:, g * D_STATE:(g + 1) * D_STATE]
        cg = cv[:, g * D_STATE:(g + 1) * D_STATE]
        h_ref[0, rs, :] = cold[rs] * h0[rs] + colx[rs] * bg
        cb_s = jnp.sum(cg * bg, -1, keepdims=True)
        c8 = jnp.broadcast_to(cg, (8, D_STATE)).astype(BF16)
        y_off = _dot_nt(c8, h0[rs].astype(BF16))[0:1]
        ys.append(y_off * decrow[pl.ds(b, 1), rs] + cb_s * xcrow[pl.ds(b, 1), rs])
    yrow[pl.ds(b, 1), :] = jnp.concatenate(ys, axis=1)

    @pl.when(b == nb - 1)
    def _():
        y = (yrow[...] + dsk_ref[...] * xsrow[...]) * _silu(z_ref[...])
        outs = []
        for g in range(M_GROUPS):
            yg = y[:, g * 256:(g + 1) * 256]
            outs.append(yg * lax.rsqrt(jnp.mean(yg * yg, -1, keepdims=True) + RMS_EPS))
        y_ref[...] = (jnp.concatenate(outs, axis=1) * ng_ref[...]).astype(BF16)


def _mamba_sample(z, xbc, sm, sc_t, h0, cw, cb, dtb, alog, dsk, ng):
    db = z.shape[0]
    vm = lambda s: pltpu.VMEM(s, F32)
    return pl.pallas_call(
        _mamba_sample_kernel, grid=(db,),
        in_specs=[_full(z.shape), _full(xbc.shape), _full(sm.shape), _full(sc_t.shape),
                  pl.BlockSpec((1, SSM_WIDTH, D_STATE), lambda b: (b, 0, 0)),
                  _full(cw.shape), _full(cb.shape), _full(dtb.shape), _full(alog.shape), _full(dsk.shape),
                  _full(ng.shape)],
        out_specs=(_full((db, 512)), _full((CONV_W - 1, db, CONV_DIM)),
                   pl.BlockSpec((1, SSM_WIDTH, D_STATE), lambda b: (b, 0, 0))),
        out_shape=(jax.ShapeDtypeStruct((db, 512), BF16),
                   jax.ShapeDtypeStruct((CONV_W - 1, db, CONV_DIM), F32),
                   jax.ShapeDtypeStruct((db, SSM_WIDTH, D_STATE), F32)),
        scratch_shapes=[vm((SSM_WIDTH, db)), vm((SSM_WIDTH, db)), vm((db, 256)), vm((db, 256)),
                        vm((db, SSM_WIDTH)), vm((db, SSM_WIDTH)), vm((db, SSM_WIDTH)), vm((db, SSM_WIDTH))],
        compiler_params=_cparams(("arbitrary",)), name="mamba_sample",
    )(z, xbc, sm, sc_t, h0, cw, cb, dtb, alog, dsk, ng)


def _stride_block_proj(x_ref, n_blocks, w_ref):
    acc = jnp.zeros((n_blocks, 512), F32)
    for j in range(0, STRIDE, 2):
        xj = jnp.concatenate([x_ref[pl.ds(j + d, n_blocks, stride=STRIDE), :].astype(BF16) for d in range(2)], axis=1)
        acc = acc + _dot(xj, w_ref[pl.ds(j * LANES, 2 * LANES), :])
    return acc


def _cmp_proj_kernel(xk_ref, xv_ref, wk_ref, wv_ref, hk_ref, hv_ref):
    n_blocks = xk_ref.shape[0] // STRIDE
    hk_ref[...] = _stride_block_proj(xk_ref, n_blocks, wk_ref)
    hv_ref[...] = _stride_block_proj(xv_ref, n_blocks, wv_ref)


def _cmp_const(pe_ref, w1f_ref, b1_ref):
    c = _dot(pe_ref[...], w1f_ref[...])[0:1]
    return jnp.concatenate([c, c], axis=1) + b1_ref[...]


def _cmp_mlp(h0, h1_next, const, w2_ref, b2_ref):
    pre = h0 + h1_next + const
    return _dot(_silu(pre).astype(BF16), w2_ref[...]) + b2_ref[...]


def _cmp_finish_kernel(hk_ref, hv_ref, pek, w1k, b1k, w2k, b2k, pev, w1v, b1v, w2v, b2v, kcc_ref, vcc_ref):
    n = hk_ref.shape[0]
    for h_ref, pe, w1f, b1, w2, b2, o_ref in ((hk_ref, pek, w1k, b1k, w2k, b2k, kcc_ref),
                                              (hv_ref, pev, w1v, b1v, w2v, b2v, vcc_ref)):
        h = h_ref[...]
        h1_next = pltpu.roll(h[:, 256:512], n - 1, 0)
        o_ref[...] = _cmp_mlp(h[:, 0:256], h1_next, _cmp_const(pe, w1f, b1), w2, b2).astype(BF16)


def _compress_prompt(kv4, cw):
    t = kv4.shape[0]
    rows = min(t, 2048)
    n_blocks = t // STRIDE
    hk, hv = pl.pallas_call(
        _cmp_proj_kernel, grid=(t // rows,),
        in_specs=[pl.BlockSpec((rows, LANES), lambda i: (i, 0)), pl.BlockSpec((rows, LANES), lambda i: (i, 1)),
                  _full(cw["wk"].shape), _full(cw["wv"].shape)],
        out_specs=(pl.BlockSpec((rows // STRIDE, 512), lambda i: (i, 0)),) * 2,
        out_shape=(jax.ShapeDtypeStruct((n_blocks, 512), F32),) * 2,
        compiler_params=_cparams(("arbitrary",)), name="cmp_proj",
    )(kv4, kv4, cw["wk"], cw["wv"])
    names = ("pek", "w1k", "b1k", "w2k", "b2k", "pev", "w1v", "b1v", "w2v", "b2v")
    args = [cw[k] for k in names]
    return pl.pallas_call(
        _cmp_finish_kernel, grid=(1,),
        in_specs=[_full(hk.shape), _full(hv.shape)] + [_full(a.shape) for a in args],
        out_specs=(_full((n_blocks, 128)),) * 2,
        out_shape=(jax.ShapeDtypeStruct((n_blocks, 128), BF16),) * 2,
        compiler_params=_cparams(("arbitrary",)), name="cmp_finish",
    )(hk, hv, *args)


def _softmax_rows(s):
    m = jnp.max(s, -1, keepdims=True)
    m = jnp.where(jnp.isfinite(m), m, 0.0)
    e = jnp.exp(s - m)
    return e / jnp.maximum(jnp.sum(e, -1, keepdims=True), 1e-30)


def _knock_out_top(sc, blkf, n_pick):
    for _ in range(n_pick):
        m = jnp.max(sc, 0, keepdims=True)
        idx = jnp.min(jnp.where(sc == m, blkf, 1e9), 0, keepdims=True)
        sc = jnp.where(blkf == idx, -jnp.inf, sc)
    return sc


def _nsa_prompt_kernel(qun_ref, qrot_ref, sm_ref, kcc_ref, vcc_ref, att_ref, wsel_ref, eall_ref, o_ref,
                       m_scr, l_scr, acc_scr, sel_scr, bias_scr, p_scr, imp_scr, ocmp_scr, biasw_scr, pw_scr, owin_scr):
    i = pl.program_id(0)
    s0 = i * Q_BLOCK
    ncp = kcc_ref.shape[0]
    nb = wsel_ref.shape[0]
    kt = min(SEL_KEY_TILE, att_ref.shape[0])
    n_top = min(N_SEL, nb)

    def head(ref, h):
        return ref[:, h * LANES:(h + 1) * LANES]

    def stacked(ref, h0, n):
        return jnp.concatenate([head(ref, h0 + j) for j in range(n)], axis=0)

    def rows(x, j):
        return x[j * Q_BLOCK:(j + 1) * Q_BLOCK]

    rg = SOFTMAX_ROW_GROUP

    def compress_and_select(nc, nbl):
        colc = lax.broadcasted_iota(jnp.int32, (Q_BLOCK, nc), 1)
        rowq = lax.broadcasted_iota(jnp.int32, (Q_BLOCK, nc), 0) + s0
        bias_scr[:, 0:nc] = jnp.where(colc * STRIDE + (L_CMP - 1) <= rowq, 0.0, -jnp.inf)
        s_all = _dot_nt(stacked(qun_ref, 0, N_HEADS), kcc_ref[0:nc, :])
        for kv in range(N_KV):
            for r in range(Q_BLOCK // rg):
                bias = bias_scr[r * rg:(r + 1) * rg, 0:nc]
                imp = None
                for g in range(GQA):
                    r0 = (kv * GQA + g) * Q_BLOCK + r * rg
                    p = _softmax_rows(s_all[r0:r0 + rg] + bias)
                    p_scr[r0:r0 + rg, 0:nc] = p.astype(BF16)
                    imp = p if imp is None else imp + p
                imp_scr[kv, r * rg:(r + 1) * rg, 0:nc] = imp
        ocmp_scr[...] = _dot(p_scr[:, 0:nc], vcc_ref[0:nc, :])

        blk = lax.broadcasted_iota(jnp.int32, (nbl, Q_BLOCK), 0)
        qpl = lax.broadcasted_iota(jnp.int32, (nbl, Q_BLOCK), 1) + s0
        cur = qpl >> 6
        valid = blk * L_SEL <= qpl
        forced = (blk == 0) | ((blk <= cur) & (blk > cur - N_LOCAL))
        blkf = blk.astype(F32)
        wsel = wsel_ref[0:nbl, 0:nc]
        scs = []
        for kv in range(N_KV):
            hi, mid, lo = _split3(imp_scr[kv, :, 0:nc])
            sc = _dot_nt(wsel, hi) + _dot_nt(wsel, mid) + _dot_nt(wsel, lo)
            scs.append(jnp.where(forced, -jnp.inf, jnp.where(valid, sc, -jnp.inf)))
        sc2 = _knock_out_top(jnp.concatenate(scs, axis=1), jnp.concatenate([blkf, blkf], axis=1),
                             n_top - 1 - N_LOCAL)
        nbl_pad = min(-(-nbl // LANES) * LANES, nb)
        for kv in range(N_KV):
            picked = valid & (sc2[:, kv * Q_BLOCK:(kv + 1) * Q_BLOCK] == -jnp.inf)
            bias_t = jnp.where(picked, 0.0, NEG)
            if nbl_pad > nbl:
                bias_t = jnp.concatenate([bias_t, jnp.full((nbl_pad - nbl, Q_BLOCK), NEG, F32)], axis=0)
            sel_scr[kv * Q_BLOCK:(kv + 1) * Q_BLOCK, 0:nbl_pad] = bias_t.T.astype(BF16)
        if nbl_pad < nb:
            sel_scr[:, nbl_pad:nb] = jnp.full((N_KV * Q_BLOCK, nb - nbl_pad), NEG, BF16)

    n_qblk = pl.num_programs(0)
    n_var = 4 if ncp % (4 * LANES) == 0 else 1
    for v in range(n_var):
        nc_v = (v + 1) * ncp // n_var
        nb_v = (v + 1) * nb // n_var

        @pl.when((i * n_var) // n_qblk == v)
        def _():
            compress_and_select(nc_v, nb_v)

    o_cmp = [rows(ocmp_scr[...], h) for h in range(N_HEADS)]

    wk = WINDOW + Q_BLOCK
    w0 = pl.multiple_of(jnp.maximum(s0 - WINDOW, 0), Q_BLOCK)
    kw = att_ref[pl.ds(w0, wk), 256:384]
    vw = att_ref[pl.ds(w0, wk), 384:512]
    kposw = lax.broadcasted_iota(jnp.int32, (Q_BLOCK, wk), 1) + w0
    roww = lax.broadcasted_iota(jnp.int32, (Q_BLOCK, wk), 0) + s0
    okw = (kposw <= roww) & (kposw > roww - WINDOW)
    biasw_scr[...] = jnp.where(okw, 0.0, -jnp.inf)
    s_all = _dot_nt(stacked(qrot_ref, 0, N_HEADS), kw)
    for h in range(N_HEADS):
        for r in range(Q_BLOCK // rg):
            r0 = h * Q_BLOCK + r * rg
            p = _softmax_rows(s_all[r0:r0 + rg] + biasw_scr[r * rg:(r + 1) * rg, :])
            pw_scr[r0:r0 + rg, :] = p.astype(BF16)
    owin_scr[...] = _dot(pw_scr[...], vw)

    o_sel, o_win = [], []
    rowk = lax.broadcasted_iota(jnp.int32, (Q_BLOCK, kt), 0) + s0
    lanek = lax.broadcasted_iota(jnp.int32, (Q_BLOCK, kt), 1)
    n_tiles = (s0 + Q_BLOCK - 1) // kt + 1
    nch = kt // LANES
    bgrp = min(LANES, nb)

    def lane_chunks(x):
        return [x[:, c * LANES:(c + 1) * LANES] for c in range(nch)]

    for h in range(N_HEADS):
        m_scr[h] = jnp.full((Q_BLOCK, LANES), NEG, F32)
        l_scr[h] = jnp.zeros((Q_BLOCK, LANES), F32)
        acc_scr[h] = jnp.zeros((Q_BLOCK, LANES), F32)
    q8 = stacked(qrot_ref, 0, N_HEADS)

    def tile_body(t, carry, diagonal):
        k0 = pl.multiple_of(t * kt, kt)
        ks = att_ref[pl.ds(k0, kt), 0:128]
        vs = att_ref[pl.ds(k0, kt), 128:256]
        b0 = pl.multiple_of(((t * (kt // L_SEL)) // bgrp) * bgrp, bgrp)
        bias2 = _dot(sel_scr[:, pl.ds(b0, bgrp)], eall_ref[pl.ds(b0, bgrp), pl.ds(k0, kt)])
        biases = [rows(bias2, kv) for kv in range(N_KV)]
        if diagonal:
            biases = [jnp.where(lanek + k0 <= rowk, b, NEG) for b in biases]
        s8 = _dot_nt(q8, ks)
        ps, alphas = [], []
        for h in range(N_HEADS):
            s = rows(s8, h) + biases[h // GQA]
            m_old = m_scr[h]
            m_new = jnp.maximum(m_old, jnp.max(functools.reduce(jnp.maximum, lane_chunks(s)), -1, keepdims=True))
            alpha = jnp.exp(m_old - m_new)
            p = jnp.exp(s - jnp.tile(m_new, (1, nch)))
            l_scr[h] = alpha * l_scr[h] + functools.reduce(jnp.add, lane_chunks(p))
            m_scr[h] = m_new
            ps.append(p.astype(BF16))
            alphas.append(alpha)
        pv = _dot(jnp.concatenate(ps, axis=0), vs)
        for h in range(N_HEADS):
            acc_scr[h] = alphas[h] * acc_scr[h] + rows(pv, h)
        return carry

    lax.fori_loop(0, n_tiles - 1, functools.partial(tile_body, diagonal=False), 0)
    tile_body(n_tiles - 1, 0, diagonal=True)
    for h in range(N_HEADS):
        o_sel.append(acc_scr[h] / jnp.sum(l_scr[h], -1, keepdims=True))

    o_win = [rows(owin_scr[...], h) for h in range(N_HEADS)]

    sm = sm_ref[...]
    lane = lax.broadcasted_iota(jnp.int32, (Q_BLOCK, LANES), 1)
    chunks = []
    for c in range(N_HEADS // 2):
        parts = []
        for h in (2 * c, 2 * c + 1):
            kv = h // GQA
            gc = M_HEADS + 3 * h
            a = sm[:, gc:gc + 1] * o_cmp[h] + sm[:, gc + 1:gc + 2] * o_sel[h] + sm[:, gc + 2:gc + 3] * o_win[h]
            a = jnp.where((lane >= kv * HEAD_DIM) & (lane < (kv + 1) * HEAD_DIM), a, 0.0)
            if kv != h % 2:
                a = pltpu.roll(a, HEAD_DIM, 1)
            parts.append(a)
        chunks.append(parts[0] + parts[1])
    o_ref[...] = jnp.concatenate(chunks, axis=1)


def _nsa_prompt(qun, qrot, sm, kcc, vcc, att, wsel, eall):
    t = qun.shape[0]
    ncp = kcc.shape[0]
    wk = WINDOW + Q_BLOCK
    row = lambda w: pl.BlockSpec((Q_BLOCK, w), lambda i: (i, 0))
    return pl.pallas_call(
        _nsa_prompt_kernel, grid=(t // Q_BLOCK,),
        in_specs=[row(1024), row(1024), row(128), _resident(kcc.shape), _resident(vcc.shape),
                  _resident(att.shape), _resident(wsel.shape), _resident(eall.shape)],
        out_specs=row(512),
        out_shape=jax.ShapeDtypeStruct((t, 512), F32),
        scratch_shapes=[pltpu.VMEM((N_HEADS, Q_BLOCK, LANES), F32)] * 3 + [
            pltpu.VMEM((N_KV * Q_BLOCK, wsel.shape[0]), BF16),
            pltpu.VMEM((Q_BLOCK, ncp), F32),
            pltpu.VMEM((N_HEADS * Q_BLOCK, ncp), BF16),
            pltpu.VMEM((N_KV, Q_BLOCK, ncp), F32),
            pltpu.VMEM((N_HEADS * Q_BLOCK, LANES), F32),
            pltpu.VMEM((Q_BLOCK, wk), F32),
            pltpu.VMEM((N_HEADS * Q_BLOCK, wk), BF16),
            pltpu.VMEM((N_HEADS * Q_BLOCK, LANES), F32)],
        compiler_params=_cparams(("arbitrary",)), name="nsa_prompt",
    )(qun, qrot, sm, kcc, vcc, att, wsel, eall)


def _out_kernel(x_ref, m_ref, a_ref, ang_ref, wom_ref, woa_ref, l1g_ref, l1b_ref, w1_ref, w2_ref,
                l2g_ref, l2b_ref, y_ref):
    a = a_ref[...]
    an = a * lax.rsqrt(jnp.mean(a * a, -1, keepdims=True) + RMS_EPS) * ang_ref[...]
    mix = _dot(m_ref[...], wom_ref[...]) + _dot(an.astype(BF16), woa_ref[...])
    h = _layer_norm(DEEPNORM_ALPHA * x_ref[...] + mix, l1g_ref[...], l1b_ref[...])
    hb = h.astype(BF16)
    f = jnp.zeros(h.shape, F32)
    for c in range(D_FF // 1024):
        cs = slice(c * 1024, (c + 1) * 1024)
        u = jnp.maximum(_dot(hb, w1_ref[:, cs]), 0.0)
        f = f + _dot((u * u).astype(BF16), w2_ref[cs, :])
    y_ref[...] = _layer_norm(DEEPNORM_ALPHA * h + f, l2g_ref[...], l2b_ref[...])


def _output_block(x, m_out, a_out, att, ow, tm):
    t = x.shape[0]
    row = lambda w: pl.BlockSpec((tm, w), lambda i: (i, 0))
    names = ("l1g", "l1b", "w1", "w2", "l2g", "l2b")
    args = [att["ang"], ow["wom"], att["woa"]] + [ow[k] for k in names]
    return pl.pallas_call(
        _out_kernel, grid=(t // tm,),
        in_specs=[row(D_MODEL), row(512), row(a_out.shape[1])] + [_resident(a.shape) for a in args],
        out_specs=row(D_MODEL),
        out_shape=jax.ShapeDtypeStruct((t, D_MODEL), F32),
        compiler_params=_cparams(("arbitrary",)), name="out_block",
    )(x, m_out, a_out, *args)


def _sample_cmp_kernel(pt_ref, cache_ref, new_ref, q_ref, wk_ref, wv_ref,
                       pek, w1k, b1k, w2k, b2k, pev, w1v, b1v, w2v, b2v, wsel_ref,
                       ocmp_ref, score_ref, buf, xs, sem, *, n_pages, past_len):
    b = pl.program_id(0)
    nb = pl.num_programs(0)
    n_blocks = n_pages * PAGE_SIZE // STRIDE

    def page_copy(bb, slot, pg):
        return pltpu.make_async_copy(cache_ref.at[pt_ref[bb * n_pages + pg], pl.ds(0, 2)],
                                     buf.at[slot, pg], sem.at[slot])

    def start_all(bb, slot):
        def body(pg, carry):
            page_copy(bb, slot, pg).start()
            return carry
        lax.fori_loop(0, n_pages, body, 0)

    def wait_all(bb, slot):
        def body(pg, carry):
            page_copy(bb, slot, pg).wait()
            return carry
        lax.fori_loop(0, n_pages, body, 0)

    slot = b % 2

    @pl.when(b == 0)
    def _():
        start_all(0, 0)

    @pl.when(b + 1 < nb)
    def _():
        start_all(b + 1, 1 - slot)

    wait_all(b, slot)

    def to_rows(pg, carry):
        r0 = pl.multiple_of(pg * PAGE_SIZE, PAGE_SIZE)
        for c in range(2):
            xs[c, pl.ds(r0, PAGE_SIZE), :] = buf[slot, pg, c].astype(BF16).T.astype(F32)
        return carry
    lax.fori_loop(0, n_pages, to_rows, 0, unroll=math.gcd(n_pages, 8))

    new = jnp.broadcast_to(new_ref[0], (8, 512))
    rowi = lax.broadcasted_iota(jnp.int32, (n_blocks, 256), 0)
    toks = []
    for c, w_ref, pe, w1f, b1, w2, b2 in ((0, wk_ref, pek, w1k, b1k, w2k, b2k),
                                          (1, wv_ref, pev, w1v, b1v, w2v, b2v)):
        h = _stride_block_proj(xs.at[c], n_blocks, w_ref)
        h_new = _dot(new[:, c * LANES:(c + 1) * LANES].astype(BF16), w_ref[0:LANES, 256:512])[0:1]
        h1_next = jnp.where(rowi == n_blocks - 1, h_new, pltpu.roll(h[:, 256:512], n_blocks - 1, 0))
        toks.append(_cmp_mlp(h[:, 0:256], h1_next, _cmp_const(pe, w1f, b1), w2, b2).astype(BF16))
    kcc, vcc = toks

    q = q_ref[0]
    zq = jnp.zeros_like(q)
    rowh = lax.broadcasted_iota(jnp.int32, (N_HEADS, LANES), 0)
    qpad = jnp.where(rowh < GQA, jnp.concatenate([q, zq], 1), jnp.concatenate([zq, q], 1))
    s = _dot_nt(qpad, kcc)
    colc = lax.broadcasted_iota(jnp.int32, s.shape, 1)
    p = _softmax_rows(jnp.where(colc * STRIDE + (L_CMP - 1) <= past_len, s, -jnp.inf))
    o = _dot(p.astype(BF16), vcc)
    ocmp_ref[0] = jnp.where(rowh[:, 0:HEAD_DIM] < GQA, o[:, 0:HEAD_DIM], o[:, HEAD_DIM:])
    imp0 = p[0:1] + p[1:2] + p[2:3] + p[3:4]
    imp1 = p[4:5] + p[5:6] + p[6:7] + p[7:8]
    rowp = lax.broadcasted_iota(jnp.int32, p.shape, 0)
    imp = jnp.where(rowp == 0, imp0, jnp.where(rowp == 1, imp1, 0.0))
    score_ref[0] = _dot3(imp, wsel_ref[...])


def _sample_cmp(page_table, cache4, kv4_s, q8, cw, wsel_s, past_len):
    db, n_pages = page_table.shape
    names = ("pek", "w1k", "b1k", "w2k", "b2k", "pev", "w1v", "b1v", "w2v", "b2v")
    args = [cw[k] for k in names]
    nbp = wsel_s.shape[1]
    full = lambda a: pl.BlockSpec(a.shape, lambda b, pt: (0,) * a.ndim)
    grid_spec = pltpu.PrefetchScalarGridSpec(
        num_scalar_prefetch=1, grid=(db,),
        in_specs=[pl.BlockSpec(memory_space=pl.ANY),
                  pl.BlockSpec((1, 1, 512), lambda b, pt: (b, 0, 0)),
                  pl.BlockSpec((1, N_HEADS, HEAD_DIM), lambda b, pt: (b, 0, 0)),
                  full(cw["wk"]), full(cw["wv"])] + [full(a) for a in args] + [full(wsel_s)],
        out_specs=(pl.BlockSpec((1, N_HEADS, HEAD_DIM), lambda b, pt: (b, 0, 0)),
                   pl.BlockSpec((1, 8, nbp), lambda b, pt: (b, 0, 0))),
        scratch_shapes=[pltpu.VMEM((2, n_pages, 2, LANES, PAGE_SIZE), F32),
                        pltpu.VMEM((2, n_pages * PAGE_SIZE, LANES), F32),
                        pltpu.SemaphoreType.DMA((2,))],
    )
    return pl.pallas_call(
        functools.partial(_sample_cmp_kernel, n_pages=n_pages, past_len=past_len),
        grid_spec=grid_spec,
        out_shape=(jax.ShapeDtypeStruct((db, N_HEADS, HEAD_DIM), F32), jax.ShapeDtypeStruct((db, 8, nbp), F32)),
        compiler_params=_cparams(("arbitrary",)), name="sample_cmp",
    )(page_table.reshape(-1), cache4, kv4_s.reshape(db, 1, 512), q8, cw["wk"], cw["wv"], *args, wsel_s)


def _sample_topk_kernel(score_ref, idx_ref, *, n_slc, qp):
    sc = score_ref[...]
    blk = lax.broadcasted_iota(jnp.int32, sc.shape, 1)
    cur = qp // L_SEL
    valid = (blk * L_SEL <= qp) & (blk < n_slc)
    forced = (blk == 0) | ((blk <= cur) & (blk > cur - N_LOCAL))
    sc = jnp.where(forced & (blk < n_slc), jnp.inf, jnp.where(valid, sc, -jnp.inf))
    real = blk < n_slc
    blkf = blk.astype(F32)
    taken = jnp.where(real, 0.0, 1.0)
    out = jnp.zeros(idx_ref.shape, F32)
    lane = lax.broadcasted_iota(jnp.int32, idx_ref.shape, 1)
    for it in range(min(N_SEL, n_slc)):
        m = jnp.max(jnp.where(taken == 0.0, sc, -jnp.inf), -1, keepdims=True)
        cand = (sc == m) & (taken == 0.0)
        idx = jnp.min(jnp.where(cand, blkf, 1e9), -1, keepdims=True)
        pick = blkf == idx
        taken = jnp.where(pick, 1.0, taken)
        out = jnp.where(lane == it, idx, out)
    idx_ref[...] = out.astype(jnp.int32)


def _sample_topk(scores, n_slc, qp):
    rows, nbp = scores.shape
    return pl.pallas_call(
        functools.partial(_sample_topk_kernel, n_slc=n_slc, qp=qp), grid=(1,),
        in_specs=[_full(scores.shape)], out_specs=_full((rows, LANES)),
        out_shape=jax.ShapeDtypeStruct((rows, LANES), jnp.int32),
        compiler_params=_cparams(("arbitrary",)), name="sample_topk",
    )(scores)


def _sample_sel_kernel(pt_ref, idx_ref, cache_ref, q_ref, new_ref, osel_ref, buf, sem, *, n_pages, past_len, n_top):
    b = pl.program_id(0)
    nb = pl.num_programs(0)
    per_page = PAGE_SIZE // L_SEL
    n_past_blocks = past_len // L_SEL
    n_cp = N_KV * n_top

    def blk_copy(bb, slot, j):
        kv = j // n_top
        n = j - kv * n_top
        blk = jnp.minimum(idx_ref[bb * n_cp + j], n_past_blocks - 1)
        page = pt_ref[bb * n_pages + blk // per_page]
        return pltpu.make_async_copy(
            cache_ref.at[page, pl.ds(2, 2), pl.ds(pl.multiple_of(kv * HEAD_DIM, HEAD_DIM), HEAD_DIM), :],
            buf.at[slot, kv, :, :, pl.ds(pl.multiple_of(n * PAGE_SIZE, PAGE_SIZE), PAGE_SIZE)], sem.at[slot])

    def start_all(bb, slot):
        def body(j, c):
            blk_copy(bb, slot, j).start()
            return c
        lax.fori_loop(0, n_cp, body, 0)

    def wait_all(bb, slot):
        def body(j, c):
            blk_copy(bb, slot, j).wait()
            return c
        lax.fori_loop(0, n_cp, body, 0)

    slot = b % 2

    @pl.when(b == 0)
    def _():
        start_all(0, 0)

    @pl.when(b + 1 < nb)
    def _():
        start_all(b + 1, 1 - slot)

    wait_all(b, slot)

    q = q_ref[0]
    new = jnp.broadcast_to(new_ref[0], (8, 512))
    nk = n_top * PAGE_SIZE
    lane = lax.broadcasted_iota(jnp.int32, (N_HEADS, nk), 1)
    outs = []
    for kv in range(N_KV):
        ks_new = new[:, 256 + kv * HEAD_DIM:256 + (kv + 1) * HEAD_DIM].astype(BF16)
        vs_new = new[:, 384 + kv * HEAD_DIM:384 + (kv + 1) * HEAD_DIM].astype(BF16).astype(F32)
        s_new = _dot_nt(q, ks_new)[:, 0:1]
        kt = buf[slot, kv, 0].astype(BF16)
        vt = buf[slot, kv, 1].astype(BF16)
        blk_of_lane = jnp.zeros((N_HEADS, nk), jnp.int32)
        for n in range(n_top):
            blk_of_lane = jnp.where((lane >> 7) == n, idx_ref[b * n_cp + kv * n_top + n], blk_of_lane)
        in_block = ((lane & (PAGE_SIZE - 1)) >> 6) == (blk_of_lane & (per_page - 1))
        ok = in_block & (blk_of_lane < n_past_blocks)
        s = jnp.where(ok, _dot(q, kt), -jnp.inf)
        m = jnp.maximum(jnp.max(s, -1, keepdims=True), s_new)
        e = jnp.exp(s - m)
        e_new = jnp.exp(s_new - m)
        denom = jnp.sum(e, -1, keepdims=True) + e_new
        o = _dot_nt((e / denom).astype(BF16), vt) + (e_new / denom).astype(BF16).astype(F32) * vs_new
        outs.append(o)
    row = lax.broadcasted_iota(jnp.int32, (N_HEADS, HEAD_DIM), 0)
    osel_ref[0] = jnp.where(row < GQA, outs[0], outs[1])


def _sample_sel(page_table, idx, cache4, qrot8, kv4_s, past_len, n_top):
    db, n_pages = page_table.shape
    assert PAGE_SIZE // L_SEL == 2
    hb = pl.BlockSpec((1, N_HEADS, HEAD_DIM), lambda b, pt, ix: (b, 0, 0))
    grid_spec = pltpu.PrefetchScalarGridSpec(
        num_scalar_prefetch=2, grid=(db,),
        in_specs=[pl.BlockSpec(memory_space=pl.ANY), hb,
                  pl.BlockSpec((1, 1, 512), lambda b, pt, ix: (b, 0, 0))],
        out_specs=hb,
        scratch_shapes=[pltpu.VMEM((2, N_KV, 2, HEAD_DIM, n_top * PAGE_SIZE), F32), pltpu.SemaphoreType.DMA((2,))],
    )
    return pl.pallas_call(
        functools.partial(_sample_sel_kernel, n_pages=n_pages, past_len=past_len, n_top=n_top),
        grid_spec=grid_spec,
        out_shape=jax.ShapeDtypeStruct((db, N_HEADS, HEAD_DIM), F32),
        compiler_params=_cparams(("arbitrary",)), name="sample_sel",
    )(page_table.reshape(-1), idx.reshape(-1), cache4, qrot8, kv4_s.reshape(db, 1, 512))


def _sample_win_kernel(st_ref, newrow_ref, newall_ref, q_ref, g_ref, ocmp_ref, osel_ref, a_ref, wout_ref):
    i = pl.program_id(0)
    tb, wb = st_ref.shape[0], st_ref.shape[3]
    db = newall_ref.shape[0]
    new_t = newall_ref[...].T
    rowb = lax.broadcasted_iota(jnp.int32, (db, LANES), 0)
    lane_w = lax.broadcasted_iota(jnp.int32, (HEAD_DIM, wb), 1)
    lane_s = lax.broadcasted_iota(jnp.int32, (N_HEADS, wb), 1)
    rowh = lax.broadcasted_iota(jnp.int32, (N_HEADS, HEAD_DIM), 0)
    for j in range(tb):
        pick = jnp.where(rowb == i * tb + j, 1.0, 0.0).astype(BF16)
        col = _dot3(new_t, pick)
        for ckv in range(2 * N_KV):
            colw = jnp.tile(col[ckv * HEAD_DIM:(ckv + 1) * HEAD_DIM], (1, wb // LANES))
            wout_ref[j, ckv] = jnp.where(lane_w == wb - 1, colw, pltpu.roll(st_ref[j, ckv], wb - 1, 1))
        new = jnp.broadcast_to(newrow_ref[j], (8, 256))
        q = q_ref[j]
        outs = []
        for kv in range(N_KV):
            kt = st_ref[j, kv].astype(BF16)
            vt = st_ref[j, N_KV + kv].astype(BF16)
            kw_new = new[:, kv * HEAD_DIM:(kv + 1) * HEAD_DIM].astype(BF16)
            vw_new = new[:, 128 + kv * HEAD_DIM:128 + (kv + 1) * HEAD_DIM].astype(BF16).astype(F32)
            s_new = _dot_nt(q, kw_new)[:, 0:1]
            s = jnp.where(lane_s >= 1, _dot(q, kt), -jnp.inf)
            m = jnp.maximum(jnp.max(s, -1, keepdims=True), s_new)
            e = jnp.exp(s - m)
            e_new = jnp.exp(s_new - m)
            denom = jnp.sum(e, -1, keepdims=True) + e_new
            outs.append(_dot_nt((e / denom).astype(BF16), vt) + (e_new / denom).astype(BF16).astype(F32) * vw_new)
        o_win = jnp.where(rowh < GQA, outs[0], outs[1])
        g = g_ref[j]
        a_ref[j] = g[:, 0:1] * ocmp_ref[j] + g[:, 1:2] * osel_ref[j] + g[:, 2:3] * o_win


def _sample_win(state4, win_new, qrot8, gates8, ocmp, osel, tb):
    db, _, _, wb = state4.shape
    b3 = lambda s: pl.BlockSpec((tb,) + s, lambda i: (i, 0, 0))
    b4 = pl.BlockSpec((tb, 2 * N_KV, HEAD_DIM, wb), lambda i: (i, 0, 0, 0))
    hd = (N_HEADS, HEAD_DIM)
    return pl.pallas_call(
        _sample_win_kernel, grid=(db // tb,),
        in_specs=[b4, b3((1, 256)), _full(win_new.shape), b3(hd), b3((8, 128)), b3(hd), b3(hd)],
        out_specs=(b3(hd), b4),
        out_shape=(jax.ShapeDtypeStruct((db,) + hd, F32), jax.ShapeDtypeStruct(state4.shape, F32)),
        compiler_params=_cparams(("arbitrary",)), name="sample_win",
    )(state4, win_new.reshape(db, 1, 256), win_new, qrot8, gates8, ocmp, osel)


def _rope_tables(pos):
    half = ROT_DIM // 2
    inv = ROPE_THETA ** (-jnp.arange(half, dtype=F32) * 2.0 / ROT_DIM)
    ang = pos.astype(F32)[:, None] * inv[None, :]
    cos, sin = jnp.cos(ang), jnp.sin(ang)
    n = pos.shape[0]
    rest = HEAD_DIM - ROT_DIM
    one, zero, z8 = jnp.ones((n, rest), F32), jnp.zeros((n, rest), F32), jnp.zeros((n, half), F32)
    c = jnp.concatenate([cos, cos, one], 1)
    s1 = jnp.concatenate([z8, sin, zero], 1)
    s2 = jnp.concatenate([-sin, z8, zero], 1)
    return tuple(jnp.tile(a, (1, 2)) for a in (c, s1, s2))


def _proj_weights(w_in):
    sizes = (SSM_WIDTH, CONV_DIM, M_HEADS, ATT_WIDTH, 6 * N_KV * HEAD_DIM, 3 * N_HEADS)
    parts, start = [], 0
    for s in sizes:
        parts.append(w_in[:, start:start + s])
        start += s
    wz, wxbc, wdt, wq, wkv, wg = parts
    zeros = jnp.zeros((D_MODEL, HEAD_DIM), w_in.dtype)
    chunks = []
    for h in range(N_HEADS):
        wh = wq[:, h * HEAD_DIM:(h + 1) * HEAD_DIM]
        chunks += [wh, zeros] if h // GQA == 0 else [zeros, wh]
    wq_pad = jnp.concatenate(chunks, 1)
    wsm = jnp.concatenate([wdt, wg, jnp.zeros((D_MODEL, LANES - M_HEADS - 3 * N_HEADS), w_in.dtype)], 1)
    wz, wxbc, wq, wq_pad, wkv, wsm = (w.astype(BF16) for w in (wz, wxbc, wq, wq_pad, wkv, wsm))
    return (wz, wxbc, wq_pad, wkv, wsm), (wz, wxbc, wq, wkv, wsm)


def _cmp_weights(pe_cmp, w1_cmp, b1_cmp, w2_cmp, b2_cmp):
    out = {}
    for i, nm in enumerate("kv"):
        w1 = w1_cmp[i].reshape(2, STRIDE, HEAD_DIM, CMP_HIDDEN)
        wbig = jnp.zeros((STRIDE, LANES, 512), F32)
        for r in range(2):
            for kv in range(N_KV):
                c0 = r * 256 + kv * CMP_HIDDEN
                wbig = wbig.at[:, kv * HEAD_DIM:(kv + 1) * HEAD_DIM, c0:c0 + CMP_HIDDEN].set(w1[r])
        out["w" + nm] = wbig.reshape(STRIDE * LANES, 512).astype(BF16)
        out["pe" + nm] = jnp.broadcast_to(pe_cmp[i].reshape(1, L_CMP * HEAD_DIM), (8, L_CMP * HEAD_DIM)).astype(BF16)
        out["w1" + nm] = w1_cmp[i].reshape(L_CMP * HEAD_DIM, CMP_HIDDEN).astype(BF16)
        out["b1" + nm] = jnp.tile(b1_cmp[i].reshape(1, CMP_HIDDEN), (1, 2))
        w2 = jnp.zeros((2 * CMP_HIDDEN, LANES), F32)
        for kv in range(N_KV):
            w2 = w2.at[kv * CMP_HIDDEN:(kv + 1) * CMP_HIDDEN, kv * HEAD_DIM:(kv + 1) * HEAD_DIM].set(w2_cmp[i])
        out["w2" + nm] = w2.astype(BF16)
        out["b2" + nm] = jnp.tile(b2_cmp[i].reshape(1, HEAD_DIM), (1, 2))
    return out


def _select_weights(n_blocks, n_cmp_padded):
    b = jnp.arange(n_blocks)[:, None]
    c = jnp.arange(n_cmp_padded)[None, :]
    k = c - 4 * b + 1
    w = jnp.where((k == 0) | (k == 4), 16.0, jnp.where((k >= 1) & (k <= 3), 32.0, 0.0))
    return w.astype(BF16)


def _pad_row(v, n=LANES):
    v = v.reshape(1, -1).astype(F32)
    return jnp.concatenate([v, jnp.zeros((1, n - v.shape[1]), F32)], 1)


def kernel(x_prompt, x_sample, cache_kv, state_win, state_ssm, state_conv, page_table, w_in, conv_w, conv_b,
           dt_bias, a_log, d_skip, ssm_norm_g, pe_cmp, w1_cmp, b1_cmp, w2_cmp, b2_cmp, att_norm_g, w_out,
           ln1_g, ln1_b, w_ff1, w_ff2, ln2_g, ln2_b):
    bsz, seq, _ = x_prompt.shape
    db, dseq, _ = x_sample.shape
    n_pages = page_table.shape[1]
    past_len = n_pages * PAGE_SIZE
    wb = state_win.shape[1]
    assert bsz == 1 and dseq == 1 and wb == WINDOW and past_len % L_SEL == 0 and seq % SEL_KEY_TILE == 0

    pw, pw_flat = _proj_weights(w_in)
    cw = _cmp_weights(pe_cmp, w1_cmp, b1_cmp, w2_cmp, b2_cmp)
    cb = conv_b.reshape(1, CONV_DIM)
    dtb, alog = _pad_row(dt_bias), _pad_row(a_log)
    dsk = jnp.repeat(d_skip, HEAD_DIM).reshape(1, SSM_WIDTH)
    ng = ssm_norm_g.reshape(1, SSM_WIDTH)
    ow = dict(wom=w_out[0:SSM_WIDTH].astype(BF16), l1g=ln1_g.reshape(1, -1),
              l1b=ln1_b.reshape(1, -1), w1=w_ff1.astype(BF16), w2=w_ff2.astype(BF16),
              l2g=ln2_g.reshape(1, -1), l2b=ln2_b.reshape(1, -1))
    woa = w_out[SSM_WIDTH:D_MODEL]
    ang = att_norm_g.reshape(ATT_WIDTH, 1)
    att_flat = dict(ang=ang.reshape(1, -1), woa=woa.astype(BF16))

    xp = x_prompt.reshape(seq, D_MODEL)
    tm = min(256, seq)
    z, xbc, qun, qrot, kv4, win, att, sm = _project(
        xp, pw, _rope_tables(jnp.arange(seq, dtype=jnp.int32)), tm, min(WINDOW, seq))
    m_out, conv_p, ssm_p = _mamba_prompt(z, xbc, sm, conv_w, cb, dtb, alog, dsk, ng)
    kcc, vcc = _compress_prompt(kv4, cw)
    n_slc = seq // L_SEL
    wsel = _select_weights(n_slc, seq // STRIDE)
    eall = (jnp.arange(n_slc)[:, None] == (jnp.arange(seq) // L_SEL)[None, :]).astype(BF16)
    a_out = _nsa_prompt(qun, qrot, sm, kcc, vcc, att, wsel, eall)
    y_prompt = _output_block(xp, m_out, a_out, att_flat, ow, tm).reshape(bsz, seq, D_MODEL)
    kv_prompt = kv4.reshape(bsz, seq, 4, N_KV, HEAD_DIM)
    win_prompt = win.reshape(bsz, min(WINDOW, seq), 2, N_KV, HEAD_DIM)
    ssm_prompt = ssm_p.reshape(bsz, M_HEADS, HEAD_DIM, D_STATE)
    conv_prompt = conv_p.reshape(bsz, CONV_W - 1, CONV_DIM)

    xs_ = x_sample.reshape(db, D_MODEL)
    pos_s = jnp.full((db,), past_len, jnp.int32)
    z, xbc, qun, qrot, kv4_s, win_s, _, sm = _project(xs_, pw_flat, _rope_tables(pos_s), db, db)
    m_out_s, conv_s, ssm_s = _mamba_sample(
        z, xbc, sm, jnp.transpose(state_conv, (1, 0, 2)), state_ssm.reshape(db, SSM_WIDTH, D_STATE),
        conv_w, cb, dtb, alog, dsk, ng)
    cache4 = jnp.transpose(cache_kv, (0, 2, 3, 4, 1)).reshape(cache_kv.shape[0], 4, N_KV * HEAD_DIM, PAGE_SIZE)
    state4 = jnp.transpose(state_win, (0, 2, 3, 4, 1)).reshape(db, 2 * N_KV, HEAD_DIM, wb)
    n_slc_s = -(-(past_len + dseq) // L_SEL)
    nbp = -(-n_slc_s // LANES) * LANES
    wsel_s = jnp.transpose(_select_weights(nbp, past_len // STRIDE))
    qun8 = qun.reshape(db, N_HEADS, HEAD_DIM)
    qrot8 = qrot.reshape(db, N_HEADS, HEAD_DIM)
    o_cmp, scores = _sample_cmp(page_table, cache4, kv4_s, qun8, cw, wsel_s, past_len)
    n_top = min(N_SEL, n_slc_s)
    idx = _sample_topk(scores[:, 0:N_KV, :].reshape(db * N_KV, nbp), n_slc_s, past_len)[:, 0:n_top]
    o_sel = _sample_sel(page_table, idx, cache4, qrot8, kv4_s, past_len, n_top)
    gates8 = jnp.pad(sm[:, M_HEADS:M_HEADS + 3 * N_HEADS].reshape(db, N_HEADS, 3), ((0, 0), (0, 0), (0, LANES - 3)))
    a8, win4 = _sample_win(state4, win_s, qrot8, gates8, o_cmp, o_sel, min(8, db))
    y_sample = _output_block(xs_, m_out_s, a8.reshape(db, ATT_WIDTH), att_flat, ow, db).reshape(db, dseq, D_MODEL)
    kv_sample = kv4_s.reshape(db, dseq, 4, N_KV, HEAD_DIM)
    win_sample = jnp.transpose(win4.reshape(db, 2, N_KV, HEAD_DIM, wb), (0, 4, 1, 2, 3))
    ssm_sample = ssm_s.reshape(db, M_HEADS, HEAD_DIM, D_STATE)
    conv_sample = jnp.transpose(conv_s, (1, 0, 2))

    return (y_prompt, y_sample, kv_prompt, win_prompt, ssm_prompt, conv_prompt,
            kv_sample, win_sample, ssm_sample, conv_sample)
```

```python
import functools
import math

import jax
import jax.numpy as jnp
from jax import lax
from jax.experimental import pallas as pl
from jax.experimental.pallas import tpu as pltpu

F32 = jnp.float32
BF16 = jnp.bfloat16

D_MODEL = 1024
HEAD_DIM = 64
SSM_WIDTH = 512
ATT_WIDTH = 512
M_HEADS = 8
M_GROUPS = 2
D_STATE = 128
CONV_W = 4
CONV_DIM = SSM_WIDTH + 2 * M_GROUPS * D_STATE
SSD_CHUNK = 256
RMS_EPS = 1e-6
N_HEADS = 8
N_KV = 2
GQA = N_HEADS // N_KV
ROT_DIM = 16
ROPE_THETA = 500000.0
L_CMP = 32
STRIDE = 16
CMP_HIDDEN = 128
L_SEL = 64
N_SEL = 16
N_LOCAL = 2
WINDOW = 512
Q_BLOCK = 128
ATT_SCALE = HEAD_DIM ** -0.5
D_FF = 4 * D_MODEL
LN_EPS = 1e-5
DEEPNORM_ALPHA = 2.0 ** 0.25
PAGE_SIZE = 128
LANES = 128
NEG = -1e30
SEL_KEY_TILE = 1024
SOFTMAX_ROW_GROUP = 16
SEL_SLOTS = 4
VMEM_LIMIT = 56 * 1024 * 1024


def _cparams(sem):
    return pltpu.CompilerParams(dimension_semantics=sem, vmem_limit_bytes=VMEM_LIMIT)


def _dot(a, b):
    return jnp.dot(a, b, preferred_element_type=F32)


def _dot_nt(a, b):
    return lax.dot_general(a, b, (((1,), (1,)), ((), ())), preferred_element_type=F32)


def _split3(x):
    hi = x.astype(BF16)
    r = x - hi.astype(F32)
    mid = r.astype(BF16)
    lo = (r - mid.astype(F32)).astype(BF16)
    return hi, mid, lo


def _dot3(x, w):
    hi, mid, lo = _split3(x)
    return _dot(hi, w) + _dot(mid, w) + _dot(lo, w)


def _sigmoid(x):
    return 1.0 / (1.0 + jnp.exp(-x))


def _silu(x):
    return x * _sigmoid(x)


def _softplus(x):
    return jnp.maximum(x, 0.0) + jnp.log1p(jnp.exp(-jnp.abs(x)))


def _layer_norm(x, g, b):
    mu = jnp.mean(x, -1, keepdims=True)
    xc = x - mu
    var = jnp.mean(xc * xc, -1, keepdims=True)
    return xc * lax.rsqrt(var + LN_EPS) * g + b


def _rope128(x, c, s1, s2):
    return x * c + pltpu.roll(x, 8, 1) * s1 + pltpu.roll(x, LANES - 8, 1) * s2


def _full(shape):
    nd = len(shape)
    return pl.BlockSpec(shape, lambda *a: (0,) * nd)


def _resident(shape):
    nd = len(shape)
    return pl.BlockSpec(shape, lambda *a: (0,) * nd, pipeline_mode=pl.Buffered(1))


def _proj_kernel(x_ref, wz_ref, wxbc_ref, wq_ref, wkv_ref, wsm_ref, c_ref, s1_ref, s2_ref,
                 z_ref, xbc_ref, qun_ref, qrot_ref, kv4_ref, win_ref, att_ref, sm_ref):
    x = x_ref[...].astype(BF16)
    c, s1, s2 = c_ref[...], s1_ref[...], s2_ref[...]
    z_ref[...] = _dot(x, wz_ref[...])
    xbc_ref[...] = _dot(x, wxbc_ref[...])
    q = _dot(x, wq_ref[...]) * ATT_SCALE
    qun_ref[...] = q.astype(BF16)
    for h in range(q.shape[1] // LANES):
        sl = slice(h * LANES, (h + 1) * LANES)
        qrot_ref[:, sl] = _rope128(q[:, sl], c, s1, s2).astype(BF16)
    kv = _dot(x, wkv_ref[...])
    ks = _rope128(kv[:, 256:384], c, s1, s2)
    kw = _rope128(kv[:, 512:640], c, s1, s2)
    kv4_ref[:, 0:256] = kv[:, 0:256]
    kv4_ref[:, 256:384] = ks
    kv4_ref[:, 384:512] = kv[:, 384:512]
    win_ref[:, 0:128] = kw
    win_ref[:, 128:256] = kv[:, 640:768]
    att_ref[:, 0:128] = ks.astype(BF16)
    att_ref[:, 128:256] = kv[:, 384:512].astype(BF16)
    att_ref[:, 256:384] = kw.astype(BF16)
    att_ref[:, 384:512] = kv[:, 640:768].astype(BF16)
    sm = _dot(x, wsm_ref[...])
    lane = lax.broadcasted_iota(jnp.int32, sm.shape, 1)
    is_gate = (lane >= M_HEADS) & (lane < M_HEADS + 3 * N_HEADS)
    sm_ref[...] = jnp.where(is_gate, _sigmoid(sm), sm)


def _project(x, weights, tables, tm, win_rows):
    t = x.shape[0]
    nt = t // tm
    nwb = win_rows // tm
    wz, wxbc, wq, wkv, wsm = weights
    qw = wq.shape[1]
    row = lambda w: pl.BlockSpec((tm, w), lambda i: (i, 0))
    out_shape = (
        jax.ShapeDtypeStruct((t, 512), F32),
        jax.ShapeDtypeStruct((t, 1024), F32),
        jax.ShapeDtypeStruct((t, qw), BF16),
        jax.ShapeDtypeStruct((t, qw), BF16),
        jax.ShapeDtypeStruct((t, 512), F32),
        jax.ShapeDtypeStruct((win_rows, 256), F32),
        jax.ShapeDtypeStruct((t, 512), BF16),
        jax.ShapeDtypeStruct((t, 128), F32),
    )
    out_specs = (row(512), row(1024), row(qw), row(qw), row(512),
                 pl.BlockSpec((tm, 256), lambda i: (jnp.maximum(i - (nt - nwb), 0), 0)),
                 row(512), row(128))
    in_specs = [row(D_MODEL), _full(wz.shape), _full(wxbc.shape), _full(wq.shape), _full(wkv.shape),
                _full(wsm.shape), row(128), row(128), row(128)]
    return pl.pallas_call(
        _proj_kernel, grid=(nt,), in_specs=in_specs, out_specs=out_specs, out_shape=out_shape,
        compiler_params=_cparams(("arbitrary",)), name="proj",
    )(x, wz, wxbc, wq, wkv, wsm, *tables)


def _mamba_prompt_kernel(z_ref, xbc_ref, sm_ref, cw_ref, cb_ref, dtb_ref, alog_ref, dsk_ref, ng_ref,
                         y_ref, conv_ref, ssm_ref, ubuf, state):
    c = pl.program_id(0)
    nc = pl.num_programs(0)
    L = SSD_CHUNK

    @pl.when(c == 0)
    def _():
        ubuf[0:8, :] = jnp.zeros((8, CONV_DIM), F32)
        state[...] = jnp.zeros(state.shape, F32)

    u = xbc_ref[...]
    ubuf[8:8 + L, :] = u
    acc = u * cw_ref[3:4, :] + cb_ref[...]
    for k in range(CONV_W - 1):
        acc = acc + ubuf[pl.ds(8 - (CONV_W - 1) + k, L), :] * cw_ref[k:k + 1, :]
    tail = ubuf[L:L + 8, :]
    ubuf[0:8, :] = tail
    xa = _silu(acc)
    xs = xa[:, 0:SSM_WIDTH]
    bm = xa[:, SSM_WIDTH:SSM_WIDTH + 256]
    cm = xa[:, SSM_WIDTH + 256:SSM_WIDTH + 512]

    lane = lax.broadcasted_iota(jnp.int32, (L, LANES), 1)
    head_lane = lane < M_HEADS
    dt = jnp.where(head_lane, _softplus(sm_ref[...] + dtb_ref[...]), 0.0)
    a_row = jnp.where(head_lane[0:1], -jnp.exp(alog_ref[...]), 0.0)
    da = dt * a_row
    ri = lax.broadcasted_iota(jnp.int32, (L, L), 0)
    ci = lax.broadcasted_iota(jnp.int32, (L, L), 1)
    causal = ri >= ci
    tril = jnp.where(causal, 1.0, 0.0).astype(BF16)
    da_hi, da_mid, da_lo = _split3(da)
    cum = _dot(tril, da_hi) + _dot(tril, da_mid) + _dot(tril, da_lo)
    cum_t = cum.T
    er = lax.broadcasted_iota(jnp.int32, (LANES, SSM_WIDTH), 0)
    ec = lax.broadcasted_iota(jnp.int32, (LANES, SSM_WIDTH), 1)
    expand = jnp.where(er == (ec >> 6), 1.0, 0.0).astype(BF16)
    dtx = _dot3(dt, expand)
    cumx = _dot3(cum, expand)
    cumx_last = cumx[L - 1:L, :]
    xc = xs * dtx
    xw = xc * jnp.exp(cumx_last - cumx)
    ecx = jnp.exp(cumx)
    dec_last = jnp.exp(cumx_last)

    half = lax.broadcasted_iota(jnp.int32, (L, LANES), 1) < HEAD_DIM
    y_parts = []
    for g in range(M_GROUPS):
        bg = bm[:, g * D_STATE:(g + 1) * D_STATE].astype(BF16)
        cg = cm[:, g * D_STATE:(g + 1) * D_STATE].astype(BF16)
        gs = slice(g * 256, (g + 1) * 256)
        scores = _dot_nt(cg, bg)
        h_prev = state[:, gs]
        y_off = _dot(cg, h_prev.astype(BF16)) * ecx[:, gs]
        st = _dot(bg.T, xw[:, gs].astype(BF16))
        state[:, gs] = h_prev * dec_last[:, gs] + st
        for pr in range(2):
            ha = g * 4 + pr * 2
            xcp = xc[:, ha * HEAD_DIM:(ha + 2) * HEAD_DIM].astype(BF16)
            outs = []
            for hh in (ha, ha + 1):
                seg = cum[:, hh:hh + 1] - cum_t[hh:hh + 1, :]
                dec = jnp.exp(jnp.where(causal, seg, -jnp.inf))
                outs.append(_dot((scores * dec).astype(BF16), xcp))
            y_parts.append(jnp.where(half, outs[0], outs[1]) + y_off[:, pr * 128:(pr + 1) * 128])
    y = jnp.concatenate(y_parts, axis=1) + dsk_ref[...] * xs
    y = y * _silu(z_ref[...])
    outs = []
    for g in range(M_GROUPS):
        yg = y[:, g * 256:(g + 1) * 256]
        outs.append(yg * lax.rsqrt(jnp.mean(yg * yg, -1, keepdims=True) + RMS_EPS))
    y_ref[...] = (jnp.concatenate(outs, axis=1) * ng_ref[...]).astype(BF16)

    @pl.when(c == nc - 1)
    def _():
        conv_ref[...] = ubuf[pl.ds(8 - (CONV_W - 1), CONV_W - 1), :]
        ssm_ref[...] = state[...].T


def _mamba_prompt(z, xbc, sm, cw, cb, dtb, alog, dsk, ng):
    t = z.shape[0]
    L = SSD_CHUNK
    row = lambda w: pl.BlockSpec((L, w), lambda i: (i, 0))
    return pl.pallas_call(
        _mamba_prompt_kernel, grid=(t // L,),
        in_specs=[row(512), row(1024), row(128), _full(cw.shape), _full(cb.shape), _full(dtb.shape),
                  _full(alog.shape), _full(dsk.shape), _full(ng.shape)],
        out_specs=(row(512), _full((CONV_W - 1, CONV_DIM)), _full((SSM_WIDTH, D_STATE))),
        out_shape=(jax.ShapeDtypeStruct((t, 512), BF16),
                   jax.ShapeDtypeStruct((CONV_W - 1, CONV_DIM), F32),
                   jax.ShapeDtypeStruct((SSM_WIDTH, D_STATE), F32)),
        scratch_shapes=[pltpu.VMEM((L + 8, CONV_DIM), F32), pltpu.VMEM((D_STATE, SSM_WIDTH), F32)],
        compiler_params=_cparams(("arbitrary",)), name="mamba_prompt",
    )(z, xbc, sm, cw, cb, dtb, alog, dsk, ng)


def _mamba_sample_kernel(z_ref, xbc_ref, sm_ref, sc_ref, h0_ref, cw_ref, cb_ref, dtb_ref, alog_ref, dsk_ref,
                         ng_ref, y_ref, conv_ref, h_ref, xct, dect, brow, crow, xcrow, decrow, xsrow, yrow):
    b = pl.program_id(0)
    nb = pl.num_programs(0)
    db = z_ref.shape[0]

    @pl.when(b == 0)
    def _():
        u = xbc_ref[...]
        acc = u * cw_ref[3:4, :] + cb_ref[...]
        for k in range(CONV_W - 1):
            acc = acc + sc_ref[k] * cw_ref[k:k + 1, :]
        conv_ref[0] = sc_ref[1]
        conv_ref[1] = sc_ref[2]
        conv_ref[2] = u
        xa = _silu(acc)
        xs = xa[:, 0:SSM_WIDTH]
        lane = lax.broadcasted_iota(jnp.int32, (db, LANES), 1)
        head_lane = lane < M_HEADS
        dt = jnp.where(head_lane, _softplus(sm_ref[...] + dtb_ref[...]), 0.0)
        da = dt * jnp.where(head_lane[0:1], -jnp.exp(alog_ref[...]), 0.0)
        er = lax.broadcasted_iota(jnp.int32, (LANES, SSM_WIDTH), 0)
        ec = lax.broadcasted_iota(jnp.int32, (LANES, SSM_WIDTH), 1)
        expand = jnp.where(er == (ec >> 6), 1.0, 0.0).astype(BF16)
        xc = xs * _dot3(dt, expand)
        dec = jnp.exp(_dot3(da, expand))
        xct[...] = xc.T
        dect[...] = dec.T
        xcrow[...] = xc
        decrow[...] = dec
        xsrow[...] = xs
        brow[...] = xa[:, SSM_WIDTH:SSM_WIDTH + 256]
        crow[...] = xa[:, SSM_WIDTH + 256:SSM_WIDTH + 512]

    r = lax.broadcasted_iota(jnp.int32, (db, LANES), 0)
    pick = jnp.where(r == b, 1.0, 0.0).astype(BF16)
    colx = _dot3(xct[...], pick)
    cold = _dot3(dect[...], pick)
    bv = brow[pl.ds(b, 1), :]
    cv = crow[pl.ds(b, 1), :]
    h0 = h0_ref[0]
    ys = []
    for g in range(M_GROUPS):
        rs = slice(g * 256, (g + 1) * 256)
        bg = bv[:, g * D_STATE:(g + 1) * D_STATE]
        cg = cv[:, g * D_STATE:(g + 1) * D_STATE]
        h_ref[0, rs, :] = cold[rs] * h0[rs] + colx[rs] * bg
        cb_s = jnp.sum(cg * bg, -1, keepdims=True)
        c8 = jnp.broadcast_to(cg, (8, D_STATE)).astype(BF16)
        y_off = _dot_nt(c8, h0[rs].astype(BF16))[0:1]
        ys.append(y_off * decrow[pl.ds(b, 1), rs] + cb_s * xcrow[pl.ds(b, 1), rs])
    yrow[pl.ds(b, 1), :] = jnp.concatenate(ys, axis=1)

    @pl.when(b == nb - 1)
    def _():
        y = (yrow[...] + dsk_ref[...] * xsrow[...]) * _silu(z_ref[...])
        outs = []
        for g in range(M_GROUPS):
            yg = y[:, g * 256:(g + 1) * 256]
            outs.append(yg * lax.rsqrt(jnp.mean(yg * yg, -1, keepdims=True) + RMS_EPS))
        y_ref[...] = (jnp.concatenate(outs, axis=1) * ng_ref[...]).astype(BF16)


def _mamba_sample(z, xbc, sm, sc_t, h0, cw, cb, dtb, alog, dsk, ng):
    db = z.shape[0]
    vm = lambda s: pltpu.VMEM(s, F32)
    return pl.pallas_call(
        _mamba_sample_kernel, grid=(db,),
        in_specs=[_full(z.shape), _full(xbc.shape), _full(sm.shape), _full(sc_t.shape),
                  pl.BlockSpec((1, SSM_WIDTH, D_STATE), lambda b: (b, 0, 0)),
                  _full(cw.shape), _full(cb.shape), _full(dtb.shape), _full(alog.shape), _full(dsk.shape),
                  _full(ng.shape)],
        out_specs=(_full((db, 512)), _full((CONV_W - 1, db, CONV_DIM)),
                   pl.BlockSpec((1, SSM_WIDTH, D_STATE), lambda b: (b, 0, 0))),
        out_shape=(jax.ShapeDtypeStruct((db, 512), BF16),
                   jax.ShapeDtypeStruct((CONV_W - 1, db, CONV_DIM), F32),
                   jax.ShapeDtypeStruct((db, SSM_WIDTH, D_STATE), F32)),
        scratch_shapes=[vm((SSM_WIDTH, db)), vm((SSM_WIDTH, db)), vm((db, 256)), vm((db, 256)),
                        vm((db, SSM_WIDTH)), vm((db, SSM_WIDTH)), vm((db, SSM_WIDTH)), vm((db, SSM_WIDTH))],
        compiler_params=_cparams(("arbitrary",)), name="mamba_sample",
    )(z, xbc, sm, sc_t, h0, cw, cb, dtb, alog, dsk, ng)


def _stride_block_proj(x_ref, n_blocks, w_ref):
    acc = jnp.zeros((n_blocks, 512), F32)
    for j in range(0, STRIDE, 2):
        xj = jnp.concatenate([x_ref[pl.ds(j + d, n_blocks, stride=STRIDE), :].astype(BF16) for d in range(2)], axis=1)
        acc = acc + _dot(xj, w_ref[pl.ds(j * LANES, 2 * LANES), :])
    return acc


def _cmp_proj_kernel(xk_ref, xv_ref, wk_ref, wv_ref, hk_ref, hv_ref):
    n_blocks = xk_ref.shape[0] // STRIDE
    hk_ref[...] = _stride_block_proj(xk_ref, n_blocks, wk_ref)
    hv_ref[...] = _stride_block_proj(xv_ref, n_blocks, wv_ref)


def _cmp_const(pe_ref, w1f_ref, b1_ref):
    c = _dot(pe_ref[...], w1f_ref[...])[0:1]
    return jnp.concatenate([c, c], axis=1) + b1_ref[...]


def _cmp_mlp(h0, h1_next, const, w2_ref, b2_ref):
    pre = h0 + h1_next + const
    return _dot(_silu(pre).astype(BF16), w2_ref[...]) + b2_ref[...]


def _cmp_finish_kernel(hk_ref, hv_ref, pek, w1k, b1k, w2k, b2k, pev, w1v, b1v, w2v, b2v, kcc_ref, vcc_ref):
    n = hk_ref.shape[0]
    for h_ref, pe, w1f, b1, w2, b2, o_ref in ((hk_ref, pek, w1k, b1k, w2k, b2k, kcc_ref),
                                              (hv_ref, pev, w1v, b1v, w2v, b2v, vcc_ref)):
        h = h_ref[...]
        h1_next = pltpu.roll(h[:, 256:512], n - 1, 0)
        o_ref[...] = _cmp_mlp(h[:, 0:256], h1_next, _cmp_const(pe, w1f, b1), w2, b2).astype(BF16)


def _compress_prompt(kv4, cw):
    t = kv4.shape[0]
    rows = min(t, 2048)
    n_blocks = t // STRIDE
    hk, hv = pl.pallas_call(
        _cmp_proj_kernel, grid=(t // rows,),
        in_specs=[pl.BlockSpec((rows, LANES), lambda i: (i, 0)), pl.BlockSpec((rows, LANES), lambda i: (i, 1)),
                  _full(cw["wk"].shape), _full(cw["wv"].shape)],
        out_specs=(pl.BlockSpec((rows // STRIDE, 512), lambda i: (i, 0)),) * 2,
        out_shape=(jax.ShapeDtypeStruct((n_blocks, 512), F32),) * 2,
        compiler_params=_cparams(("arbitrary",)), name="cmp_proj",
    )(kv4, kv4, cw["wk"], cw["wv"])
    names = ("pek", "w1k", "b1k", "w2k", "b2k", "pev", "w1v", "b1v", "w2v", "b2v")
    args = [cw[k] for k in names]
    return pl.pallas_call(
        _cmp_finish_kernel, grid=(1,),
        in_specs=[_full(hk.shape), _full(hv.shape)] + [_full(a.shape) for a in args],
        out_specs=(_full((n_blocks, 128)),) * 2,
        out_shape=(jax.ShapeDtypeStruct((n_blocks, 128), BF16),) * 2,
        compiler_params=_cparams(("arbitrary",)), name="cmp_finish",
    )(hk, hv, *args)


def _softmax_rows(s):
    m = jnp.max(s, -1, keepdims=True)
    m = jnp.where(jnp.isfinite(m), m, 0.0)
    e = jnp.exp(s - m)
    return e / jnp.maximum(jnp.sum(e, -1, keepdims=True), 1e-30)


def _knock_out_top(sc, blkf, n_pick):
    for _ in range(n_pick):
        m = jnp.max(sc, 0, keepdims=True)
        idx = jnp.min(jnp.where(sc == m, blkf, 1e9), 0, keepdims=True)
        sc = jnp.where(blkf == idx, -jnp.inf, sc)
    return sc


def _nsa_prompt_kernel(qun_ref, qrot_ref, sm_ref, kcc_ref, vcc_ref, att_ref, wsel_ref, eall_ref, o_ref,
                       m_scr, l_scr, acc_scr, sel_scr, bias_scr, p_scr, imp_scr, ocmp_scr, biasw_scr, pw_scr, owin_scr):
    i = pl.program_id(0)
    s0 = i * Q_BLOCK
    ncp = kcc_ref.shape[0]
    nb = wsel_ref.shape[0]
    kt = min(SEL_KEY_TILE, att_ref.shape[0])
    n_top = min(N_SEL, nb)

    def head(ref, h):
        return ref[:, h * LANES:(h + 1) * LANES]

    def stacked(ref, h0, n):
        return jnp.concatenate([head(ref, h0 + j) for j in range(n)], axis=0)

    def rows(x, j):
        return x[j * Q_BLOCK:(j + 1) * Q_BLOCK]

    rg = SOFTMAX_ROW_GROUP

    def compress_and_select(nc, nbl):
        colc = lax.broadcasted_iota(jnp.int32, (Q_BLOCK, nc), 1)
        rowq = lax.broadcasted_iota(jnp.int32, (Q_BLOCK, nc), 0) + s0
        bias_scr[:, 0:nc] = jnp.where(colc * STRIDE + (L_CMP - 1) <= rowq, 0.0, -jnp.inf)
        s_all = _dot_nt(stacked(qun_ref, 0, N_HEADS), kcc_ref[0:nc, :])
        for kv in range(N_KV):
            for r in range(Q_BLOCK // rg):
                bias = bias_scr[r * rg:(r + 1) * rg, 0:nc]
                imp = None
                for g in range(GQA):
                    r0 = (kv * GQA + g) * Q_BLOCK + r * rg
                    p = _softmax_rows(s_all[r0:r0 + rg] + bias)
                    p_scr[r0:r0 + rg, 0:nc] = p.astype(BF16)
                    imp = p if imp is None else imp + p
                imp_scr[kv, r * rg:(r + 1) * rg, 0:nc] = imp
        ocmp_scr[...] = _dot(p_scr[:, 0:nc], vcc_ref[0:nc, :])

        blk = lax.broadcasted_iota(jnp.int32, (nbl, Q_BLOCK), 0)
        qpl = lax.broadcasted_iota(jnp.int32, (nbl, Q_BLOCK), 1) + s0
        cur = qpl >> 6
        valid = blk * L_SEL <= qpl
        forced = (blk == 0) | ((blk <= cur) & (blk > cur - N_LOCAL))
        blkf = blk.astype(F32)
        wsel = wsel_ref[0:nbl, 0:nc]
        scs = []
        for kv in range(N_KV):
            hi, mid, lo = _split3(imp_scr[kv, :, 0:nc])
            sc = _dot_nt(wsel, hi) + _dot_nt(wsel, mid) + _dot_nt(wsel, lo)
            scs.append(jnp.where(forced, -jnp.inf, jnp.where(valid, sc, -jnp.inf)))
        sc2 = _knock_out_top(jnp.concatenate(scs, axis=1), jnp.concatenate([blkf, blkf], axis=1),
                             n_top - 1 - N_LOCAL)
        nbl_pad = min(-(-nbl // LANES) * LANES, nb)
        for kv in range(N_KV):
            picked = valid & (sc2[:, kv * Q_BLOCK:(kv + 1) * Q_BLOCK] == -jnp.inf)
            bias_t = jnp.where(picked, 0.0, NEG)
            if nbl_pad > nbl:
                bias_t = jnp.concatenate([bias_t, jnp.full((nbl_pad - nbl, Q_BLOCK), NEG, F32)], axis=0)
            sel_scr[kv * Q_BLOCK:(kv + 1) * Q_BLOCK, 0:nbl_pad] = bias_t.T.astype(BF16)
        if nbl_pad < nb:
            sel_scr[:, nbl_pad:nb] = jnp.full((N_KV * Q_BLOCK, nb - nbl_pad), NEG, BF16)

    n_qblk = pl.num_programs(0)
    n_var = 4 if ncp % (4 * LANES) == 0 else 1
    for v in range(n_var):
        nc_v = (v + 1) * ncp // n_var
        nb_v = (v + 1) * nb // n_var

        @pl.when((i * n_var) // n_qblk == v)
        def _():
            compress_and_select(nc_v, nb_v)

    o_cmp = [rows(ocmp_scr[...], h) for h in range(N_HEADS)]

    wk = WINDOW + Q_BLOCK
    w0 = pl.multiple_of(jnp.maximum(s0 - WINDOW, 0), Q_BLOCK)
    kw = att_ref[pl.ds(w0, wk), 256:384]
    vw = att_ref[pl.ds(w0, wk), 384:512]
    kposw = lax.broadcasted_iota(jnp.int32, (Q_BLOCK, wk), 1) + w0
    roww = lax.broadcasted_iota(jnp.int32, (Q_BLOCK, wk), 0) + s0
    okw = (kposw <= roww) & (kposw > roww - WINDOW)
    biasw_scr[...] = jnp.where(okw, 0.0, -jnp.inf)
    s_all = _dot_nt(stacked(qrot_ref, 0, N_HEADS), kw)
    for h in range(N_HEADS):
        for r in range(Q_BLOCK // rg):
            r0 = h * Q_BLOCK + r * rg
            p = _softmax_rows(s_all[r0:r0 + rg] + biasw_scr[r * rg:(r + 1) * rg, :])
            pw_scr[r0:r0 + rg, :] = p.astype(BF16)
    owin_scr[...] = _dot(pw_scr[...], vw)

    o_sel, o_win = [], []
    rowk = lax.broadcasted_iota(jnp.int32, (Q_BLOCK, kt), 0) + s0
    lanek = lax.broadcasted_iota(jnp.int32, (Q_BLOCK, kt), 1)
    n_tiles = (s0 + Q_BLOCK - 1) // kt + 1
    nch = kt // LANES
    bgrp = min(LANES, nb)

    def lane_chunks(x):
        return [x[:, c * LANES:(c + 1) * LANES] for c in range(nch)]

    for h in range(N_HEADS):
        m_scr[h] = jnp.full((Q_BLOCK, LANES), NEG, F32)
        l_scr[h] = jnp.zeros((Q_BLOCK, LANES), F32)
        acc_scr[h] = jnp.zeros((Q_BLOCK, LANES), F32)
    q8 = stacked(qrot_ref, 0, N_HEADS)

    def tile_body(t, carry, diagonal):
        k0 = pl.multiple_of(t * kt, kt)
        ks = att_ref[pl.ds(k0, kt), 0:128]
        vs = att_ref[pl.ds(k0, kt), 128:256]
        b0 = pl.multiple_of(((t * (kt // L_SEL)) // bgrp) * bgrp, bgrp)
        bias2 = _dot(sel_scr[:, pl.ds(b0, bgrp)], eall_ref[pl.ds(b0, bgrp), pl.ds(k0, kt)])
        biases = [rows(bias2, kv) for kv in range(N_KV)]
        if diagonal:
            biases = [jnp.where(lanek + k0 <= rowk, b, NEG) for b in biases]
        s8 = _dot_nt(q8, ks)
        ps, alphas = [], []
        for h in range(N_HEADS):
            s = rows(s8, h) + biases[h // GQA]
            m_old = m_scr[h]
            m_new = jnp.maximum(m_old, jnp.max(functools.reduce(jnp.maximum, lane_chunks(s)), -1, keepdims=True))
            alpha = jnp.exp(m_old - m_new)
            p = jnp.exp(s - jnp.tile(m_new, (1, nch)))
            l_scr[h] = alpha * l_scr[h] + functools.reduce(jnp.add, lane_chunks(p))
            m_scr[h] = m_new
            ps.append(p.astype(BF16))
            alphas.append(alpha)
        pv = _dot(jnp.concatenate(ps, axis=0), vs)
        for h in range(N_HEADS):
            acc_scr[h] = alphas[h] * acc_scr[h] + rows(pv, h)
        return carry

    lax.fori_loop(0, n_tiles - 1, functools.partial(tile_body, diagonal=False), 0)
    tile_body(n_tiles - 1, 0, diagonal=True)
    for h in range(N_HEADS):
        o_sel.append(acc_scr[h] / jnp.sum(l_scr[h], -1, keepdims=True))

    o_win = [rows(owin_scr[...], h) for h in range(N_HEADS)]

    sm = sm_ref[...]
    lane = lax.broadcasted_iota(jnp.int32, (Q_BLOCK, LANES), 1)
    chunks = []
    for c in range(N_HEADS // 2):
        parts = []
        for h in (2 * c, 2 * c + 1):
            kv = h // GQA
            gc = M_HEADS + 3 * h
            a = sm[:, gc:gc + 1] * o_cmp[h] + sm[:, gc + 1:gc + 2] * o_sel[h] + sm[:, gc + 2:gc + 3] * o_win[h]
            a = jnp.where((lane >= kv * HEAD_DIM) & (lane < (kv + 1) * HEAD_DIM), a, 0.0)
            if kv != h % 2:
                a = pltpu.roll(a, HEAD_DIM, 1)
            parts.append(a)
        chunks.append(parts[0] + parts[1])
    o_ref[...] = jnp.concatenate(chunks, axis=1)


def _nsa_prompt(qun, qrot, sm, kcc, vcc, att, wsel, eall):
    t = qun.shape[0]
    ncp = kcc.shape[0]
    wk = WINDOW + Q_BLOCK
    row = lambda w: pl.BlockSpec((Q_BLOCK, w), lambda i: (i, 0))
    return pl.pallas_call(
        _nsa_prompt_kernel, grid=(t // Q_BLOCK,),
        in_specs=[row(1024), row(1024), row(128), _resident(kcc.shape), _resident(vcc.shape),
                  _resident(att.shape), _resident(wsel.shape), _resident(eall.shape)],
        out_specs=row(512),
        out_shape=jax.ShapeDtypeStruct((t, 512), F32),
        scratch_shapes=[pltpu.VMEM((N_HEADS, Q_BLOCK, LANES), F32)] * 3 + [
            pltpu.VMEM((N_KV * Q_BLOCK, wsel.shape[0]), BF16),
            pltpu.VMEM((Q_BLOCK, ncp), F32),
            pltpu.VMEM((N_HEADS * Q_BLOCK, ncp), BF16),
            pltpu.VMEM((N_KV, Q_BLOCK, ncp), F32),
            pltpu.VMEM((N_HEADS * Q_BLOCK, LANES), F32),
            pltpu.VMEM((Q_BLOCK, wk), F32),
            pltpu.VMEM((N_HEADS * Q_BLOCK, wk), BF16),
            pltpu.VMEM((N_HEADS * Q_BLOCK, LANES), F32)],
        compiler_params=_cparams(("arbitrary",)), name="nsa_prompt",
    )(qun, qrot, sm, kcc, vcc, att, wsel, eall)


def _out_kernel(x_ref, m_ref, a_ref, ang_ref, wom_ref, woa_ref, l1g_ref, l1b_ref, w1_ref, w2_ref,
                l2g_ref, l2b_ref, y_ref):
    a = a_ref[...]
    an = a * lax.rsqrt(jnp.mean(a * a, -1, keepdims=True) + RMS_EPS) * ang_ref[...]
    mix = _dot(m_ref[...], wom_ref[...]) + _dot(an.astype(BF16), woa_ref[...])
    h = _layer_norm(DEEPNORM_ALPHA * x_ref[...] + mix, l1g_ref[...], l1b_ref[...])
    hb = h.astype(BF16)
    f = jnp.zeros(h.shape, F32)
    for c in range(D_FF // 1024):
        cs = slice(c * 1024, (c + 1) * 1024)
        u = jnp.maximum(_dot(hb, w1_ref[:, cs]), 0.0)
        f = f + _dot((u * u).astype(BF16), w2_ref[cs, :])
    y_ref[...] = _layer_norm(DEEPNORM_ALPHA * h + f, l2g_ref[...], l2b_ref[...])


def _output_block(x, m_out, a_out, att, ow, tm):
    t = x.shape[0]
    row = lambda w: pl.BlockSpec((tm, w), lambda i: (i, 0))
    names = ("l1g", "l1b", "w1", "w2", "l2g", "l2b")
    args = [att["ang"], ow["wom"], att["woa"]] + [ow[k] for k in names]
    return pl.pallas_call(
        _out_kernel, grid=(t // tm,),
        in_specs=[row(D_MODEL), row(512), row(a_out.shape[1])] + [_resident(a.shape) for a in args],
        out_specs=row(D_MODEL),
        out_shape=jax.ShapeDtypeStruct((t, D_MODEL), F32),
        compiler_params=_cparams(("arbitrary",)), name="out_block",
    )(x, m_out, a_out, *args)


def _sample_cmp_kernel(pt_ref, cache_ref, new_ref, q_ref, wk_ref, wv_ref,
                       pek, w1k, b1k, w2k, b2k, pev, w1v, b1v, w2v, b2v, wsel_ref,
                       ocmp_ref, score_ref, buf, xs, sem, *, n_pages, past_len):
    b = pl.program_id(0)
    nb = pl.num_programs(0)
    n_blocks = n_pages * PAGE_SIZE // STRIDE

    def page_copy(bb, slot, pg):
        return pltpu.make_async_copy(cache_ref.at[pt_ref[bb * n_pages + pg], pl.ds(0, 2)],
                                     buf.at[slot, pg], sem.at[slot])

    def start_all(bb, slot):
        def body(pg, carry):
            page_copy(bb, slot, pg).start()
            return carry
        lax.fori_loop(0, n_pages, body, 0)

    def wait_all(bb, slot):
        def body(pg, carry):
            page_copy(bb, slot, pg).wait()
            return carry
        lax.fori_loop(0, n_pages, body, 0)

    slot = b % 2

    @pl.when(b == 0)
    def _():
        start_all(0, 0)

        @pl.when(nb > 1)
        def _():
            start_all(1, 1)

    wait_all(b, slot)

    def to_rows(pg, carry):
        r0 = pl.multiple_of(pg * PAGE_SIZE, PAGE_SIZE)
        for c in range(2):
            xs[c, pl.ds(r0, PAGE_SIZE), :] = buf[slot, pg, c].astype(BF16).T.astype(F32)
        return carry
    lax.fori_loop(0, n_pages, to_rows, 0, unroll=math.gcd(n_pages, 8))

    @pl.when(b + 2 < nb)
    def _():
        start_all(b + 2, slot)

    new = jnp.broadcast_to(new_ref[0], (8, 512))
    rowi = lax.broadcasted_iota(jnp.int32, (n_blocks, 256), 0)
    toks = []
    for c, w_ref, pe, w1f, b1, w2, b2 in ((0, wk_ref, pek, w1k, b1k, w2k, b2k),
                                          (1, wv_ref, pev, w1v, b1v, w2v, b2v)):
        h = _stride_block_proj(xs.at[c], n_blocks, w_ref)
        h_new = _dot(new[:, c * LANES:(c + 1) * LANES].astype(BF16), w_ref[0:LANES, 256:512])[0:1]
        h1_next = jnp.where(rowi == n_blocks - 1, h_new, pltpu.roll(h[:, 256:512], n_blocks - 1, 0))
        toks.append(_cmp_mlp(h[:, 0:256], h1_next, _cmp_const(pe, w1f, b1), w2, b2).astype(BF16))
    kcc, vcc = toks

    q = q_ref[0]
    zq = jnp.zeros_like(q)
    rowh = lax.broadcasted_iota(jnp.int32, (N_HEADS, LANES), 0)
    qpad = jnp.where(rowh < GQA, jnp.concatenate([q, zq], 1), jnp.concatenate([zq, q], 1))
    s = _dot_nt(qpad, kcc)
    colc = lax.broadcasted_iota(jnp.int32, s.shape, 1)
    p = _softmax_rows(jnp.where(colc * STRIDE + (L_CMP - 1) <= past_len, s, -jnp.inf))
    o = _dot(p.astype(BF16), vcc)
    ocmp_ref[0] = jnp.where(rowh[:, 0:HEAD_DIM] < GQA, o[:, 0:HEAD_DIM], o[:, HEAD_DIM:])
    imp0 = p[0:1] + p[1:2] + p[2:3] + p[3:4]
    imp1 = p[4:5] + p[5:6] + p[6:7] + p[7:8]
    rowp = lax.broadcasted_iota(jnp.int32, p.shape, 0)
    imp = jnp.where(rowp == 0, imp0, jnp.where(rowp == 1, imp1, 0.0))
    score_ref[0] = _dot3(imp, wsel_ref[...])


def _sample_cmp(page_table, cache4, kv4_s, q8, cw, wsel_s, past_len):
    db, n_pages = page_table.shape
    names = ("pek", "w1k", "b1k", "w2k", "b2k", "pev", "w1v", "b1v", "w2v", "b2v")
    args = [cw[k] for k in names]
    nbp = wsel_s.shape[1]
    full = lambda a: pl.BlockSpec(a.shape, lambda b, pt: (0,) * a.ndim)
    grid_spec = pltpu.PrefetchScalarGridSpec(
        num_scalar_prefetch=1, grid=(db,),
        in_specs=[pl.BlockSpec(memory_space=pl.ANY),
                  pl.BlockSpec((1, 1, 512), lambda b, pt: (b, 0, 0)),
                  pl.BlockSpec((1, N_HEADS, HEAD_DIM), lambda b, pt: (b, 0, 0)),
                  full(cw["wk"]), full(cw["wv"])] + [full(a) for a in args] + [full(wsel_s)],
        out_specs=(pl.BlockSpec((1, N_HEADS, HEAD_DIM), lambda b, pt: (b, 0, 0)),
                   pl.BlockSpec((1, 8, nbp), lambda b, pt: (b, 0, 0))),
        scratch_shapes=[pltpu.VMEM((2, n_pages, 2, LANES, PAGE_SIZE), F32),
                        pltpu.VMEM((2, n_pages * PAGE_SIZE, LANES), F32),
                        pltpu.SemaphoreType.DMA((2,))],
    )
    return pl.pallas_call(
        functools.partial(_sample_cmp_kernel, n_pages=n_pages, past_len=past_len),
        grid_spec=grid_spec,
        out_shape=(jax.ShapeDtypeStruct((db, N_HEADS, HEAD_DIM), F32), jax.ShapeDtypeStruct((db, 8, nbp), F32)),
        compiler_params=_cparams(("arbitrary",)), name="sample_cmp",
    )(page_table.reshape(-1), cache4, kv4_s.reshape(db, 1, 512), q8, cw["wk"], cw["wv"], *args, wsel_s)


def _sample_topk_kernel(score_ref, idx_ref, *, n_slc, qp):
    sc = score_ref[...]
    blk = lax.broadcasted_iota(jnp.int32, sc.shape, 1)
    cur = qp // L_SEL
    valid = (blk * L_SEL <= qp) & (blk < n_slc)
    forced = (blk == 0) | ((blk <= cur) & (blk > cur - N_LOCAL))
    sc = jnp.where(forced & (blk < n_slc), jnp.inf, jnp.where(valid, sc, -jnp.inf))
    real = blk < n_slc
    blkf = blk.astype(F32)
    taken = jnp.where(real, 0.0, 1.0)
    out = jnp.zeros(idx_ref.shape, F32)
    lane = lax.broadcasted_iota(jnp.int32, idx_ref.shape, 1)
    for it in range(min(N_SEL, n_slc)):
        m = jnp.max(jnp.where(taken == 0.0, sc, -jnp.inf), -1, keepdims=True)
        cand = (sc == m) & (taken == 0.0)
        idx = jnp.min(jnp.where(cand, blkf, 1e9), -1, keepdims=True)
        pick = blkf == idx
        taken = jnp.where(pick, 1.0, taken)
        out = jnp.where(lane == it, idx, out)
    idx_ref[...] = out.astype(jnp.int32)


def _sample_topk(scores, n_slc, qp):
    rows, nbp = scores.shape
    return pl.pallas_call(
        functools.partial(_sample_topk_kernel, n_slc=n_slc, qp=qp), grid=(1,),
        in_specs=[_full(scores.shape)], out_specs=_full((rows, LANES)),
        out_shape=jax.ShapeDtypeStruct((rows, LANES), jnp.int32),
        compiler_params=_cparams(("arbitrary",)), name="sample_topk",
    )(scores)


def _sample_sel_kernel(pt_ref, idx_ref, cache_ref, q_ref, new_ref, osel_ref, buf, sem, *, n_pages, past_len, n_top):
    b = pl.program_id(0)
    nb = pl.num_programs(0)
    per_page = PAGE_SIZE // L_SEL
    n_past_blocks = past_len // L_SEL
    n_cp = N_KV * n_top

    def blk_copy(bb, slot, j):
        kv = j // n_top
        n = j - kv * n_top
        blk = jnp.minimum(idx_ref[bb * n_cp + j], n_past_blocks - 1)
        page = pt_ref[bb * n_pages + blk // per_page]
        return pltpu.make_async_copy(
            cache_ref.at[page, pl.ds(2, 2), pl.ds(pl.multiple_of(kv * HEAD_DIM, HEAD_DIM), HEAD_DIM), :],
            buf.at[slot, kv, :, :, pl.ds(pl.multiple_of(n * PAGE_SIZE, PAGE_SIZE), PAGE_SIZE)], sem.at[slot])

    def start_all(bb, slot):
        def body(j, c):
            blk_copy(bb, slot, j).start()
            return c
        lax.fori_loop(0, n_cp, body, 0)

    def wait_all(bb, slot):
        def body(j, c):
            blk_copy(bb, slot, j).wait()
            return c
        lax.fori_loop(0, n_cp, body, 0)

    slot = b % SEL_SLOTS

    @pl.when(b == 0)
    def _():
        for d in range(SEL_SLOTS - 1):
            @pl.when(d < nb)
            def _():
                start_all(d, d)

    @pl.when(b + SEL_SLOTS - 1 < nb)
    def _():
        start_all(b + SEL_SLOTS - 1, (b + SEL_SLOTS - 1) % SEL_SLOTS)

    wait_all(b, slot)

    q = q_ref[0]
    new = jnp.broadcast_to(new_ref[0], (8, 512))
    nk = n_top * PAGE_SIZE
    lane = lax.broadcasted_iota(jnp.int32, (N_HEADS, nk), 1)
    outs = []
    for kv in range(N_KV):
        ks_new = new[:, 256 + kv * HEAD_DIM:256 + (kv + 1) * HEAD_DIM].astype(BF16)
        vs_new = new[:, 384 + kv * HEAD_DIM:384 + (kv + 1) * HEAD_DIM].astype(BF16).astype(F32)
        s_new = _dot_nt(q, ks_new)[:, 0:1]
        kt = buf[slot, kv, 0].astype(BF16)
        vt = buf[slot, kv, 1].astype(BF16)
        blk_of_lane = jnp.zeros((N_HEADS, nk), jnp.int32)
        for n in range(n_top):
            blk_of_lane = jnp.where((lane >> 7) == n, idx_ref[b * n_cp + kv * n_top + n], blk_of_lane)
        in_block = ((lane & (PAGE_SIZE - 1)) >> 6) == (blk_of_lane & (per_page - 1))
        ok = in_block & (blk_of_lane < n_past_blocks)
        s = jnp.where(ok, _dot(q, kt), -jnp.inf)
        m = jnp.maximum(jnp.max(s, -1, keepdims=True), s_new)
        e = jnp.exp(s - m)
        e_new = jnp.exp(s_new - m)
        denom = jnp.sum(e, -1, keepdims=True) + e_new
        o = _dot_nt((e / denom).astype(BF16), vt) + (e_new / denom).astype(BF16).astype(F32) * vs_new
        outs.append(o)
    row = lax.broadcasted_iota(jnp.int32, (N_HEADS, HEAD_DIM), 0)
    osel_ref[0] = jnp.where(row < GQA, outs[0], outs[1])


def _sample_sel(page_table, idx, cache4, qrot8, kv4_s, past_len, n_top):
    db, n_pages = page_table.shape
    assert PAGE_SIZE // L_SEL == 2
    hb = pl.BlockSpec((1, N_HEADS, HEAD_DIM), lambda b, pt, ix: (b, 0, 0))
    grid_spec = pltpu.PrefetchScalarGridSpec(
        num_scalar_prefetch=2, grid=(db,),
        in_specs=[pl.BlockSpec(memory_space=pl.ANY), hb,
                  pl.BlockSpec((1, 1, 512), lambda b, pt, ix: (b, 0, 0))],
        out_specs=hb,
        scratch_shapes=[pltpu.VMEM((SEL_SLOTS, N_KV, 2, HEAD_DIM, n_top * PAGE_SIZE), F32),
                        pltpu.SemaphoreType.DMA((SEL_SLOTS,))],
    )
    return pl.pallas_call(
        functools.partial(_sample_sel_kernel, n_pages=n_pages, past_len=past_len, n_top=n_top),
        grid_spec=grid_spec,
        out_shape=jax.ShapeDtypeStruct((db, N_HEADS, HEAD_DIM), F32),
        compiler_params=_cparams(("arbitrary",)), name="sample_sel",
    )(page_table.reshape(-1), idx.reshape(-1), cache4, qrot8, kv4_s.reshape(db, 1, 512))


def _sample_win_kernel(st_ref, newrow_ref, newall_ref, q_ref, g_ref, ocmp_ref, osel_ref, a_ref, wout_ref):
    i = pl.program_id(0)
    tb, wb = st_ref.shape[0], st_ref.shape[3]
    db = newall_ref.shape[0]
    new_t = newall_ref[...].T
    rowb = lax.broadcasted_iota(jnp.int32, (db, LANES), 0)
    lane_w = lax.broadcasted_iota(jnp.int32, (HEAD_DIM, wb), 1)
    lane_s = lax.broadcasted_iota(jnp.int32, (N_HEADS, wb), 1)
    rowh = lax.broadcasted_iota(jnp.int32, (N_HEADS, HEAD_DIM), 0)
    for j in range(tb):
        pick = jnp.where(rowb == i * tb + j, 1.0, 0.0).astype(BF16)
        col = _dot3(new_t, pick)
        for ckv in range(2 * N_KV):
            colw = jnp.tile(col[ckv * HEAD_DIM:(ckv + 1) * HEAD_DIM], (1, wb // LANES))
            wout_ref[j, ckv] = jnp.where(lane_w == wb - 1, colw, pltpu.roll(st_ref[j, ckv], wb - 1, 1))
        new = jnp.broadcast_to(newrow_ref[j], (8, 256))
        q = q_ref[j]
        outs = []
        for kv in range(N_KV):
            kt = st_ref[j, kv].astype(BF16)
            vt = st_ref[j, N_KV + kv].astype(BF16)
            kw_new = new[:, kv * HEAD_DIM:(kv + 1) * HEAD_DIM].astype(BF16)
            vw_new = new[:, 128 + kv * HEAD_DIM:128 + (kv + 1) * HEAD_DIM].astype(BF16).astype(F32)
            s_new = _dot_nt(q, kw_new)[:, 0:1]
            s = jnp.where(lane_s >= 1, _dot(q, kt), -jnp.inf)
            m = jnp.maximum(jnp.max(s, -1, keepdims=True), s_new)
            e = jnp.exp(s - m)
            e_new = jnp.exp(s_new - m)
            denom = jnp.sum(e, -1, keepdims=True) + e_new
            outs.append(_dot_nt((e / denom).astype(BF16), vt) + (e_new / denom).astype(BF16).astype(F32) * vw_new)
        o_win = jnp.where(rowh < GQA, outs[0], outs[1])
        g = g_ref[j]
        a_ref[j] = g[:, 0:1] * ocmp_ref[j] + g[:, 1:2] * osel_ref[j] + g[:, 2:3] * o_win


def _sample_win(state4, win_new, qrot8, gates8, ocmp, osel, tb):
    db, _, _, wb = state4.shape
    b3 = lambda s: pl.BlockSpec((tb,) + s, lambda i: (i, 0, 0))
    b4 = pl.BlockSpec((tb, 2 * N_KV, HEAD_DIM, wb), lambda i: (i, 0, 0, 0))
    hd = (N_HEADS, HEAD_DIM)
    return pl.pallas_call(
        _sample_win_kernel, grid=(db // tb,),
        in_specs=[b4, b3((1, 256)), _full(win_new.shape), b3(hd), b3((8, 128)), b3(hd), b3(hd)],
        out_specs=(b3(hd), b4),
        out_shape=(jax.ShapeDtypeStruct((db,) + hd, F32), jax.ShapeDtypeStruct(state4.shape, F32)),
        compiler_params=_cparams(("arbitrary",)), name="sample_win",
    )(state4, win_new.reshape(db, 1, 256), win_new, qrot8, gates8, ocmp, osel)


def _rope_tables(pos):
    half = ROT_DIM // 2
    inv = ROPE_THETA ** (-jnp.arange(half, dtype=F32) * 2.0 / ROT_DIM)
    ang = pos.astype(F32)[:, None] * inv[None, :]
    cos, sin = jnp.cos(ang), jnp.sin(ang)
    n = pos.shape[0]
    rest = HEAD_DIM - ROT_DIM
    one, zero, z8 = jnp.ones((n, rest), F32), jnp.zeros((n, rest), F32), jnp.zeros((n, half), F32)
    c = jnp.concatenate([cos, cos, one], 1)
    s1 = jnp.concatenate([z8, sin, zero], 1)
    s2 = jnp.concatenate([-sin, z8, zero], 1)
    return tuple(jnp.tile(a, (1, 2)) for a in (c, s1, s2))


def _proj_weights(w_in):
    sizes = (SSM_WIDTH, CONV_DIM, M_HEADS, ATT_WIDTH, 6 * N_KV * HEAD_DIM, 3 * N_HEADS)
    parts, start = [], 0
    for s in sizes:
        parts.append(w_in[:, start:start + s])
        start += s
    wz, wxbc, wdt, wq, wkv, wg = parts
    zeros = jnp.zeros((D_MODEL, HEAD_DIM), w_in.dtype)
    chunks = []
    for h in range(N_HEADS):
        wh = wq[:, h * HEAD_DIM:(h + 1) * HEAD_DIM]
        chunks += [wh, zeros] if h // GQA == 0 else [zeros, wh]
    wq_pad = jnp.concatenate(chunks, 1)
    wsm = jnp.concatenate([wdt, wg, jnp.zeros((D_MODEL, LANES - M_HEADS - 3 * N_HEADS), w_in.dtype)], 1)
    wz, wxbc, wq, wq_pad, wkv, wsm = (w.astype(BF16) for w in (wz, wxbc, wq, wq_pad, wkv, wsm))
    return (wz, wxbc, wq_pad, wkv, wsm), (wz, wxbc, wq, wkv, wsm)


def _cmp_weights(pe_cmp, w1_cmp, b1_cmp, w2_cmp, b2_cmp):
    out = {}
    for i, nm in enumerate("kv"):
        w1 = w1_cmp[i].reshape(2, STRIDE, HEAD_DIM, CMP_HIDDEN)
        wbig = jnp.zeros((STRIDE, LANES, 512), F32)
        for r in range(2):
            for kv in range(N_KV):
                c0 = r * 256 + kv * CMP_HIDDEN
                wbig = wbig.at[:, kv * HEAD_DIM:(kv + 1) * HEAD_DIM, c0:c0 + CMP_HIDDEN].set(w1[r])
        out["w" + nm] = wbig.reshape(STRIDE * LANES, 512).astype(BF16)
        out["pe" + nm] = jnp.broadcast_to(pe_cmp[i].reshape(1, L_CMP * HEAD_DIM), (8, L_CMP * HEAD_DIM)).astype(BF16)
        out["w1" + nm] = w1_cmp[i].reshape(L_CMP * HEAD_DIM, CMP_HIDDEN).astype(BF16)
        out["b1" + nm] = jnp.tile(b1_cmp[i].reshape(1, CMP_HIDDEN), (1, 2))
        w2 = jnp.zeros((2 * CMP_HIDDEN, LANES), F32)
        for kv in range(N_KV):
            w2 = w2.at[kv * CMP_HIDDEN:(kv + 1) * CMP_HIDDEN, kv * HEAD_DIM:(kv + 1) * HEAD_DIM].set(w2_cmp[i])
        out["w2" + nm] = w2.astype(BF16)
        out["b2" + nm] = jnp.tile(b2_cmp[i].reshape(1, HEAD_DIM), (1, 2))
    return out


def _select_weights(n_blocks, n_cmp_padded):
    b = jnp.arange(n_blocks)[:, None]
    c = jnp.arange(n_cmp_padded)[None, :]
    k = c - 4 * b + 1
    w = jnp.where((k == 0) | (k == 4), 16.0, jnp.where((k >= 1) & (k <= 3), 32.0, 0.0))
    return w.astype(BF16)


def _pad_row(v, n=LANES):
    v = v.reshape(1, -1).astype(F32)
    return jnp.concatenate([v, jnp.zeros((1, n - v.shape[1]), F32)], 1)


def kernel(x_prompt, x_sample, cache_kv, state_win, state_ssm, state_conv, page_table, w_in, conv_w, conv_b,
           dt_bias, a_log, d_skip, ssm_norm_g, pe_cmp, w1_cmp, b1_cmp, w2_cmp, b2_cmp, att_norm_g, w_out,
           ln1_g, ln1_b, w_ff1, w_ff2, ln2_g, ln2_b):
    bsz, seq, _ = x_prompt.shape
    db, dseq, _ = x_sample.shape
    n_pages = page_table.shape[1]
    past_len = n_pages * PAGE_SIZE
    wb = state_win.shape[1]
    assert bsz == 1 and dseq == 1 and wb == WINDOW and past_len % L_SEL == 0 and seq % SEL_KEY_TILE == 0

    pw, pw_flat = _proj_weights(w_in)
    cw = _cmp_weights(pe_cmp, w1_cmp, b1_cmp, w2_cmp, b2_cmp)
    cb = conv_b.reshape(1, CONV_DIM)
    dtb, alog = _pad_row(dt_bias), _pad_row(a_log)
    dsk = jnp.repeat(d_skip, HEAD_DIM).reshape(1, SSM_WIDTH)
    ng = ssm_norm_g.reshape(1, SSM_WIDTH)
    ow = dict(wom=w_out[0:SSM_WIDTH].astype(BF16), l1g=ln1_g.reshape(1, -1),
              l1b=ln1_b.reshape(1, -1), w1=w_ff1.astype(BF16), w2=w_ff2.astype(BF16),
              l2g=ln2_g.reshape(1, -1), l2b=ln2_b.reshape(1, -1))
    woa = w_out[SSM_WIDTH:D_MODEL]
    ang = att_norm_g.reshape(ATT_WIDTH, 1)
    att_flat = dict(ang=ang.reshape(1, -1), woa=woa.astype(BF16))

    xp = x_prompt.reshape(seq, D_MODEL)
    tm = min(256, seq)
    z, xbc, qun, qrot, kv4, win, att, sm = _project(
        xp, pw, _rope_tables(jnp.arange(seq, dtype=jnp.int32)), tm, min(WINDOW, seq))
    m_out, conv_p, ssm_p = _mamba_prompt(z, xbc, sm, conv_w, cb, dtb, alog, dsk, ng)
    kcc, vcc = _compress_prompt(kv4, cw)
    n_slc = seq // L_SEL
    wsel = _select_weights(n_slc, seq // STRIDE)
    eall = (jnp.arange(n_slc)[:, None] == (jnp.arange(seq) // L_SEL)[None, :]).astype(BF16)
    a_out = _nsa_prompt(qun, qrot, sm, kcc, vcc, att, wsel, eall)
    y_prompt = _output_block(xp, m_out, a_out, att_flat, ow, tm).reshape(bsz, seq, D_MODEL)
    kv_prompt = kv4.reshape(bsz, seq, 4, N_KV, HEAD_DIM)
    win_prompt = win.reshape(bsz, min(WINDOW, seq), 2, N_KV, HEAD_DIM)
    ssm_prompt = ssm_p.reshape(bsz, M_HEADS, HEAD_DIM, D_STATE)
    conv_prompt = conv_p.reshape(bsz, CONV_W - 1, CONV_DIM)

    xs_ = x_sample.reshape(db, D_MODEL)
    pos_s = jnp.full((db,), past_len, jnp.int32)
    z, xbc, qun, qrot, kv4_s, win_s, _, sm = _project(xs_, pw_flat, _rope_tables(pos_s), db, db)
    m_out_s, conv_s, ssm_s = _mamba_sample(
        z, xbc, sm, jnp.transpose(state_conv, (1, 0, 2)), state_ssm.reshape(db, SSM_WIDTH, D_STATE),
        conv_w, cb, dtb, alog, dsk, ng)
    cache4 = jnp.transpose(cache_kv, (0, 2, 3, 4, 1)).reshape(cache_kv.shape[0], 4, N_KV * HEAD_DIM, PAGE_SIZE)
    state4 = jnp.transpose(state_win, (0, 2, 3, 4, 1)).reshape(db, 2 * N_KV, HEAD_DIM, wb)
    n_slc_s = -(-(past_len + dseq) // L_SEL)
    nbp = -(-n_slc_s // LANES) * LANES
    wsel_s = jnp.transpose(_select_weights(nbp, past_len // STRIDE))
    qun8 = qun.reshape(db, N_HEADS, HEAD_DIM)
    qrot8 = qrot.reshape(db, N_HEADS, HEAD_DIM)
    o_cmp, scores = _sample_cmp(page_table, cache4, kv4_s, qun8, cw, wsel_s, past_len)
    n_top = min(N_SEL, n_slc_s)
    idx = _sample_topk(scores[:, 0:N_KV, :].reshape(db * N_KV, nbp), n_slc_s, past_len)[:, 0:n_top]
    o_sel = _sample_sel(page_table, idx, cache4, qrot8, kv4_s, past_len, n_top)
    gates8 = jnp.pad(sm[:, M_HEADS:M_HEADS + 3 * N_HEADS].reshape(db, N_HEADS, 3), ((0, 0), (0, 0), (0, LANES - 3)))
    a8, win4 = _sample_win(state4, win_s, qrot8, gates8, o_cmp, o_sel, min(8, db))
    y_sample = _output_block(xs_, m_out_s, a8.reshape(db, ATT_WIDTH), att_flat, ow, db).reshape(db, dseq, D_MODEL)
    kv_sample = kv4_s.reshape(db, dseq, 4, N_KV, HEAD_DIM)
    win_sample = jnp.transpose(win4.reshape(db, 2, N_KV, HEAD_DIM, wb), (0, 4, 1, 2, 3))
    ssm_sample = ssm_s.reshape(db, M_HEADS, HEAD_DIM, D_STATE)
    conv_sample = jnp.transpose(conv_s, (1, 0, 2))

    return (y_prompt, y_sample, kv_prompt, win_prompt, ssm_prompt, conv_prompt,
            kv_sample, win_sample, ssm_sample, conv_sample)
```

```python
import functools
import math

import jax
import jax.numpy as jnp
from jax import lax
from jax.experimental import pallas as pl
from jax.experimental.pallas import tpu as pltpu

F32 = jnp.float32
BF16 = jnp.bfloat16

D_MODEL = 1024
HEAD_DIM = 64
SSM_WIDTH = 512
ATT_WIDTH = 512
M_HEADS = 8
M_GROUPS = 2
D_STATE = 128
CONV_W = 4
CONV_DIM = SSM_WIDTH + 2 * M_GROUPS * D_STATE
SSD_CHUNK = 256
RMS_EPS = 1e-6
N_HEADS = 8
N_KV = 2
GQA = N_HEADS // N_KV
ROT_DIM = 16
ROPE_THETA = 500000.0
L_CMP = 32
STRIDE = 16
CMP_HIDDEN = 128
L_SEL = 64
N_SEL = 16
N_LOCAL = 2
WINDOW = 512
Q_BLOCK = 128
ATT_SCALE = HEAD_DIM ** -0.5
D_FF = 4 * D_MODEL
LN_EPS = 1e-5
DEEPNORM_ALPHA = 2.0 ** 0.25
PAGE_SIZE = 128
LANES = 128
NEG = -1e30
SEL_KEY_TILE = 1024
SOFTMAX_ROW_GROUP = 16
SEL_SLOTS = 4
VMEM_LIMIT = 56 * 1024 * 1024


def _cparams(sem):
    return pltpu.CompilerParams(dimension_semantics=sem, vmem_limit_bytes=VMEM_LIMIT)


def _dot(a, b):
    return jnp.dot(a, b, preferred_element_type=F32)


def _dot_nt(a, b):
    return lax.dot_general(a, b, (((1,), (1,)), ((), ())), preferred_element_type=F32)


def _split3(x):
    hi = x.astype(BF16)
    r = x - hi.astype(F32)
    mid = r.astype(BF16)
    lo = (r - mid.astype(F32)).astype(BF16)
    return hi, mid, lo


def _dot3(x, w):
    hi, mid, lo = _split3(x)
    return _dot(hi, w) + _dot(mid, w) + _dot(lo, w)


def _sigmoid(x):
    return 1.0 / (1.0 + jnp.exp(-x))


def _silu(x):
    return x * _sigmoid(x)


def _softplus(x):
    return jnp.maximum(x, 0.0) + jnp.log1p(jnp.exp(-jnp.abs(x)))


def _layer_norm(x, g, b):
    mu = jnp.mean(x, -1, keepdims=True)
    xc = x - mu
    var = jnp.mean(xc * xc, -1, keepdims=True)
    return xc * lax.rsqrt(var + LN_EPS) * g + b


def _rope128(x, c, s1, s2):
    return x * c + pltpu.roll(x, 8, 1) * s1 + pltpu.roll(x, LANES - 8, 1) * s2


def _full(shape):
    nd = len(shape)
    return pl.BlockSpec(shape, lambda *a: (0,) * nd)


def _resident(shape):
    nd = len(shape)
    return pl.BlockSpec(shape, lambda *a: (0,) * nd, pipeline_mode=pl.Buffered(1))


def _proj_kernel(x_ref, wz_ref, wxbc_ref, wq_ref, wkv_ref, wsm_ref, c_ref, s1_ref, s2_ref,
                 z_ref, xbc_ref, qun_ref, qrot_ref, kv4_ref, win_ref, att_ref, sm_ref):
    x = x_ref[...].astype(BF16)
    c, s1, s2 = c_ref[...], s1_ref[...], s2_ref[...]
    z_ref[...] = _dot(x, wz_ref[...])
    xbc_ref[...] = _dot(x, wxbc_ref[...])
    q = _dot(x, wq_ref[...]) * ATT_SCALE
    qun_ref[...] = q.astype(BF16)
    for h in range(q.shape[1] // LANES):
        sl = slice(h * LANES, (h + 1) * LANES)
        qrot_ref[:, sl] = _rope128(q[:, sl], c, s1, s2).astype(BF16)
    kv = _dot(x, wkv_ref[...])
    ks = _rope128(kv[:, 256:384], c, s1, s2)
    kw = _rope128(kv[:, 512:640], c, s1, s2)
    kv4_ref[:, 0:256] = kv[:, 0:256]
    kv4_ref[:, 256:384] = ks
    kv4_ref[:, 384:512] = kv[:, 384:512]
    win_ref[:, 0:128] = kw
    win_ref[:, 128:256] = kv[:, 640:768]
    att_ref[:, 0:128] = ks.astype(BF16)
    att_ref[:, 128:256] = kv[:, 384:512].astype(BF16)
    att_ref[:, 256:384] = kw.astype(BF16)
    att_ref[:, 384:512] = kv[:, 640:768].astype(BF16)
    sm = _dot(x, wsm_ref[...])
    lane = lax.broadcasted_iota(jnp.int32, sm.shape, 1)
    is_gate = (lane >= M_HEADS) & (lane < M_HEADS + 3 * N_HEADS)
    sm_ref[...] = jnp.where(is_gate, _sigmoid(sm), sm)


def _project(x, weights, tables, tm, win_rows):
    t = x.shape[0]
    nt = t // tm
    nwb = win_rows // tm
    wz, wxbc, wq, wkv, wsm = weights
    qw = wq.shape[1]
    row = lambda w: pl.BlockSpec((tm, w), lambda i: (i, 0))
    out_shape = (
        jax.ShapeDtypeStruct((t, 512), F32),
        jax.ShapeDtypeStruct((t, 1024), F32),
        jax.ShapeDtypeStruct((t, qw), BF16),
        jax.ShapeDtypeStruct((t, qw), BF16),
        jax.ShapeDtypeStruct((t, 512), F32),
        jax.ShapeDtypeStruct((win_rows, 256), F32),
        jax.ShapeDtypeStruct((t, 512), BF16),
        jax.ShapeDtypeStruct((t, 128), F32),
    )
    out_specs = (row(512), row(1024), row(qw), row(qw), row(512),
                 pl.BlockSpec((tm, 256), lambda i: (jnp.maximum(i - (nt - nwb), 0), 0)),
                 row(512), row(128))
    in_specs = [row(D_MODEL), _full(wz.shape), _full(wxbc.shape), _full(wq.shape), _full(wkv.shape),
                _full(wsm.shape), row(128), row(128), row(128)]
    return pl.pallas_call(
        _proj_kernel, grid=(nt,), in_specs=in_specs, out_specs=out_specs, out_shape=out_shape,
        compiler_params=_cparams(("arbitrary",)), name="proj",
    )(x, wz, wxbc, wq, wkv, wsm, *tables)


def _mamba_prompt_kernel(z_ref, xbc_ref, sm_ref, cw_ref, cb_ref, dtb_ref, alog_ref, dsk_ref, ng_ref,
                         y_ref, conv_ref, ssm_ref, ubuf, state):
    c = pl.program_id(0)
    nc = pl.num_programs(0)
    L = SSD_CHUNK

    @pl.when(c == 0)
    def _():
        ubuf[0:8, :] = jnp.zeros((8, CONV_DIM), F32)
        state[...] = jnp.zeros(state.shape, F32)

    u = xbc_ref[...]
    ubuf[8:8 + L, :] = u
    acc = u * cw_ref[3:4, :] + cb_ref[...]
    for k in range(CONV_W - 1):
        acc = acc + ubuf[pl.ds(8 - (CONV_W - 1) + k, L), :] * cw_ref[k:k + 1, :]
    tail = ubuf[L:L + 8, :]
    ubuf[0:8, :] = tail
    xa = _silu(acc)
    xs = xa[:, 0:SSM_WIDTH]
    bm = xa[:, SSM_WIDTH:SSM_WIDTH + 256]
    cm = xa[:, SSM_WIDTH + 256:SSM_WIDTH + 512]

    lane = lax.broadcasted_iota(jnp.int32, (L, LANES), 1)
    head_lane = lane < M_HEADS
    dt = jnp.where(head_lane, _softplus(sm_ref[...] + dtb_ref[...]), 0.0)
    a_row = jnp.where(head_lane[0:1], -jnp.exp(alog_ref[...]), 0.0)
    da = dt * a_row
    ri = lax.broadcasted_iota(jnp.int32, (L, L), 0)
    ci = lax.broadcasted_iota(jnp.int32, (L, L), 1)
    causal = ri >= ci
    tril = jnp.where(causal, 1.0, 0.0).astype(BF16)
    da_hi, da_mid, da_lo = _split3(da)
    cum = _dot(tril, da_hi) + _dot(tril, da_mid) + _dot(tril, da_lo)
    cum_t = cum.T
    er = lax.broadcasted_iota(jnp.int32, (LANES, SSM_WIDTH), 0)
    ec = lax.broadcasted_iota(jnp.int32, (LANES, SSM_WIDTH), 1)
    expand = jnp.where(er == (ec >> 6), 1.0, 0.0).astype(BF16)
    dtx = _dot3(dt, expand)
    cumx = _dot3(cum, expand)
    cumx_last = cumx[L - 1:L, :]
    xc = xs * dtx
    xw = xc * jnp.exp(cumx_last - cumx)
    ecx = jnp.exp(cumx)
    dec_last = jnp.exp(cumx_last)

    half = lax.broadcasted_iota(jnp.int32, (L, LANES), 1) < HEAD_DIM
    y_parts = []
    for g in range(M_GROUPS):
        bg = bm[:, g * D_STATE:(g + 1) * D_STATE].astype(BF16)
        cg = cm[:, g * D_STATE:(g + 1) * D_STATE].astype(BF16)
        gs = slice(g * 256, (g + 1) * 256)
        scores = _dot_nt(cg, bg)
        h_prev = state[:, gs]
        y_off = _dot(cg, h_prev.astype(BF16)) * ecx[:, gs]
        st = _dot(bg.T, xw[:, gs].astype(BF16))
        state[:, gs] = h_prev * dec_last[:, gs] + st
        for pr in range(2):
            ha = g * 4 + pr * 2
            xcp = xc[:, ha * HEAD_DIM:(ha + 2) * HEAD_DIM].astype(BF16)
            outs = []
            for hh in (ha, ha + 1):
                seg = cum[:, hh:hh + 1] - cum_t[hh:hh + 1, :]
                dec = jnp.exp(jnp.where(causal, seg, -jnp.inf))
                outs.append(_dot((scores * dec).astype(BF16), xcp))
            y_parts.append(jnp.where(half, outs[0], outs[1]) + y_off[:, pr * 128:(pr + 1) * 128])
    y = jnp.concatenate(y_parts, axis=1) + dsk_ref[...] * xs
    y = y * _silu(z_ref[...])
    outs = []
    for g in range(M_GROUPS):
        yg = y[:, g * 256:(g + 1) * 256]
        outs.append(yg * lax.rsqrt(jnp.mean(yg * yg, -1, keepdims=True) + RMS_EPS))
    y_ref[...] = (jnp.concatenate(outs, axis=1) * ng_ref[...]).astype(BF16)

    @pl.when(c == nc - 1)
    def _():
        conv_ref[...] = ubuf[pl.ds(8 - (CONV_W - 1), CONV_W - 1), :]
        ssm_ref[...] = state[...].T


def _mamba_prompt(z, xbc, sm, cw, cb, dtb, alog, dsk, ng):
    t = z.shape[0]
    L = SSD_CHUNK
    row = lambda w: pl.BlockSpec((L, w), lambda i: (i, 0))
    return pl.pallas_call(
        _mamba_prompt_kernel, grid=(t // L,),
        in_specs=[row(512), row(1024), row(128), _full(cw.shape), _full(cb.shape), _full(dtb.shape),
                  _full(alog.shape), _full(dsk.shape), _full(ng.shape)],
        out_specs=(row(512), _full((CONV_W - 1, CONV_DIM)), _full((SSM_WIDTH, D_STATE))),
        out_shape=(jax.ShapeDtypeStruct((t, 512), BF16),
                   jax.ShapeDtypeStruct((CONV_W - 1, CONV_DIM), F32),
                   jax.ShapeDtypeStruct((SSM_WIDTH, D_STATE), F32)),
        scratch_shapes=[pltpu.VMEM((L + 8, CONV_DIM), F32), pltpu.VMEM((D_STATE, SSM_WIDTH), F32)],
        compiler_params=_cparams(("arbitrary",)), name="mamba_prompt",
    )(z, xbc, sm, cw, cb, dtb, alog, dsk, ng)


def _mamba_sample_kernel(z_ref, xbc_ref, sm_ref, sc_ref, h0_ref, cw_ref, cb_ref, dtb_ref, alog_ref, dsk_ref,
                         ng_ref, y_ref, conv_ref, h_ref, xct, dect, brow, crow, xcrow, decrow, xsrow, yrow):
    b = pl.program_id(0)
    nb = pl.num_programs(0)
    db = z_ref.shape[0]

    @pl.when(b == 0)
    def _():
        u = xbc_ref[...]
        acc = u * cw_ref[3:4, :] + cb_ref[...]
        for k in range(CONV_W - 1):
            acc = acc + sc_ref[k] * cw_ref[k:k + 1, :]
        conv_ref[0] = sc_ref[1]
        conv_ref[1] = sc_ref[2]
        conv_ref[2] = u
        xa = _silu(acc)
        xs = xa[:, 0:SSM_WIDTH]
        lane = lax.broadcasted_iota(jnp.int32, (db, LANES), 1)
        head_lane = lane < M_HEADS
        dt = jnp.where(head_lane, _softplus(sm_ref[...] + dtb_ref[...]), 0.0)
        da = dt * jnp.where(head_lane[0:1], -jnp.exp(alog_ref[...]), 0.0)
        er = lax.broadcasted_iota(jnp.int32, (LANES, SSM_WIDTH), 0)
        ec = lax.broadcasted_iota(jnp.int32, (LANES, SSM_WIDTH), 1)
        expand = jnp.where(er == (ec >> 6), 1.0, 0.0).astype(BF16)
        xc = xs * _dot3(dt, expand)
        dec = jnp.exp(_dot3(da, expand))
        xct[...] = xc.T
        dect[...] = dec.T
        xcrow[...] = xc
        decrow[...] = dec
        xsrow[...] = xs
        brow[...] = xa[:, SSM_WIDTH:SSM_WIDTH + 256]
        crow[...] = xa[:, SSM_WIDTH + 256:SSM_WIDTH + 512]

    r = lax.broadcasted_iota(jnp.int32, (db, LANES), 0)
    for j in range(h0_ref.shape[0]):
        bb = b * h0_ref.shape[0] + j
        pick = jnp.where(r == bb, 1.0, 0.0).astype(BF16)
        colx = _dot3(xct[...], pick)
        cold = _dot3(dect[...], pick)
        bv = brow[pl.ds(bb, 1), :]
        cv = crow[pl.ds(bb, 1), :]
        h0 = h0_ref[j]
        ys = []
        for g in range(M_GROUPS):
            rs = slice(g * 256, (g + 1) * 256)
            bg = bv[:, g * D_STATE:(g + 1) * D_STATE]
            cg = cv[:, g * D_STATE:(g + 1) * D_STATE]
            h_ref[j, rs, :] = cold[rs] * h0[rs] + colx[rs] * bg
            cb_s = jnp.sum(cg * bg, -1, keepdims=True)
            c8 = jnp.broadcast_to(cg, (8, D_STATE)).astype(BF16)
            y_off = _dot_nt(c8, h0[rs].astype(BF16))[0:1]
            ys.append(y_off * decrow[pl.ds(bb, 1), rs] + cb_s * xcrow[pl.ds(bb, 1), rs])
        yrow[pl.ds(bb, 1), :] = jnp.concatenate(ys, axis=1)

    @pl.when(b == nb - 1)
    def _():
        y = (yrow[...] + dsk_ref[...] * xsrow[...]) * _silu(z_ref[...])
        outs = []
        for g in range(M_GROUPS):
            yg = y[:, g * 256:(g + 1) * 256]
            outs.append(yg * lax.rsqrt(jnp.mean(yg * yg, -1, keepdims=True) + RMS_EPS))
        y_ref[...] = (jnp.concatenate(outs, axis=1) * ng_ref[...]).astype(BF16)


def _mamba_sample(z, xbc, sm, sc_t, h0, cw, cb, dtb, alog, dsk, ng):
    db = z.shape[0]
    tb = math.gcd(db, 8)
    vm = lambda s: pltpu.VMEM(s, F32)
    return pl.pallas_call(
        _mamba_sample_kernel, grid=(db // tb,),
        in_specs=[_full(z.shape), _full(xbc.shape), _full(sm.shape), _full(sc_t.shape),
                  pl.BlockSpec((tb, SSM_WIDTH, D_STATE), lambda b: (b, 0, 0)),
                  _full(cw.shape), _full(cb.shape), _full(dtb.shape), _full(alog.shape), _full(dsk.shape),
                  _full(ng.shape)],
        out_specs=(_full((db, 512)), _full((CONV_W - 1, db, CONV_DIM)),
                   pl.BlockSpec((tb, SSM_WIDTH, D_STATE), lambda b: (b, 0, 0))),
        out_shape=(jax.ShapeDtypeStruct((db, 512), BF16),
                   jax.ShapeDtypeStruct((CONV_W - 1, db, CONV_DIM), F32),
                   jax.ShapeDtypeStruct((db, SSM_WIDTH, D_STATE), F32)),
        scratch_shapes=[vm((SSM_WIDTH, db)), vm((SSM_WIDTH, db)), vm((db, 256)), vm((db, 256)),
                        vm((db, SSM_WIDTH)), vm((db, SSM_WIDTH)), vm((db, SSM_WIDTH)), vm((db, SSM_WIDTH))],
        compiler_params=_cparams(("arbitrary",)), name="mamba_sample",
    )(z, xbc, sm, sc_t, h0, cw, cb, dtb, alog, dsk, ng)


def _stride_block_proj(x_ref, n_blocks, w_ref):
    acc = jnp.zeros((n_blocks, 512), F32)
    for j in range(0, STRIDE, 2):
        xj = jnp.concatenate([x_ref[pl.ds(j + d, n_blocks, stride=STRIDE), :].astype(BF16) for d in range(2)], axis=1)
        acc = acc + _dot(xj, w_ref[pl.ds(j * LANES, 2 * LANES), :])
    return acc


def _cmp_proj_kernel(xk_ref, xv_ref, wk_ref, wv_ref, hk_ref, hv_ref):
    n_blocks = xk_ref.shape[0] // STRIDE
    hk_ref[...] = _stride_block_proj(xk_ref, n_blocks, wk_ref)
    hv_ref[...] = _stride_block_proj(xv_ref, n_blocks, wv_ref)


def _cmp_const(pe_ref, w1f_ref, b1_ref):
    c = _dot(pe_ref[...], w1f_ref[...])[0:1]
    return jnp.concatenate([c, c], axis=1) + b1_ref[...]


def _cmp_mlp(h0, h1_next, const, w2_ref, b2_ref):
    pre = h0 + h1_next + const
    return _dot(_silu(pre).astype(BF16), w2_ref[...]) + b2_ref[...]


def _cmp_finish_kernel(hk_ref, hv_ref, pek, w1k, b1k, w2k, b2k, pev, w1v, b1v, w2v, b2v, kcc_ref, vcc_ref):
    n = hk_ref.shape[0]
    for h_ref, pe, w1f, b1, w2, b2, o_ref in ((hk_ref, pek, w1k, b1k, w2k, b2k, kcc_ref),
                                              (hv_ref, pev, w1v, b1v, w2v, b2v, vcc_ref)):
        h = h_ref[...]
        h1_next = pltpu.roll(h[:, 256:512], n - 1, 0)
        o_ref[...] = _cmp_mlp(h[:, 0:256], h1_next, _cmp_const(pe, w1f, b1), w2, b2).astype(BF16)


def _compress_prompt(kv4, cw):
    t = kv4.shape[0]
    rows = min(t, 2048)
    n_blocks = t // STRIDE
    hk, hv = pl.pallas_call(
        _cmp_proj_kernel, grid=(t // rows,),
        in_specs=[pl.BlockSpec((rows, LANES), lambda i: (i, 0)), pl.BlockSpec((rows, LANES), lambda i: (i, 1)),
                  _full(cw["wk"].shape), _full(cw["wv"].shape)],
        out_specs=(pl.BlockSpec((rows // STRIDE, 512), lambda i: (i, 0)),) * 2,
        out_shape=(jax.ShapeDtypeStruct((n_blocks, 512), F32),) * 2,
        compiler_params=_cparams(("arbitrary",)), name="cmp_proj",
    )(kv4, kv4, cw["wk"], cw["wv"])
    names = ("pek", "w1k", "b1k", "w2k", "b2k", "pev", "w1v", "b1v", "w2v", "b2v")
    args = [cw[k] for k in names]
    return pl.pallas_call(
        _cmp_finish_kernel, grid=(1,),
        in_specs=[_full(hk.shape), _full(hv.shape)] + [_full(a.shape) for a in args],
        out_specs=(_full((n_blocks, 128)),) * 2,
        out_shape=(jax.ShapeDtypeStruct((n_blocks, 128), BF16),) * 2,
        compiler_params=_cparams(("arbitrary",)), name="cmp_finish",
    )(hk, hv, *args)


def _softmax_rows(s):
    m = jnp.max(s, -1, keepdims=True)
    m = jnp.where(jnp.isfinite(m), m, 0.0)
    e = jnp.exp(s - m)
    return e / jnp.maximum(jnp.sum(e, -1, keepdims=True), 1e-30)


def _knock_out_top(sc, blkf, n_pick):
    for _ in range(n_pick):
        m = jnp.max(sc, 0, keepdims=True)
        idx = jnp.min(jnp.where(sc == m, blkf, 1e9), 0, keepdims=True)
        sc = jnp.where(blkf == idx, -jnp.inf, sc)
    return sc


def _nsa_prompt_kernel(qun_ref, qrot_ref, sm_ref, kcc_ref, vcc_ref, att_ref, wsel_ref, eall_ref, o_ref,
                       m_scr, l_scr, acc_scr, sel_scr, bias_scr, p_scr, imp_scr, ocmp_scr, biasw_scr, pw_scr, owin_scr):
    i = pl.program_id(0)
    s0 = i * Q_BLOCK
    ncp = kcc_ref.shape[0]
    nb = wsel_ref.shape[0]
    kt = min(SEL_KEY_TILE, att_ref.shape[0])
    n_top = min(N_SEL, nb)

    def head(ref, h):
        return ref[:, h * LANES:(h + 1) * LANES]

    def stacked(ref, h0, n):
        return jnp.concatenate([head(ref, h0 + j) for j in range(n)], axis=0)

    def rows(x, j):
        return x[j * Q_BLOCK:(j + 1) * Q_BLOCK]

    rg = SOFTMAX_ROW_GROUP

    def compress_and_select(nc, nbl):
        colc = lax.broadcasted_iota(jnp.int32, (Q_BLOCK, nc), 1)
        rowq = lax.broadcasted_iota(jnp.int32, (Q_BLOCK, nc), 0) + s0
        bias_scr[:, 0:nc] = jnp.where(colc * STRIDE + (L_CMP - 1) <= rowq, 0.0, -jnp.inf)
        s_all = _dot_nt(stacked(qun_ref, 0, N_HEADS), kcc_ref[0:nc, :])
        for kv in range(N_KV):
            for r in range(Q_BLOCK // rg):
                bias = bias_scr[r * rg:(r + 1) * rg, 0:nc]
                imp = None
                for g in range(GQA):
                    r0 = (kv * GQA + g) * Q_BLOCK + r * rg
                    p = _softmax_rows(s_all[r0:r0 + rg] + bias)
                    p_scr[r0:r0 + rg, 0:nc] = p.astype(BF16)
                    imp = p if imp is None else imp + p
                imp_scr[kv, r * rg:(r + 1) * rg, 0:nc] = imp
        ocmp_scr[...] = _dot(p_scr[:, 0:nc], vcc_ref[0:nc, :])

        blk = lax.broadcasted_iota(jnp.int32, (nbl, Q_BLOCK), 0)
        qpl = lax.broadcasted_iota(jnp.int32, (nbl, Q_BLOCK), 1) + s0
        cur = qpl >> 6
        valid = blk * L_SEL <= qpl
        forced = (blk == 0) | ((blk <= cur) & (blk > cur - N_LOCAL))
        blkf = blk.astype(F32)
        wsel = wsel_ref[0:nbl, 0:nc]
        scs = []
        for kv in range(N_KV):
            hi, mid, lo = _split3(imp_scr[kv, :, 0:nc])
            sc = _dot_nt(wsel, hi) + _dot_nt(wsel, mid) + _dot_nt(wsel, lo)
            scs.append(jnp.where(forced, -jnp.inf, jnp.where(valid, sc, -jnp.inf)))
        sc2 = _knock_out_top(jnp.concatenate(scs, axis=1), jnp.concatenate([blkf, blkf], axis=1),
                             n_top - 1 - N_LOCAL)
        nbl_pad = min(-(-nbl // LANES) * LANES, nb)
        for kv in range(N_KV):
            picked = valid & (sc2[:, kv * Q_BLOCK:(kv + 1) * Q_BLOCK] == -jnp.inf)
            bias_t = jnp.where(picked, 0.0, NEG)
            if nbl_pad > nbl:
                bias_t = jnp.concatenate([bias_t, jnp.full((nbl_pad - nbl, Q_BLOCK), NEG, F32)], axis=0)
            sel_scr[kv * Q_BLOCK:(kv + 1) * Q_BLOCK, 0:nbl_pad] = bias_t.T.astype(BF16)
        if nbl_pad < nb:
            sel_scr[:, nbl_pad:nb] = jnp.full((N_KV * Q_BLOCK, nb - nbl_pad), NEG, BF16)

    n_qblk = pl.num_programs(0)
    n_var = 4 if ncp % (4 * LANES) == 0 else 1
    for v in range(n_var):
        nc_v = (v + 1) * ncp // n_var
        nb_v = (v + 1) * nb // n_var

        @pl.when((i * n_var) // n_qblk == v)
        def _():
            compress_and_select(nc_v, nb_v)

    o_cmp = [rows(ocmp_scr[...], h) for h in range(N_HEADS)]

    wk = WINDOW + Q_BLOCK
    w0 = pl.multiple_of(jnp.maximum(s0 - WINDOW, 0), Q_BLOCK)
    kw = att_ref[pl.ds(w0, wk), 256:384]
    vw = att_ref[pl.ds(w0, wk), 384:512]
    kposw = lax.broadcasted_iota(jnp.int32, (Q_BLOCK, wk), 1) + w0
    roww = lax.broadcasted_iota(jnp.int32, (Q_BLOCK, wk), 0) + s0
    okw = (kposw <= roww) & (kposw > roww - WINDOW)
    biasw_scr[...] = jnp.where(okw, 0.0, -jnp.inf)
    s_all = _dot_nt(stacked(qrot_ref, 0, N_HEADS), kw)
    for h in range(N_HEADS):
        for r in range(Q_BLOCK // rg):
            r0 = h * Q_BLOCK + r * rg
            p = _softmax_rows(s_all[r0:r0 + rg] + biasw_scr[r * rg:(r + 1) * rg, :])
            pw_scr[r0:r0 + rg, :] = p.astype(BF16)
    owin_scr[...] = _dot(pw_scr[...], vw)

    o_sel, o_win = [], []
    rowk = lax.broadcasted_iota(jnp.int32, (Q_BLOCK, kt), 0) + s0
    lanek = lax.broadcasted_iota(jnp.int32, (Q_BLOCK, kt), 1)
    n_tiles = (s0 + Q_BLOCK - 1) // kt + 1
    nch = kt // LANES
    bgrp = min(LANES, nb)

    def lane_chunks(x):
        return [x[:, c * LANES:(c + 1) * LANES] for c in range(nch)]

    for h in range(N_HEADS):
        m_scr[h] = jnp.full((Q_BLOCK, LANES), NEG, F32)
        l_scr[h] = jnp.zeros((Q_BLOCK, LANES), F32)
        acc_scr[h] = jnp.zeros((Q_BLOCK, LANES), F32)
    q8 = stacked(qrot_ref, 0, N_HEADS)

    def tile_body(t, carry, diagonal):
        k0 = pl.multiple_of(t * kt, kt)
        ks = att_ref[pl.ds(k0, kt), 0:128]
        vs = att_ref[pl.ds(k0, kt), 128:256]
        b0 = pl.multiple_of(((t * (kt // L_SEL)) // bgrp) * bgrp, bgrp)
        bias2 = _dot(sel_scr[:, pl.ds(b0, bgrp)], eall_ref[pl.ds(b0, bgrp), pl.ds(k0, kt)])
        biases = [rows(bias2, kv) for kv in range(N_KV)]
        if diagonal:
            biases = [jnp.where(lanek + k0 <= rowk, b, NEG) for b in biases]
        s8 = _dot_nt(q8, ks)
        ps, alphas = [], []
        for h in range(N_HEADS):
            s = rows(s8, h) + biases[h // GQA]
            m_old = m_scr[h]
            m_new = jnp.maximum(m_old, jnp.max(functools.reduce(jnp.maximum, lane_chunks(s)), -1, keepdims=True))
            alpha = jnp.exp(m_old - m_new)
            p = jnp.exp(s - jnp.tile(m_new, (1, nch)))
            l_scr[h] = alpha * l_scr[h] + functools.reduce(jnp.add, lane_chunks(p))
            m_scr[h] = m_new
            ps.append(p.astype(BF16))
            alphas.append(alpha)
        pv = _dot(jnp.concatenate(ps, axis=0), vs)
        for h in range(N_HEADS):
            acc_scr[h] = alphas[h] * acc_scr[h] + rows(pv, h)
        return carry

    lax.fori_loop(0, n_tiles - 1, functools.partial(tile_body, diagonal=False), 0)
    tile_body(n_tiles - 1, 0, diagonal=True)
    for h in range(N_HEADS):
        o_sel.append(acc_scr[h] / jnp.sum(l_scr[h], -1, keepdims=True))

    o_win = [rows(owin_scr[...], h) for h in range(N_HEADS)]

    sm = sm_ref[...]
    lane = lax.broadcasted_iota(jnp.int32, (Q_BLOCK, LANES), 1)
    chunks = []
    for c in range(N_HEADS // 2):
        parts = []
        for h in (2 * c, 2 * c + 1):
            kv = h // GQA
            gc = M_HEADS + 3 * h
            a = sm[:, gc:gc + 1] * o_cmp[h] + sm[:, gc + 1:gc + 2] * o_sel[h] + sm[:, gc + 2:gc + 3] * o_win[h]
            a = jnp.where((lane >= kv * HEAD_DIM) & (lane < (kv + 1) * HEAD_DIM), a, 0.0)
            if kv != h % 2:
                a = pltpu.roll(a, HEAD_DIM, 1)
            parts.append(a)
        chunks.append(parts[0] + parts[1])
    o_ref[...] = jnp.concatenate(chunks, axis=1)


def _nsa_prompt(qun, qrot, sm, kcc, vcc, att, wsel, eall):
    t = qun.shape[0]
    ncp = kcc.shape[0]
    wk = WINDOW + Q_BLOCK
    row = lambda w: pl.BlockSpec((Q_BLOCK, w), lambda i: (i, 0))
    return pl.pallas_call(
        _nsa_prompt_kernel, grid=(t // Q_BLOCK,),
        in_specs=[row(1024), row(1024), row(128), _resident(kcc.shape), _resident(vcc.shape),
                  _resident(att.shape), _resident(wsel.shape), _resident(eall.shape)],
        out_specs=row(512),
        out_shape=jax.ShapeDtypeStruct((t, 512), F32),
        scratch_shapes=[pltpu.VMEM((N_HEADS, Q_BLOCK, LANES), F32)] * 3 + [
            pltpu.VMEM((N_KV * Q_BLOCK, wsel.shape[0]), BF16),
            pltpu.VMEM((Q_BLOCK, ncp), F32),
            pltpu.VMEM((N_HEADS * Q_BLOCK, ncp), BF16),
            pltpu.VMEM((N_KV, Q_BLOCK, ncp), F32),
            pltpu.VMEM((N_HEADS * Q_BLOCK, LANES), F32),
            pltpu.VMEM((Q_BLOCK, wk), F32),
            pltpu.VMEM((N_HEADS * Q_BLOCK, wk), BF16),
            pltpu.VMEM((N_HEADS * Q_BLOCK, LANES), F32)],
        compiler_params=_cparams(("arbitrary",)), name="nsa_prompt",
    )(qun, qrot, sm, kcc, vcc, att, wsel, eall)


def _out_kernel(x_ref, m_ref, a_ref, ang_ref, wom_ref, woa_ref, l1g_ref, l1b_ref, w1_ref, w2_ref,
                l2g_ref, l2b_ref, y_ref):
    a = a_ref[...]
    an = a * lax.rsqrt(jnp.mean(a * a, -1, keepdims=True) + RMS_EPS) * ang_ref[...]
    mix = _dot(m_ref[...], wom_ref[...]) + _dot(an.astype(BF16), woa_ref[...])
    h = _layer_norm(DEEPNORM_ALPHA * x_ref[...] + mix, l1g_ref[...], l1b_ref[...])
    hb = h.astype(BF16)
    f = jnp.zeros(h.shape, F32)
    for c in range(D_FF // 1024):
        cs = slice(c * 1024, (c + 1) * 1024)
        u = jnp.maximum(_dot(hb, w1_ref[:, cs]), 0.0)
        f = f + _dot((u * u).astype(BF16), w2_ref[cs, :])
    y_ref[...] = _layer_norm(DEEPNORM_ALPHA * h + f, l2g_ref[...], l2b_ref[...])


def _output_block(x, m_out, a_out, att, ow, tm):
    t = x.shape[0]
    row = lambda w: pl.BlockSpec((tm, w), lambda i: (i, 0))
    names = ("l1g", "l1b", "w1", "w2", "l2g", "l2b")
    args = [att["ang"], ow["wom"], att["woa"]] + [ow[k] for k in names]
    return pl.pallas_call(
        _out_kernel, grid=(t // tm,),
        in_specs=[row(D_MODEL), row(512), row(a_out.shape[1])] + [_resident(a.shape) for a in args],
        out_specs=row(D_MODEL),
        out_shape=jax.ShapeDtypeStruct((t, D_MODEL), F32),
        compiler_params=_cparams(("arbitrary",)), name="out_block",
    )(x, m_out, a_out, *args)


def _sample_cmp_kernel(pt_ref, cache_ref, new_ref, q_ref, wk_ref, wv_ref,
                       pek, w1k, b1k, w2k, b2k, pev, w1v, b1v, w2v, b2v, wsel_ref,
                       ocmp_ref, score_ref, buf, xs, sem, *, n_pages, past_len):
    b = pl.program_id(0)
    nb = pl.num_programs(0)
    n_blocks = n_pages * PAGE_SIZE // STRIDE

    def page_copy(bb, slot, pg):
        return pltpu.make_async_copy(cache_ref.at[pt_ref[bb * n_pages + pg], pl.ds(0, 2)],
                                     buf.at[slot, pg], sem.at[slot])

    def start_all(bb, slot):
        def body(pg, carry):
            page_copy(bb, slot, pg).start()
            return carry
        lax.fori_loop(0, n_pages, body, 0)

    def wait_all(bb, slot):
        def body(pg, carry):
            page_copy(bb, slot, pg).wait()
            return carry
        lax.fori_loop(0, n_pages, body, 0)

    slot = b % 2

    @pl.when(b == 0)
    def _():
        start_all(0, 0)

        @pl.when(nb > 1)
        def _():
            start_all(1, 1)

    wait_all(b, slot)

    def to_rows(pg, carry):
        r0 = pl.multiple_of(pg * PAGE_SIZE, PAGE_SIZE)
        for c in range(2):
            xs[c, pl.ds(r0, PAGE_SIZE), :] = buf[slot, pg, c].astype(BF16).T.astype(F32)
        return carry
    lax.fori_loop(0, n_pages, to_rows, 0, unroll=math.gcd(n_pages, 8))

    @pl.when(b + 2 < nb)
    def _():
        start_all(b + 2, slot)

    new = jnp.broadcast_to(new_ref[0], (8, 512))
    rowi = lax.broadcasted_iota(jnp.int32, (n_blocks, 256), 0)
    toks = []
    for c, w_ref, pe, w1f, b1, w2, b2 in ((0, wk_ref, pek, w1k, b1k, w2k, b2k),
                                          (1, wv_ref, pev, w1v, b1v, w2v, b2v)):
        h = _stride_block_proj(xs.at[c], n_blocks, w_ref)
        h_new = _dot(new[:, c * LANES:(c + 1) * LANES].astype(BF16), w_ref[0:LANES, 256:512])[0:1]
        h1_next = jnp.where(rowi == n_blocks - 1, h_new, pltpu.roll(h[:, 256:512], n_blocks - 1, 0))
        toks.append(_cmp_mlp(h[:, 0:256], h1_next, _cmp_const(pe, w1f, b1), w2, b2).astype(BF16))
    kcc, vcc = toks

    q = q_ref[0]
    zq = jnp.zeros_like(q)
    rowh = lax.broadcasted_iota(jnp.int32, (N_HEADS, LANES), 0)
    qpad = jnp.where(rowh < GQA, jnp.concatenate([q, zq], 1), jnp.concatenate([zq, q], 1))
    s = _dot_nt(qpad, kcc)
    colc = lax.broadcasted_iota(jnp.int32, s.shape, 1)
    p = _softmax_rows(jnp.where(colc * STRIDE + (L_CMP - 1) <= past_len, s, -jnp.inf))
    o = _dot(p.astype(BF16), vcc)
    ocmp_ref[0] = jnp.where(rowh[:, 0:HEAD_DIM] < GQA, o[:, 0:HEAD_DIM], o[:, HEAD_DIM:])
    imp0 = p[0:1] + p[1:2] + p[2:3] + p[3:4]
    imp1 = p[4:5] + p[5:6] + p[6:7] + p[7:8]
    rowp = lax.broadcasted_iota(jnp.int32, p.shape, 0)
    imp = jnp.where(rowp == 0, imp0, jnp.where(rowp == 1, imp1, 0.0))
    score_ref[0] = _dot3(imp, wsel_ref[...])


def _sample_cmp(page_table, cache4, kv4_s, q8, cw, wsel_s, past_len):
    db, n_pages = page_table.shape
    names = ("pek", "w1k", "b1k", "w2k", "b2k", "pev", "w1v", "b1v", "w2v", "b2v")
    args = [cw[k] for k in names]
    nbp = wsel_s.shape[1]
    full = lambda a: pl.BlockSpec(a.shape, lambda b, pt: (0,) * a.ndim)
    grid_spec = pltpu.PrefetchScalarGridSpec(
        num_scalar_prefetch=1, grid=(db,),
        in_specs=[pl.BlockSpec(memory_space=pl.ANY),
                  pl.BlockSpec((1, 1, 512), lambda b, pt: (b, 0, 0)),
                  pl.BlockSpec((1, N_HEADS, HEAD_DIM), lambda b, pt: (b, 0, 0)),
                  full(cw["wk"]), full(cw["wv"])] + [full(a) for a in args] + [full(wsel_s)],
        out_specs=(pl.BlockSpec((1, N_HEADS, HEAD_DIM), lambda b, pt: (b, 0, 0)),
                   pl.BlockSpec((1, 8, nbp), lambda b, pt: (b, 0, 0))),
        scratch_shapes=[pltpu.VMEM((2, n_pages, 2, LANES, PAGE_SIZE), F32),
                        pltpu.VMEM((2, n_pages * PAGE_SIZE, LANES), F32),
                        pltpu.SemaphoreType.DMA((2,))],
    )
    return pl.pallas_call(
        functools.partial(_sample_cmp_kernel, n_pages=n_pages, past_len=past_len),
        grid_spec=grid_spec,
        out_shape=(jax.ShapeDtypeStruct((db, N_HEADS, HEAD_DIM), F32), jax.ShapeDtypeStruct((db, 8, nbp), F32)),
        compiler_params=_cparams(("arbitrary",)), name="sample_cmp",
    )(page_table.reshape(-1), cache4, kv4_s.reshape(db, 1, 512), q8, cw["wk"], cw["wv"], *args, wsel_s)


def _sample_topk_kernel(score_ref, idx_ref, *, n_slc, qp):
    sc = score_ref[...]
    blk = lax.broadcasted_iota(jnp.int32, sc.shape, 1)
    cur = qp // L_SEL
    valid = (blk * L_SEL <= qp) & (blk < n_slc)
    forced = (blk == 0) | ((blk <= cur) & (blk > cur - N_LOCAL))
    sc = jnp.where(forced & (blk < n_slc), jnp.inf, jnp.where(valid, sc, -jnp.inf))
    real = blk < n_slc
    blkf = blk.astype(F32)
    taken = jnp.where(real, 0.0, 1.0)
    out = jnp.zeros(idx_ref.shape, F32)
    lane = lax.broadcasted_iota(jnp.int32, idx_ref.shape, 1)
    for it in range(min(N_SEL, n_slc)):
        m = jnp.max(jnp.where(taken == 0.0, sc, -jnp.inf), -1, keepdims=True)
        cand = (sc == m) & (taken == 0.0)
        idx = jnp.min(jnp.where(cand, blkf, 1e9), -1, keepdims=True)
        pick = blkf == idx
        taken = jnp.where(pick, 1.0, taken)
        out = jnp.where(lane == it, idx, out)
    idx_ref[...] = out.astype(jnp.int32)


def _sample_topk(scores, n_slc, qp):
    rows, nbp = scores.shape
    return pl.pallas_call(
        functools.partial(_sample_topk_kernel, n_slc=n_slc, qp=qp), grid=(1,),
        in_specs=[_full(scores.shape)], out_specs=_full((rows, LANES)),
        out_shape=jax.ShapeDtypeStruct((rows, LANES), jnp.int32),
        compiler_params=_cparams(("arbitrary",)), name="sample_topk",
    )(scores)


def _sample_sel_kernel(pt_ref, idx_ref, cache_ref, q_ref, new_ref, osel_ref, buf, sem, *, n_pages, past_len, n_top):
    b = pl.program_id(0)
    nb = pl.num_programs(0)
    per_page = PAGE_SIZE // L_SEL
    n_past_blocks = past_len // L_SEL
    n_cp = N_KV * n_top

    def blk_copy(bb, slot, j):
        kv = j // n_top
        n = j - kv * n_top
        blk = jnp.minimum(idx_ref[bb * n_cp + j], n_past_blocks - 1)
        page = pt_ref[bb * n_pages + blk // per_page]
        return pltpu.make_async_copy(
            cache_ref.at[page, pl.ds(2, 2), pl.ds(pl.multiple_of(kv * HEAD_DIM, HEAD_DIM), HEAD_DIM), :],
            buf.at[slot, kv, :, :, pl.ds(pl.multiple_of(n * PAGE_SIZE, PAGE_SIZE), PAGE_SIZE)], sem.at[slot])

    def start_all(bb, slot):
        def body(j, c):
            blk_copy(bb, slot, j).start()
            return c
        lax.fori_loop(0, n_cp, body, 0)

    def wait_all(bb, slot):
        def body(j, c):
            blk_copy(bb, slot, j).wait()
            return c
        lax.fori_loop(0, n_cp, body, 0)

    slot = b % SEL_SLOTS

    @pl.when(b == 0)
    def _():
        for d in range(SEL_SLOTS - 1):
            @pl.when(d < nb)
            def _():
                start_all(d, d)

    @pl.when(b + SEL_SLOTS - 1 < nb)
    def _():
        start_all(b + SEL_SLOTS - 1, (b + SEL_SLOTS - 1) % SEL_SLOTS)

    wait_all(b, slot)

    q = q_ref[0]
    new = jnp.broadcast_to(new_ref[0], (8, 512))
    nk = n_top * PAGE_SIZE
    lane = lax.broadcasted_iota(jnp.int32, (N_HEADS, nk), 1)
    outs = []
    for kv in range(N_KV):
        ks_new = new[:, 256 + kv * HEAD_DIM:256 + (kv + 1) * HEAD_DIM].astype(BF16)
        vs_new = new[:, 384 + kv * HEAD_DIM:384 + (kv + 1) * HEAD_DIM].astype(BF16).astype(F32)
        s_new = _dot_nt(q, ks_new)[:, 0:1]
        kt = buf[slot, kv, 0].astype(BF16)
        vt = buf[slot, kv, 1].astype(BF16)
        blk_of_lane = jnp.zeros((N_HEADS, nk), jnp.int32)
        for n in range(n_top):
            blk_of_lane = jnp.where((lane >> 7) == n, idx_ref[b * n_cp + kv * n_top + n], blk_of_lane)
        in_block = ((lane & (PAGE_SIZE - 1)) >> 6) == (blk_of_lane & (per_page - 1))
        ok = in_block & (blk_of_lane < n_past_blocks)
        s = jnp.where(ok, _dot(q, kt), -jnp.inf)
        m = jnp.maximum(jnp.max(s, -1, keepdims=True), s_new)
        e = jnp.exp(s - m)
        e_new = jnp.exp(s_new - m)
        denom = jnp.sum(e, -1, keepdims=True) + e_new
        o = _dot_nt((e / denom).astype(BF16), vt) + (e_new / denom).astype(BF16).astype(F32) * vs_new
        outs.append(o)
    row = lax.broadcasted_iota(jnp.int32, (N_HEADS, HEAD_DIM), 0)
    osel_ref[0] = jnp.where(row < GQA, outs[0], outs[1])


def _sample_sel(page_table, idx, cache4, qrot8, kv4_s, past_len, n_top):
    db, n_pages = page_table.shape
    assert PAGE_SIZE // L_SEL == 2
    hb = pl.BlockSpec((1, N_HEADS, HEAD_DIM), lambda b, pt, ix: (b, 0, 0))
    grid_spec = pltpu.PrefetchScalarGridSpec(
        num_scalar_prefetch=2, grid=(db,),
        in_specs=[pl.BlockSpec(memory_space=pl.ANY), hb,
                  pl.BlockSpec((1, 1, 512), lambda b, pt, ix: (b, 0, 0))],
        out_specs=hb,
        scratch_shapes=[pltpu.VMEM((SEL_SLOTS, N_KV, 2, HEAD_DIM, n_top * PAGE_SIZE), F32),
                        pltpu.SemaphoreType.DMA((SEL_SLOTS,))],
    )
    return pl.pallas_call(
        functools.partial(_sample_sel_kernel, n_pages=n_pages, past_len=past_len, n_top=n_top),
        grid_spec=grid_spec,
        out_shape=jax.ShapeDtypeStruct((db, N_HEADS, HEAD_DIM), F32),
        compiler_params=_cparams(("arbitrary",)), name="sample_sel",
    )(page_table.reshape(-1), idx.reshape(-1), cache4, qrot8, kv4_s.reshape(db, 1, 512))


def _sample_win_kernel(st_ref, newrow_ref, newall_ref, q_ref, g_ref, ocmp_ref, osel_ref, a_ref, wout_ref):
    i = pl.program_id(0)
    tb, wb = st_ref.shape[0], st_ref.shape[3]
    db = newall_ref.shape[0]
    new_t = newall_ref[...].T
    rowb = lax.broadcasted_iota(jnp.int32, (db, LANES), 0)
    lane_w = lax.broadcasted_iota(jnp.int32, (HEAD_DIM, wb), 1)
    lane_s = lax.broadcasted_iota(jnp.int32, (N_HEADS, wb), 1)
    rowh = lax.broadcasted_iota(jnp.int32, (N_HEADS, HEAD_DIM), 0)
    for j in range(tb):
        pick = jnp.where(rowb == i * tb + j, 1.0, 0.0).astype(BF16)
        col = _dot3(new_t, pick)
        for ckv in range(2 * N_KV):
            colw = jnp.tile(col[ckv * HEAD_DIM:(ckv + 1) * HEAD_DIM], (1, wb // LANES))
            wout_ref[j, ckv] = jnp.where(lane_w == wb - 1, colw, pltpu.roll(st_ref[j, ckv], wb - 1, 1))
        new = jnp.broadcast_to(newrow_ref[j], (8, 256))
        q = q_ref[j]
        outs = []
        for kv in range(N_KV):
            kt = st_ref[j, kv].astype(BF16)
            vt = st_ref[j, N_KV + kv].astype(BF16)
            kw_new = new[:, kv * HEAD_DIM:(kv + 1) * HEAD_DIM].astype(BF16)
            vw_new = new[:, 128 + kv * HEAD_DIM:128 + (kv + 1) * HEAD_DIM].astype(BF16).astype(F32)
            s_new = _dot_nt(q, kw_new)[:, 0:1]
            s = jnp.where(lane_s >= 1, _dot(q, kt), -jnp.inf)
            m = jnp.maximum(jnp.max(s, -1, keepdims=True), s_new)
            e = jnp.exp(s - m)
            e_new = jnp.exp(s_new - m)
            denom = jnp.sum(e, -1, keepdims=True) + e_new
            outs.append(_dot_nt((e / denom).astype(BF16), vt) + (e_new / denom).astype(BF16).astype(F32) * vw_new)
        o_win = jnp.where(rowh < GQA, outs[0], outs[1])
        g = g_ref[j]
        a_ref[j] = g[:, 0:1] * ocmp_ref[j] + g[:, 1:2] * osel_ref[j] + g[:, 2:3] * o_win


def _sample_win(state4, win_new, qrot8, gates8, ocmp, osel, tb):
    db, _, _, wb = state4.shape
    b3 = lambda s: pl.BlockSpec((tb,) + s, lambda i: (i, 0, 0))
    b4 = pl.BlockSpec((tb, 2 * N_KV, HEAD_DIM, wb), lambda i: (i, 0, 0, 0))
    hd = (N_HEADS, HEAD_DIM)
    return pl.pallas_call(
        _sample_win_kernel, grid=(db // tb,),
        in_specs=[b4, b3((1, 256)), _full(win_new.shape), b3(hd), b3((8, 128)), b3(hd), b3(hd)],
        out_specs=(b3(hd), b4),
        out_shape=(jax.ShapeDtypeStruct((db,) + hd, F32), jax.ShapeDtypeStruct(state4.shape, F32)),
        compiler_params=_cparams(("arbitrary",)), name="sample_win",
    )(state4, win_new.reshape(db, 1, 256), win_new, qrot8, gates8, ocmp, osel)


def _rope_tables(pos):
    half = ROT_DIM // 2
    inv = ROPE_THETA ** (-jnp.arange(half, dtype=F32) * 2.0 / ROT_DIM)
    ang = pos.astype(F32)[:, None] * inv[None, :]
    cos, sin = jnp.cos(ang), jnp.sin(ang)
    n = pos.shape[0]
    rest = HEAD_DIM - ROT_DIM
    one, zero, z8 = jnp.ones((n, rest), F32), jnp.zeros((n, rest), F32), jnp.zeros((n, half), F32)
    c = jnp.concatenate([cos, cos, one], 1)
    s1 = jnp.concatenate([z8, sin, zero], 1)
    s2 = jnp.concatenate([-sin, z8, zero], 1)
    return tuple(jnp.tile(a, (1, 2)) for a in (c, s1, s2))


def _proj_weights(w_in):
    sizes = (SSM_WIDTH, CONV_DIM, M_HEADS, ATT_WIDTH, 6 * N_KV * HEAD_DIM, 3 * N_HEADS)
    parts, start = [], 0
    for s in sizes:
        parts.append(w_in[:, start:start + s])
        start += s
    wz, wxbc, wdt, wq, wkv, wg = parts
    zeros = jnp.zeros((D_MODEL, HEAD_DIM), w_in.dtype)
    chunks = []
    for h in range(N_HEADS):
        wh = wq[:, h * HEAD_DIM:(h + 1) * HEAD_DIM]
        chunks += [wh, zeros] if h // GQA == 0 else [zeros, wh]
    wq_pad = jnp.concatenate(chunks, 1)
    wsm = jnp.concatenate([wdt, wg, jnp.zeros((D_MODEL, LANES - M_HEADS - 3 * N_HEADS), w_in.dtype)], 1)
    wz, wxbc, wq, wq_pad, wkv, wsm = (w.astype(BF16) for w in (wz, wxbc, wq, wq_pad, wkv, wsm))
    return (wz, wxbc, wq_pad, wkv, wsm), (wz, wxbc, wq, wkv, wsm)


def _cmp_weights(pe_cmp, w1_cmp, b1_cmp, w2_cmp, b2_cmp):
    out = {}
    for i, nm in enumerate("kv"):
        w1 = w1_cmp[i].reshape(2, STRIDE, HEAD_DIM, CMP_HIDDEN)
        wbig = jnp.zeros((STRIDE, LANES, 512), F32)
        for r in range(2):
            for kv in range(N_KV):
                c0 = r * 256 + kv * CMP_HIDDEN
                wbig = wbig.at[:, kv * HEAD_DIM:(kv + 1) * HEAD_DIM, c0:c0 + CMP_HIDDEN].set(w1[r])
        out["w" + nm] = wbig.reshape(STRIDE * LANES, 512).astype(BF16)
        out["pe" + nm] = jnp.broadcast_to(pe_cmp[i].reshape(1, L_CMP * HEAD_DIM), (8, L_CMP * HEAD_DIM)).astype(BF16)
        out["w1" + nm] = w1_cmp[i].reshape(L_CMP * HEAD_DIM, CMP_HIDDEN).astype(BF16)
        out["b1" + nm] = jnp.tile(b1_cmp[i].reshape(1, CMP_HIDDEN), (1, 2))
        w2 = jnp.zeros((2 * CMP_HIDDEN, LANES), F32)
        for kv in range(N_KV):
            w2 = w2.at[kv * CMP_HIDDEN:(kv + 1) * CMP_HIDDEN, kv * HEAD_DIM:(kv + 1) * HEAD_DIM].set(w2_cmp[i])
        out["w2" + nm] = w2.astype(BF16)
        out["b2" + nm] = jnp.tile(b2_cmp[i].reshape(1, HEAD_DIM), (1, 2))
    return out


def _select_weights(n_blocks, n_cmp_padded):
    b = jnp.arange(n_blocks)[:, None]
    c = jnp.arange(n_cmp_padded)[None, :]
    k = c - 4 * b + 1
    w = jnp.where((k == 0) | (k == 4), 16.0, jnp.where((k >= 1) & (k <= 3), 32.0, 0.0))
    return w.astype(BF16)


def _pad_row(v, n=LANES):
    v = v.reshape(1, -1).astype(F32)
    return jnp.concatenate([v, jnp.zeros((1, n - v.shape[1]), F32)], 1)


def kernel(x_prompt, x_sample, cache_kv, state_win, state_ssm, state_conv, page_table, w_in, conv_w, conv_b,
           dt_bias, a_log, d_skip, ssm_norm_g, pe_cmp, w1_cmp, b1_cmp, w2_cmp, b2_cmp, att_norm_g, w_out,
           ln1_g, ln1_b, w_ff1, w_ff2, ln2_g, ln2_b):
    bsz, seq, _ = x_prompt.shape
    db, dseq, _ = x_sample.shape
    n_pages = page_table.shape[1]
    past_len = n_pages * PAGE_SIZE
    wb = state_win.shape[1]
    assert bsz == 1 and dseq == 1 and wb == WINDOW and past_len % L_SEL == 0 and seq % SEL_KEY_TILE == 0

    pw, pw_flat = _proj_weights(w_in)
    cw = _cmp_weights(pe_cmp, w1_cmp, b1_cmp, w2_cmp, b2_cmp)
    cb = conv_b.reshape(1, CONV_DIM)
    dtb, alog = _pad_row(dt_bias), _pad_row(a_log)
    dsk = jnp.repeat(d_skip, HEAD_DIM).reshape(1, SSM_WIDTH)
    ng = ssm_norm_g.reshape(1, SSM_WIDTH)
    ow = dict(wom=w_out[0:SSM_WIDTH].astype(BF16), l1g=ln1_g.reshape(1, -1),
              l1b=ln1_b.reshape(1, -1), w1=w_ff1.astype(BF16), w2=w_ff2.astype(BF16),
              l2g=ln2_g.reshape(1, -1), l2b=ln2_b.reshape(1, -1))
    woa = w_out[SSM_WIDTH:D_MODEL]
    ang = att_norm_g.reshape(ATT_WIDTH, 1)
    att_flat = dict(ang=ang.reshape(1, -1), woa=woa.astype(BF16))

    xp = x_prompt.reshape(seq, D_MODEL)
    tm = min(256, seq)
    z, xbc, qun, qrot, kv4, win, att, sm = _project(
        xp, pw, _rope_tables(jnp.arange(seq, dtype=jnp.int32)), tm, min(WINDOW, seq))
    m_out, conv_p, ssm_p = _mamba_prompt(z, xbc, sm, conv_w, cb, dtb, alog, dsk, ng)
    kcc, vcc = _compress_prompt(kv4, cw)
    n_slc = seq // L_SEL
    wsel = _select_weights(n_slc, seq // STRIDE)
    eall = (jnp.arange(n_slc)[:, None] == (jnp.arange(seq) // L_SEL)[None, :]).astype(BF16)
    a_out = _nsa_prompt(qun, qrot, sm, kcc, vcc, att, wsel, eall)
    y_prompt = _output_block(xp, m_out, a_out, att_flat, ow, tm).reshape(bsz, seq, D_MODEL)
    kv_prompt = kv4.reshape(bsz, seq, 4, N_KV, HEAD_DIM)
    win_prompt = win.reshape(bsz, min(WINDOW, seq), 2, N_KV, HEAD_DIM)
    ssm_prompt = ssm_p.reshape(bsz, M_HEADS, HEAD_DIM, D_STATE)
    conv_prompt = conv_p.reshape(bsz, CONV_W - 1, CONV_DIM)

    xs_ = x_sample.reshape(db, D_MODEL)
    pos_s = jnp.full((db,), past_len, jnp.int32)
    z, xbc, qun, qrot, kv4_s, win_s, _, sm = _project(xs_, pw_flat, _rope_tables(pos_s), db, db)
    m_out_s, conv_s, ssm_s = _mamba_sample(
        z, xbc, sm, jnp.transpose(state_conv, (1, 0, 2)), state_ssm.reshape(db, SSM_WIDTH, D_STATE),
        conv_w, cb, dtb, alog, dsk, ng)
    cache4 = jnp.transpose(cache_kv, (0, 2, 3, 4, 1)).reshape(cache_kv.shape[0], 4, N_KV * HEAD_DIM, PAGE_SIZE)
    state4 = jnp.transpose(state_win, (0, 2, 3, 4, 1)).reshape(db, 2 * N_KV, HEAD_DIM, wb)
    n_slc_s = -(-(past_len + dseq) // L_SEL)
    nbp = -(-n_slc_s // LANES) * LANES
    wsel_s = jnp.transpose(_select_weights(nbp, past_len // STRIDE))
    qun8 = qun.reshape(db, N_HEADS, HEAD_DIM)
    qrot8 = qrot.reshape(db, N_HEADS, HEAD_DIM)
    o_cmp, scores = _sample_cmp(page_table, cache4, kv4_s, qun8, cw, wsel_s, past_len)
    n_top = min(N_SEL, n_slc_s)
    idx = _sample_topk(scores[:, 0:N_KV, :].reshape(db * N_KV, nbp), n_slc_s, past_len)[:, 0:n_top]
    o_sel = _sample_sel(page_table, idx, cache4, qrot8, kv4_s, past_len, n_top)
    gates8 = jnp.pad(sm[:, M_HEADS:M_HEADS + 3 * N_HEADS].reshape(db, N_HEADS, 3), ((0, 0), (0, 0), (0, LANES - 3)))
    a8, win4 = _sample_win(state4, win_s, qrot8, gates8, o_cmp, o_sel, min(8, db))
    y_sample = _output_block(xs_, m_out_s, a8.reshape(db, ATT_WIDTH), att_flat, ow, db).reshape(db, dseq, D_MODEL)
    kv_sample = kv4_s.reshape(db, dseq, 4, N_KV, HEAD_DIM)
    win_sample = jnp.transpose(win4.reshape(db, 2, N_KV, HEAD_DIM, wb), (0, 4, 1, 2, 3))
    ssm_sample = ssm_s.reshape(db, M_HEADS, HEAD_DIM, D_STATE)
    conv_sample = jnp.transpose(conv_s, (1, 0, 2))

    return (y_prompt, y_sample, kv_prompt, win_prompt, ssm_prompt, conv_prompt,
            kv_sample, win_sample, ssm_sample, conv_sample)
```
